```python
import jax, jax.numpy as jnp
from jax import lax
import numpy as np

D_MODEL = 1024
BATCH = 2
SEQ = 8192
DEPTH = 1

MIX_WIDTH = D_MODEL
GLA_HEADS = 4
GLA_DK = 64
GLA_DV = 128
GLA_QK = GLA_HEADS * GLA_DK
GLA_V = GLA_HEADS * GLA_DV
GLA_GATE_RANK = 16
GLA_GATE_NORM = 16.0
HGRN_HEADS = 4
HGRN_DF = 128
HGRN_DV = 128
HGRN_F = HGRN_HEADS * HGRN_DF
HGRN_V = HGRN_HEADS * HGRN_DV
D_FF = 4 * D_MODEL
CHUNK = 64
EPS = 1e-6

IN_SPLITS = (GLA_QK, GLA_QK, GLA_V, GLA_V, GLA_GATE_RANK, HGRN_F, HGRN_F, HGRN_V, HGRN_V)
IN_COLS = sum(IN_SPLITS)

kernel_name = "hybrid_gla_hgrn2_sandwich_block"


def _split_points():
    pts, acc = [], 0
    for s in IN_SPLITS[:-1]:
        acc += s
        pts.append(acc)
    return pts


def rmsnorm(x, w):
    xf = x.astype(jnp.float32)
    y = xf * lax.rsqrt(jnp.mean(xf * xf, axis=-1, keepdims=True) + EPS)
    return (y * w.astype(jnp.float32)).astype(x.dtype)


def gated_head_norm(o, gate, w):
    B, T, H, dv = o.shape
    y = o * lax.rsqrt(jnp.mean(o * o, axis=-1, keepdims=True) + EPS)
    y = y.reshape(B, T, H * dv) * w.astype(jnp.float32)
    return (y * jax.nn.silu(gate.astype(jnp.float32))).astype(gate.dtype)


def chunked_gated_linear_attention(q, k, v, g):
    B, T, H, dk = q.shape
    dv = v.shape[-1]
    n = T // CHUNK

    def to_chunks(a):
        return a.astype(jnp.float32).reshape(B, n, CHUNK, H, a.shape[-1]).transpose(1, 0, 3, 2, 4)

    qc, kc, vc, gc = to_chunks(q), to_chunks(k), to_chunks(v), to_chunks(g)
    bc = jnp.cumsum(gc, axis=3)
    causal = jnp.tril(jnp.ones((CHUNK, CHUNK), dtype=bool))[:, :, None]

    def step(S, inp):
        qi, ki, vi, bi = inp
        rel = bi[:, :, :, None, :] - bi[:, :, None, :, :]
        decay = jnp.exp(jnp.where(causal, rel, -jnp.inf))
        scores = jnp.einsum('bhid,bhjd,bhijd->bhij', qi, ki, decay)
        o = jnp.einsum('bhij,bhjv->bhiv', scores, vi) + jnp.einsum('bhid,bhdv->bhiv', qi * jnp.exp(bi), S)
        b_last = bi[:, :, -1:, :]
        S = S * jnp.exp(b_last)[:, :, 0, :, None] + jnp.einsum('bhjd,bhjv->bhdv', ki * jnp.exp(b_last - bi), vi)
        return S, o

    S0 = jnp.zeros((B, H, dk, dv), jnp.float32)
    _, o = lax.scan(step, S0, (qc, kc, vc, bc))
    return o.transpose(1, 0, 3, 2, 4).reshape(B, T, H, dv)


def hybrid_mixer(h, w_in, w_gk_up, b_gk, gla_norm_w, hgrn_norm_w, lb, w_out):
    B, T, _ = h.shape
    z = h @ w_in
    gq, gk, gv, gg, glr, hq, hf, hi, hg = jnp.split(z, _split_points(), axis=-1)

    q_a = (gq * (GLA_DK ** -0.5)).reshape(B, T, GLA_HEADS, GLA_DK)
    k_a = gk.reshape(B, T, GLA_HEADS, GLA_DK)
    v_a = gv.reshape(B, T, GLA_HEADS, GLA_DV)
    log_a = jax.nn.log_sigmoid((glr @ w_gk_up + b_gk).astype(jnp.float32)) / GLA_GATE_NORM
    o_a = chunked_gated_linear_attention(q_a, k_a, v_a, log_a.reshape(B, T, GLA_HEADS, GLA_DK))
    o_a = gated_head_norm(o_a, gg, gla_norm_w)

    f = lb + (1.0 - lb) * jax.nn.sigmoid(hf.astype(jnp.float32))
    q_b = jax.nn.silu(hq.astype(jnp.float32)).reshape(B, T, HGRN_HEADS, HGRN_DF)
    k_b = (1.0 - f).reshape(B, T, HGRN_HEADS, HGRN_DF)
    v_b = hi.reshape(B, T, HGRN_HEADS, HGRN_DV)
    o_b = chunked_gated_linear_attention(q_b, k_b, v_b, jnp.log(f).reshape(B, T, HGRN_HEADS, HGRN_DF))
    o_b = gated_head_norm(o_b, hg, hgrn_norm_w)

    return jnp.concatenate([o_a, o_b], axis=-1) @ w_out


def setup_inputs(seed: int = 0) -> dict:
    key = jax.random.key(seed)
    ks = jax.random.split(key, 16)
    f32 = jnp.float32

    def nrm(k, shape, fan_in):
        return jax.random.normal(k, shape, f32) * (fan_in ** -0.5)

    def gain(k, shape):
        return 1.0 + 0.02 * jax.random.normal(k, shape, f32)

    return {
        "x": jax.random.normal(ks[0], (BATCH, SEQ, D_MODEL), f32),
        "w_in": nrm(ks[1], (DEPTH, D_MODEL, IN_COLS), D_MODEL),
        "w_gk_up": nrm(ks[2], (DEPTH, GLA_GATE_RANK, GLA_QK), GLA_GATE_RANK),
        "b_gk": 0.02 * jax.random.normal(ks[3], (DEPTH, GLA_QK), f32),
        "gla_norm_w": gain(ks[4], (DEPTH, GLA_V)),
        "hgrn_norm_w": gain(ks[5], (DEPTH, HGRN_V)),
        "hgrn_lower_bounds": 0.01 * jax.random.normal(ks[6], (DEPTH + 1, HGRN_F), f32),
        "w_out": nrm(ks[7], (DEPTH, MIX_WIDTH, D_MODEL), MIX_WIDTH),
        "pre_mix_norm": gain(ks[8], (DEPTH, D_MODEL)),
        "post_mix_norm": gain(ks[9], (DEPTH, D_MODEL)),
        "pre_mlp_norm": gain(ks[10], (DEPTH, D_MODEL)),
        "post_mlp_norm": gain(ks[11], (DEPTH, D_MODEL)),
        "w_up": nrm(ks[12], (DEPTH, D_MODEL, D_FF), D_MODEL),
        "w_down": nrm(ks[13], (DEPTH, D_FF, D_MODEL), D_FF),
    }


def reference(x, w_in, w_gk_up, b_gk, gla_norm_w, hgrn_norm_w, hgrn_lower_bounds, w_out,
              pre_mix_norm, post_mix_norm, pre_mlp_norm, post_mlp_norm, w_up, w_down):
    lbs = jnp.cumsum(jax.nn.softmax(hgrn_lower_bounds.astype(jnp.float32), axis=0), axis=0)
    h = x
    for l in range(DEPTH):
        mix = hybrid_mixer(rmsnorm(h, pre_mix_norm[l]), w_in[l], w_gk_up[l], b_gk[l],
                           gla_norm_w[l], hgrn_norm_w[l], lbs[l], w_out[l])
        h = h + rmsnorm(mix, post_mix_norm[l])
        u = rmsnorm(h, pre_mlp_norm[l]) @ w_up[l]
        m = jnp.square(jax.nn.relu(u)) @ w_down[l]
        h = h + rmsnorm(m, post_mlp_norm[l])
    return h
```

```python
import functools

import jax
import jax.numpy as jnp
from jax import lax
from jax.experimental import pallas as pl
from jax.experimental.pallas import tpu as pltpu

F32 = jnp.float32
BF16 = jnp.bfloat16

D_MODEL = 1024
GLA_HEADS, GLA_DK, GLA_DV = 4, 64, 128
GLA_QK = GLA_HEADS * GLA_DK
GLA_V = GLA_HEADS * GLA_DV
GLA_GATE_RANK = 16
GLA_GATE_NORM = 16.0
HGRN_HEADS, HGRN_DF, HGRN_DV = 4, 128, 128
HGRN_F = HGRN_HEADS * HGRN_DF
HGRN_V = HGRN_HEADS * HGRN_DV
D_FF = 4 * D_MODEL
EPS = 1e-6

LANES = 128
VMEM_LIMIT_BYTES = 56 * 1024 * 1024

_C_GQ = 0
_C_GK = _C_GQ + GLA_QK
_C_GV = _C_GK + GLA_QK
_C_GG = _C_GV + GLA_V
_C_HQ = _C_GG + GLA_V
_C_HF = _C_HQ + HGRN_F
_C_HI = _C_HF + HGRN_F
_C_HG = _C_HI + HGRN_V
_C_LR = _C_HG + HGRN_V
IN_COLS_PADDED = _C_LR + LANES

PROJ_ROWS = 512
ATT_ROWS = 256
CHUNK = 64
SUB = 16
NLAG = CHUNK // SUB
TAIL_ROWS = 256


def _rmsnorm(x, w):
    return x * lax.rsqrt(jnp.mean(x * x, axis=-1, keepdims=True) + EPS) * w


def _sigmoid(x):
    return 1.0 / (1.0 + jnp.exp(-x))


def _silu(x):
    return x * _sigmoid(x)


def _log_sigmoid(x):
    return -(jnp.maximum(-x, 0.0) + jnp.log(1.0 + jnp.exp(-jnp.abs(x))))


def _proj_kernel(x_ref, wn_ref, w_in_ref, wgk_ref, bgk_ref, lbl_ref,
                 qa_ref, ka_ref, ga_ref, va_ref, gga_ref,
                 qb_ref, kb_ref, gb_ref, vb_ref, ggb_ref):
    hb = _rmsnorm(x_ref[...], wn_ref[...]).astype(BF16)

    def proj(lo, width):
        return jnp.dot(hb, w_in_ref[:, lo:lo + width], preferred_element_type=F32)

    qa_ref[...] = proj(_C_GQ, GLA_QK) * (GLA_DK ** -0.5)
    ka_ref[...] = proj(_C_GK, GLA_QK)
    va_ref[...] = proj(_C_GV, GLA_V).astype(BF16)
    gga_ref[...] = proj(_C_GG, GLA_V)
    glr = proj(_C_LR, LANES).astype(BF16)
    gk = jnp.dot(glr, wgk_ref[...], preferred_element_type=F32) + bgk_ref[...]
    ga_ref[...] = _log_sigmoid(gk) * (1.0 / GLA_GATE_NORM)

    lbl = lbl_ref[...]
    e = jnp.exp(lbl - jnp.max(lbl, axis=0, keepdims=True))
    lb = e[0:1, :] / jnp.sum(e, axis=0, keepdims=True)
    f = lb + (1.0 - lb) * _sigmoid(proj(_C_HF, HGRN_F))
    qb_ref[...] = _silu(proj(_C_HQ, HGRN_F))
    kb_ref[...] = 1.0 - f
    gb_ref[...] = jnp.log(f)
    vb_ref[...] = proj(_C_HI, HGRN_V).astype(BF16)
    ggb_ref[...] = proj(_C_HG, HGRN_V)


def _proj_call(x2, wn, w_in_p, wgk_p, bgk, lbl):
    m = x2.shape[0]
    tm = PROJ_ROWS
    row = lambda width: pl.BlockSpec((tm, width), lambda i: (i, 0))
    full = lambda a: pl.BlockSpec(a.shape, lambda i: (0,) * a.ndim)
    out_shapes = (
        jax.ShapeDtypeStruct((m, GLA_QK), F32), jax.ShapeDtypeStruct((m, GLA_QK), F32),
        jax.ShapeDtypeStruct((m, GLA_QK), F32), jax.ShapeDtypeStruct((m, GLA_V), BF16),
        jax.ShapeDtypeStruct((m, GLA_V), F32),
        jax.ShapeDtypeStruct((m, HGRN_F), F32), jax.ShapeDtypeStruct((m, HGRN_F), F32),
        jax.ShapeDtypeStruct((m, HGRN_F), F32), jax.ShapeDtypeStruct((m, HGRN_V), BF16),
        jax.ShapeDtypeStruct((m, HGRN_V), F32),
    )
    return pl.pallas_call(
        _proj_kernel,
        grid=(m // tm,),
        in_specs=[row(D_MODEL), full(wn), full(w_in_p), full(wgk_p), full(bgk), full(lbl)],
        out_specs=tuple(row(s.shape[1]) for s in out_shapes),
        out_shape=out_shapes,
        compiler_params=pltpu.CompilerParams(
            dimension_semantics=("parallel",), vmem_limit_bytes=VMEM_LIMIT_BYTES),
        name="in_proj_gates",
    )(x2, wn, w_in_p, wgk_p, bgk, lbl)


def _att_kernel(q_ref, k_ref, g_ref, v_ref, o_ref, b_ref, st_ref, *, heads, dk, dv):
    rows = q_ref.shape[0]
    nchunk = rows // CHUNK
    gw = max(dk, LANES)
    hpg = gw // dk
    ngroups = heads // hpg

    @pl.when(pl.program_id(1) == 0)
    def _():
        st_ref[...] = jnp.zeros_like(st_ref)

    ri = lax.broadcasted_iota(jnp.int32, (rows, rows), 0)
    ci = lax.broadcasted_iota(jnp.int32, (rows, rows), 1)
    tri = jnp.where((ri // CHUNK == ci // CHUNK) & (ci <= ri), 1.0, 0.0).astype(BF16)
    g = g_ref[...]
    g_hi = g.astype(BF16)
    r1 = g - g_hi.astype(F32)
    g_mid = r1.astype(BF16)
    g_lo = (r1 - g_mid.astype(F32)).astype(BF16)
    b_ref[...] = (jnp.dot(tri, g_hi, preferred_element_type=F32)
                  + jnp.dot(tri, g_mid, preferred_element_type=F32)
                  + jnp.dot(tri, g_lo, preferred_element_type=F32))

    ii = lax.broadcasted_iota(jnp.int32, (CHUNK, CHUNK), 0)
    jj = lax.broadcasted_iota(jnp.int32, (CHUNK, CHUNK), 1)
    lag_of = ii // SUB - jj // SUB
    masks = [(lag_of == l) & (jj <= ii) for l in range(NLAG)]
    lane = lax.broadcasted_iota(jnp.int32, (1, gw), 1)

    for gi in range(ngroups):
        lanes = slice(gi * gw, (gi + 1) * gw)
        for c in range(nchunk):
            r0 = c * CHUNK
            rws = slice(r0, r0 + CHUNK)
            b = b_ref[rws, lanes]
            q = q_ref[rws, lanes]
            k = k_ref[rws, lanes]
            ends = [b_ref[r0 + SUB * r + SUB - 1:r0 + SUB * r + SUB, lanes] for r in range(NLAG)]
            zero = jnp.zeros_like(ends[0])

            def ref_rows(lag):
                return jnp.concatenate(
                    [jnp.broadcast_to(ends[r - lag] if r - lag >= 0 else zero, (SUB, gw))
                     for r in range(NLAG)], axis=0)

            q_lag = jnp.concatenate(
                [(q * jnp.exp(b - ref_rows(l))).astype(BF16) for l in range(NLAG)], axis=0)
            q_in = q * jnp.exp(b)
            k_hat = k * jnp.exp(ref_rows(0) - b)
            k_dec = (k * jnp.exp(ends[NLAG - 1] - b)).astype(BF16)
            dec = jnp.exp(ends[NLAG - 1])

            for hh in range(hpg):
                h = gi * hpg + hh
                if hpg == 1:
                    k_h = k_hat.astype(BF16)
                    q_in_h = q_in.astype(BF16)
                else:
                    in_head = (lane >= hh * dk) & (lane < (hh + 1) * dk)
                    k_h = jnp.where(in_head, k_hat, 0.0).astype(BF16)
                    q_in_h = jnp.where(in_head, q_in, 0.0).astype(BF16)
                v_h = v_ref[rws, h * dv:(h + 1) * dv]
                s_all = lax.dot_general(q_lag, k_h, (((1,), (1,)), ((), ())),
                                        preferred_element_type=F32)
                p = jnp.zeros((CHUNK, CHUNK), F32)
                for l in range(NLAG):
                    p = jnp.where(masks[l], s_all[l * CHUNK:(l + 1) * CHUNK, :], p)
                st = st_ref[h]
                o = (jnp.dot(p.astype(BF16), v_h, preferred_element_type=F32)
                     + lax.dot_general(q_in_h, st.astype(BF16), (((1,), (1,)), ((), ())),
                                       preferred_element_type=F32))
                o_ref[rws, h * dv:(h + 1) * dv] = o
                upd = lax.dot_general(v_h, k_dec, (((0,), (0,)), ((), ())),
                                      preferred_element_type=F32)
                st_ref[h] = st * dec + upd


def _att_call(q, k, g, v, *, batch, heads, dk, dv, name):
    m = q.shape[0]
    t = m // batch
    tb = ATT_ROWS
    nt = t // tb
    w = heads * dk
    gw = max(dk, LANES)
    qspec = pl.BlockSpec((tb, w), lambda b, i: (b * nt + i, 0))
    vspec = pl.BlockSpec((tb, heads * dv), lambda b, i: (b * nt + i, 0))
    return pl.pallas_call(
        functools.partial(_att_kernel, heads=heads, dk=dk, dv=dv),
        grid=(batch, nt),
        in_specs=[qspec, qspec, qspec, vspec],
        out_specs=vspec,
        out_shape=jax.ShapeDtypeStruct((m, heads * dv), F32),
        scratch_shapes=[pltpu.VMEM((tb, w), F32), pltpu.VMEM((heads, dv, gw), F32)],
        compiler_params=pltpu.CompilerParams(
            dimension_semantics=("arbitrary", "arbitrary"), vmem_limit_bytes=VMEM_LIMIT_BYTES),
        name=name,
    )(q, k, g, v)


def _head_norm_gate(o, gate, w, heads, dv):
    parts = []
    for h in range(heads):
        oh = o[:, h * dv:(h + 1) * dv]
        parts.append(oh * lax.rsqrt(jnp.mean(oh * oh, axis=-1, keepdims=True) + EPS))
    y = jnp.concatenate(parts, axis=-1) * w
    return (y * _silu(gate)).astype(BF16)


def _tail_kernel(x_ref, oa_ref, ob_ref, gga_ref, ggb_ref, nwa_ref, nwb_ref,
                 w_out_ref, post_mix_ref, pre_mlp_ref, post_mlp_ref, w_up_ref, w_down_ref,
                 out_ref):
    ya = _head_norm_gate(oa_ref[...], gga_ref[...], nwa_ref[...], GLA_HEADS, GLA_DV)
    yb = _head_norm_gate(ob_ref[...], ggb_ref[...], nwb_ref[...], HGRN_HEADS, HGRN_DV)
    mix = (jnp.dot(ya, w_out_ref[0:GLA_V, :], preferred_element_type=F32)
           + jnp.dot(yb, w_out_ref[GLA_V:GLA_V + HGRN_V, :], preferred_element_type=F32))
    h = x_ref[...] + _rmsnorm(mix, post_mix_ref[...])
    u = jnp.dot(_rmsnorm(h, pre_mlp_ref[...]).astype(BF16), w_up_ref[...],
                preferred_element_type=F32)
    r = jnp.maximum(u, 0.0)
    m = jnp.dot((r * r).astype(BF16), w_down_ref[...], preferred_element_type=F32)
    out_ref[...] = h + _rmsnorm(m, post_mlp_ref[...])


def _tail_call(x2, oa, ob, gga, ggb, nwa, nwb, w_out, post_mix, pre_mlp, post_mlp, w_up, w_down):
    m = x2.shape[0]
    tm = TAIL_ROWS
    row = lambda width: pl.BlockSpec((tm, width), lambda i: (i, 0))
    full = lambda a: pl.BlockSpec(a.shape, lambda i: (0,) * a.ndim)
    resident = lambda a: pl.BlockSpec(a.shape, lambda i: (0,) * a.ndim,
                                      pipeline_mode=pl.Buffered(1))
    return pl.pallas_call(
        _tail_kernel,
        grid=(m // tm,),
        in_specs=[row(D_MODEL), row(GLA_V), row(HGRN_V), row(GLA_V), row(HGRN_V),
                  full(nwa), full(nwb), resident(w_out), full(post_mix), full(pre_mlp),
                  full(post_mlp), resident(w_up), resident(w_down)],
        out_specs=row(D_MODEL),
        out_shape=jax.ShapeDtypeStruct((m, D_MODEL), F32),
        compiler_params=pltpu.CompilerParams(
            dimension_semantics=("parallel",), vmem_limit_bytes=VMEM_LIMIT_BYTES),
        name="out_proj_mlp",
    )(x2, oa, ob, gga, ggb, nwa, nwb, w_out, post_mix, pre_mlp, post_mlp, w_up, w_down)


def kernel(x, w_in, w_gk_up, b_gk, gla_norm_w, hgrn_norm_w, hgrn_lower_bounds, w_out,
           pre_mix_norm, post_mix_norm, pre_mlp_norm, post_mlp_norm, w_up, w_down):
    batch, seq, d = x.shape
    x2 = x.reshape(batch * seq, d)
    l = 0
    wi = w_in[l]
    lr0 = 2 * GLA_QK + 2 * GLA_V
    w_in_p = jnp.concatenate(
        [wi[:, :lr0], wi[:, lr0 + GLA_GATE_RANK:], wi[:, lr0:lr0 + GLA_GATE_RANK],
         jnp.zeros((d, LANES - GLA_GATE_RANK), wi.dtype)], axis=1).astype(BF16)
    wgk_p = jnp.concatenate(
        [w_gk_up[l], jnp.zeros((LANES - GLA_GATE_RANK, GLA_QK), w_gk_up.dtype)], axis=0).astype(BF16)
    row2 = lambda a: a.reshape(1, -1)

    qa, ka, ga, va, gga, qb, kb, gb, vb, ggb = _proj_call(
        x2, row2(pre_mix_norm[l]), w_in_p, wgk_p, row2(b_gk[l]), hgrn_lower_bounds)
    oa = _att_call(qa, ka, ga, va, batch=batch, heads=GLA_HEADS, dk=GLA_DK, dv=GLA_DV,
                   name="gla_attention")
    ob = _att_call(qb, kb, gb, vb, batch=batch, heads=HGRN_HEADS, dk=HGRN_DF, dv=HGRN_DV,
                   name="hgrn_attention")
    out = _tail_call(x2, oa, ob, gga, ggb, row2(gla_norm_w[l]), row2(hgrn_norm_w[l]),
                     w_out[l].astype(BF16), row2(post_mix_norm[l]), row2(pre_mlp_norm[l]),
                     row2(post_mlp_norm[l]), w_up[l].astype(BF16), w_down[l].astype(BF16))
    return out.reshape(batch, seq, d)
```

```python
import functools

import jax
import jax.numpy as jnp
from jax import lax
from jax.experimental import pallas as pl
from jax.experimental.pallas import tpu as pltpu

F32 = jnp.float32
BF16 = jnp.bfloat16

D_MODEL = 1024
GLA_HEADS, GLA_DK, GLA_DV = 4, 64, 128
GLA_QK = GLA_HEADS * GLA_DK
GLA_V = GLA_HEADS * GLA_DV
GLA_GATE_RANK = 16
GLA_GATE_NORM = 16.0
HGRN_HEADS, HGRN_DF, HGRN_DV = 4, 128, 128
HGRN_F = HGRN_HEADS * HGRN_DF
HGRN_V = HGRN_HEADS * HGRN_DV
D_FF = 4 * D_MODEL
EPS = 1e-6

LANES = 128
VMEM_LIMIT_BYTES = 56 * 1024 * 1024

_C_GQ = 0
_C_GK = _C_GQ + GLA_QK
_C_GV = _C_GK + GLA_QK
_C_GG = _C_GV + GLA_V
GLA_COLS = _C_GG + GLA_V
_C_HQ = 0
_C_HF = _C_HQ + HGRN_F
_C_HI = _C_HF + HGRN_F
_C_HG = _C_HI + HGRN_V
HGRN_COLS = _C_HG + HGRN_V

PROJ_ROWS = 512
ATT_ROWS = 256
CHUNK = 64
SUB = 16
NLAG = CHUNK // SUB
TAIL_ROWS = 256


def _rmsnorm(x, w):
    return x * lax.rsqrt(jnp.mean(x * x, axis=-1, keepdims=True) + EPS) * w


def _sigmoid(x):
    return 1.0 / (1.0 + jnp.exp(-x))


def _silu(x):
    return x * _sigmoid(x)


def _log_sigmoid(x):
    return -(jnp.maximum(-x, 0.0) + jnp.log(1.0 + jnp.exp(-jnp.abs(x))))


def _proj_kernel(x_ref, wn_ref, wa_ref, wlr_ref, wb_ref, wgk_ref, bgk_ref, lbl_ref,
                 qa_ref, ka_ref, ga_ref, va_ref, gga_ref,
                 qb_ref, kb_ref, gb_ref, vb_ref, ggb_ref):
    hb = _rmsnorm(x_ref[...], wn_ref[...]).astype(BF16)

    def proj(w_ref, lo, width):
        return jnp.dot(hb, w_ref[:, lo:lo + width], preferred_element_type=F32)

    qa_ref[...] = proj(wa_ref, _C_GQ, GLA_QK) * (GLA_DK ** -0.5)
    ka_ref[...] = proj(wa_ref, _C_GK, GLA_QK)
    va_ref[...] = proj(wa_ref, _C_GV, GLA_V).astype(BF16)
    gga_ref[...] = proj(wa_ref, _C_GG, GLA_V)
    glr = proj(wlr_ref, 0, LANES).astype(BF16)
    gk = jnp.dot(glr, wgk_ref[...], preferred_element_type=F32) + bgk_ref[...]
    ga_ref[...] = _log_sigmoid(gk) * (1.0 / GLA_GATE_NORM)

    lbl = lbl_ref[...]
    e = jnp.exp(lbl - jnp.max(lbl, axis=0, keepdims=True))
    lb = e[0:1, :] / jnp.sum(e, axis=0, keepdims=True)
    f = lb + (1.0 - lb) * _sigmoid(proj(wb_ref, _C_HF, HGRN_F))
    qb_ref[...] = _silu(proj(wb_ref, _C_HQ, HGRN_F))
    kb_ref[...] = 1.0 - f
    gb_ref[...] = jnp.log(f)
    vb_ref[...] = proj(wb_ref, _C_HI, HGRN_V).astype(BF16)
    ggb_ref[...] = proj(wb_ref, _C_HG, HGRN_V)


def _proj_call(x2, wn, w_a, w_lr, w_b, wgk_p, bgk, lbl):
    m = x2.shape[0]
    tm = PROJ_ROWS
    row = lambda width: pl.BlockSpec((tm, width), lambda i: (i, 0))
    full = lambda a: pl.BlockSpec(a.shape, lambda i: (0,) * a.ndim)
    out_shapes = (
        jax.ShapeDtypeStruct((m, GLA_QK), F32), jax.ShapeDtypeStruct((m, GLA_QK), F32),
        jax.ShapeDtypeStruct((m, GLA_QK), F32), jax.ShapeDtypeStruct((m, GLA_V), BF16),
        jax.ShapeDtypeStruct((m, GLA_V), F32),
        jax.ShapeDtypeStruct((m, HGRN_F), F32), jax.ShapeDtypeStruct((m, HGRN_F), F32),
        jax.ShapeDtypeStruct((m, HGRN_F), F32), jax.ShapeDtypeStruct((m, HGRN_V), BF16),
        jax.ShapeDtypeStruct((m, HGRN_V), F32),
    )
    return pl.pallas_call(
        _proj_kernel,
        grid=(m // tm,),
        in_specs=[row(D_MODEL), full(wn), full(w_a), full(w_lr), full(w_b), full(wgk_p),
                  full(bgk), full(lbl)],
        out_specs=tuple(row(s.shape[1]) for s in out_shapes),
        out_shape=out_shapes,
        compiler_params=pltpu.CompilerParams(
            dimension_semantics=("parallel",), vmem_limit_bytes=VMEM_LIMIT_BYTES),
        name="in_proj_gates",
    )(x2, wn, w_a, w_lr, w_b, wgk_p, bgk, lbl)


def _att_kernel(q_ref, k_ref, g_ref, v_ref, o_ref, b_ref, st_ref, *, heads, dk, dv):
    rows = q_ref.shape[0]
    nchunk = rows // CHUNK
    gw = max(dk, LANES)
    hpg = gw // dk
    ngroups = heads // hpg

    @pl.when(pl.program_id(1) == 0)
    def _():
        st_ref[...] = jnp.zeros_like(st_ref)

    ri = lax.broadcasted_iota(jnp.int32, (rows, rows), 0)
    ci = lax.broadcasted_iota(jnp.int32, (rows, rows), 1)
    tri = jnp.where((ri // CHUNK == ci // CHUNK) & (ci <= ri), 1.0, 0.0).astype(BF16)
    g = g_ref[...]
    g_hi = g.astype(BF16)
    r1 = g - g_hi.astype(F32)
    g_mid = r1.astype(BF16)
    g_lo = (r1 - g_mid.astype(F32)).astype(BF16)
    b_ref[...] = (jnp.dot(tri, g_hi, preferred_element_type=F32)
                  + jnp.dot(tri, g_mid, preferred_element_type=F32)
                  + jnp.dot(tri, g_lo, preferred_element_type=F32))

    ii = lax.broadcasted_iota(jnp.int32, (CHUNK, CHUNK), 0)
    jj = lax.broadcasted_iota(jnp.int32, (CHUNK, CHUNK), 1)
    lag_of = ii // SUB - jj // SUB
    masks = [(lag_of == l) & (jj <= ii) for l in range(NLAG)]
    lane = lax.broadcasted_iota(jnp.int32, (1, gw), 1)

    prep = {}
    for gi in range(ngroups):
        lanes = slice(gi * gw, (gi + 1) * gw)
        for c in range(nchunk):
            r0 = c * CHUNK
            rws = slice(r0, r0 + CHUNK)
            b = b_ref[rws, lanes]
            q = q_ref[rws, lanes]
            k = k_ref[rws, lanes]
            ends = [b_ref[r0 + SUB * r + SUB - 1:r0 + SUB * r + SUB, lanes] for r in range(NLAG)]
            zero = jnp.zeros_like(ends[0])

            def ref_rows(lag):
                return jnp.concatenate(
                    [jnp.broadcast_to(ends[r - lag] if r - lag >= 0 else zero, (SUB, gw))
                     for r in range(NLAG)], axis=0)

            q_lag = jnp.concatenate(
                [(q * jnp.exp(b - ref_rows(l))).astype(BF16) for l in range(NLAG)], axis=0)
            q_in = q * jnp.exp(b)
            k_hat = k * jnp.exp(ref_rows(0) - b)
            k_dec = (k * jnp.exp(ends[NLAG - 1] - b)).astype(BF16)
            dec = jnp.exp(ends[NLAG - 1])
            prep[gi, c] = (q_lag, q_in, k_hat, k_dec, dec)

    def head_operands(gi, hh, c):
        _, q_in, k_hat, _, _ = prep[gi, c]
        if hpg == 1:
            return k_hat.astype(BF16), q_in.astype(BF16)
        in_head = (lane >= hh * dk) & (lane < (hh + 1) * dk)
        return (jnp.where(in_head, k_hat, 0.0).astype(BF16),
                jnp.where(in_head, q_in, 0.0).astype(BF16))

    units = [(gi, hh, c) for gi in range(ngroups) for hh in range(hpg) for c in range(nchunk)]
    v_of = lambda h, c: v_ref[c * CHUNK:(c + 1) * CHUNK, h * dv:(h + 1) * dv]

    probs, qin, upd = {}, {}, {}
    for gi, hh, c in units:
        h = gi * hpg + hh
        k_h, qin[gi, hh, c] = head_operands(gi, hh, c)
        s_all = lax.dot_general(prep[gi, c][0], k_h, (((1,), (1,)), ((), ())),
                                preferred_element_type=F32)
        p = jnp.zeros((CHUNK, CHUNK), F32)
        for l in range(NLAG):
            p = jnp.where(masks[l], s_all[l * CHUNK:(l + 1) * CHUNK, :], p)
        probs[gi, hh, c] = p.astype(BF16)
        upd[gi, hh, c] = lax.dot_general(v_of(h, c), prep[gi, c][3], (((0,), (0,)), ((), ())),
                                         preferred_element_type=F32)

    states = {}
    for gi in range(ngroups):
        for hh in range(hpg):
            h = gi * hpg + hh
            st = st_ref[h]
            for c in range(nchunk):
                states[gi, hh, c] = st.astype(BF16)
                st = st * prep[gi, c][4] + upd[gi, hh, c]
            st_ref[h] = st

    for gi, hh, c in units:
        h = gi * hpg + hh
        o = (jnp.dot(probs[gi, hh, c], v_of(h, c), preferred_element_type=F32)
             + lax.dot_general(qin[gi, hh, c], states[gi, hh, c], (((1,), (1,)), ((), ())),
                               preferred_element_type=F32))
        o_ref[c * CHUNK:(c + 1) * CHUNK, h * dv:(h + 1) * dv] = o


def _att_call(q, k, g, v, *, batch, heads, dk, dv, name):
    m = q.shape[0]
    t = m // batch
    tb = ATT_ROWS
    nt = t // tb
    w = heads * dk
    gw = max(dk, LANES)
    qspec = pl.BlockSpec((tb, w), lambda b, i: (b * nt + i, 0))
    vspec = pl.BlockSpec((tb, heads * dv), lambda b, i: (b * nt + i, 0))
    return pl.pallas_call(
        functools.partial(_att_kernel, heads=heads, dk=dk, dv=dv),
        grid=(batch, nt),
        in_specs=[qspec, qspec, qspec, vspec],
        out_specs=vspec,
        out_shape=jax.ShapeDtypeStruct((m, heads * dv), F32),
        scratch_shapes=[pltpu.VMEM((tb, w), F32), pltpu.VMEM((heads, dv, gw), F32)],
        compiler_params=pltpu.CompilerParams(
            dimension_semantics=("arbitrary", "arbitrary"), vmem_limit_bytes=VMEM_LIMIT_BYTES),
        name=name,
    )(q, k, g, v)


def _head_norm_gate(o, gate, w, heads, dv):
    parts = []
    for h in range(heads):
        oh = o[:, h * dv:(h + 1) * dv]
        parts.append(oh * lax.rsqrt(jnp.mean(oh * oh, axis=-1, keepdims=True) + EPS))
    y = jnp.concatenate(parts, axis=-1) * w
    return (y * _silu(gate)).astype(BF16)


def _tail_kernel(x_ref, oa_ref, ob_ref, gga_ref, ggb_ref, nwa_ref, nwb_ref,
                 w_out_ref, post_mix_ref, pre_mlp_ref, post_mlp_ref, w_up_ref, w_down_ref,
                 out_ref):
    ya = _head_norm_gate(oa_ref[...], gga_ref[...], nwa_ref[...], GLA_HEADS, GLA_DV)
    yb = _head_norm_gate(ob_ref[...], ggb_ref[...], nwb_ref[...], HGRN_HEADS, HGRN_DV)
    mix = (jnp.dot(ya, w_out_ref[0:GLA_V, :], preferred_element_type=F32)
           + jnp.dot(yb, w_out_ref[GLA_V:GLA_V + HGRN_V, :], preferred_element_type=F32))
    h = x_ref[...] + _rmsnorm(mix, post_mix_ref[...])
    u = jnp.dot(_rmsnorm(h, pre_mlp_ref[...]).astype(BF16), w_up_ref[...],
                preferred_element_type=F32)
    r = jnp.maximum(u, 0.0)
    m = jnp.dot((r * r).astype(BF16), w_down_ref[...], preferred_element_type=F32)
    out_ref[...] = h + _rmsnorm(m, post_mlp_ref[...])


def _tail_call(x2, oa, ob, gga, ggb, nwa, nwb, w_out, post_mix, pre_mlp, post_mlp, w_up, w_down):
    m = x2.shape[0]
    tm = TAIL_ROWS
    row = lambda width: pl.BlockSpec((tm, width), lambda i: (i, 0))
    full = lambda a: pl.BlockSpec(a.shape, lambda i: (0,) * a.ndim)
    resident = lambda a: pl.BlockSpec(a.shape, lambda i: (0,) * a.ndim,
                                      pipeline_mode=pl.Buffered(1))
    return pl.pallas_call(
        _tail_kernel,
        grid=(m // tm,),
        in_specs=[row(D_MODEL), row(GLA_V), row(HGRN_V), row(GLA_V), row(HGRN_V),
                  full(nwa), full(nwb), resident(w_out), full(post_mix), full(pre_mlp),
                  full(post_mlp), resident(w_up), resident(w_down)],
        out_specs=row(D_MODEL),
        out_shape=jax.ShapeDtypeStruct((m, D_MODEL), F32),
        compiler_params=pltpu.CompilerParams(
            dimension_semantics=("parallel",), vmem_limit_bytes=VMEM_LIMIT_BYTES),
        name="out_proj_mlp",
    )(x2, oa, ob, gga, ggb, nwa, nwb, w_out, post_mix, pre_mlp, post_mlp, w_up, w_down)


def kernel(x, w_in, w_gk_up, b_gk, gla_norm_w, hgrn_norm_w, hgrn_lower_bounds, w_out,
           pre_mix_norm, post_mix_norm, pre_mlp_norm, post_mlp_norm, w_up, w_down):
    batch, seq, d = x.shape
    x2 = x.reshape(batch * seq, d)
    l = 0
    wi = w_in[l]
    w_a = wi[:, :GLA_COLS].astype(BF16)
    w_lr = jnp.pad(wi[:, GLA_COLS:GLA_COLS + GLA_GATE_RANK],
                   ((0, 0), (0, LANES - GLA_GATE_RANK))).astype(BF16)
    w_b = wi[:, GLA_COLS + GLA_GATE_RANK:].astype(BF16)
    wgk_p = jnp.pad(w_gk_up[l], ((0, LANES - GLA_GATE_RANK), (0, 0))).astype(BF16)
    row2 = lambda a: a.reshape(1, -1)

    qa, ka, ga, va, gga, qb, kb, gb, vb, ggb = _proj_call(
        x2, row2(pre_mix_norm[l]), w_a, w_lr, w_b, wgk_p, row2(b_gk[l]), hgrn_lower_bounds)
    oa = _att_call(qa, ka, ga, va, batch=batch, heads=GLA_HEADS, dk=GLA_DK, dv=GLA_DV,
                   name="gla_attention")
    ob = _att_call(qb, kb, gb, vb, batch=batch, heads=HGRN_HEADS, dk=HGRN_DF, dv=HGRN_DV,
                   name="hgrn_attention")
    out = _tail_call(x2, oa, ob, gga, ggb, row2(gla_norm_w[l]), row2(hgrn_norm_w[l]),
                     w_out[l].astype(BF16), row2(post_mix_norm[l]), row2(pre_mlp_norm[l]),
                     row2(post_mlp_norm[l]), w_up[l].astype(BF16), w_down[l].astype(BF16))
    return out.reshape(batch, seq, d)
```

```python
import functools

import jax
import jax.numpy as jnp
from jax import lax
from jax.experimental import pallas as pl
from jax.experimental.pallas import tpu as pltpu

F32 = jnp.float32
BF16 = jnp.bfloat16

D_MODEL = 1024
GLA_HEADS, GLA_DK, GLA_DV = 4, 64, 128
GLA_QK = GLA_HEADS * GLA_DK
GLA_V = GLA_HEADS * GLA_DV
GLA_GATE_RANK = 16
GLA_GATE_NORM = 16.0
HGRN_HEADS, HGRN_DF, HGRN_DV = 4, 128, 128
HGRN_F = HGRN_HEADS * HGRN_DF
HGRN_V = HGRN_HEADS * HGRN_DV
D_FF = 4 * D_MODEL
EPS = 1e-6

LANES = 128
VMEM_LIMIT_BYTES = 56 * 1024 * 1024

_C_GQ = 0
_C_GK = _C_GQ + GLA_QK
_C_GV = _C_GK + GLA_QK
_C_GG = _C_GV + GLA_V
GLA_COLS = _C_GG + GLA_V
_C_HQ = 0
_C_HF = _C_HQ + HGRN_F
_C_HI = _C_HF + HGRN_F
_C_HG = _C_HI + HGRN_V
HGRN_COLS = _C_HG + HGRN_V

PROJ_ROWS = 512
ATT_ROWS = 512
CUMSUM_ROWS = 256
CHUNK = 64
SUB = 16
NLAG = CHUNK // SUB
LAG_OFFSETS = [sum(CHUNK - SUB * m for m in range(l)) for l in range(NLAG + 1)]
LAG_ROWS = LAG_OFFSETS[NLAG]
TAIL_ROWS = 512
TAIL_SUBTILES = 2


def _rmsnorm(x, w):
    return x * lax.rsqrt(jnp.mean(x * x, axis=-1, keepdims=True) + EPS) * w


def _sigmoid(x):
    return 1.0 / (1.0 + jnp.exp(-x))


def _silu(x):
    return x * _sigmoid(x)


def _log_sigmoid(x):
    return -(jnp.maximum(-x, 0.0) + jnp.log(1.0 + jnp.exp(-jnp.abs(x))))


def _proj_kernel(x_ref, wn_ref, wa_ref, wlr_ref, wb_ref, wgk_ref, bgk_ref, lbl_ref,
                 qa_ref, ka_ref, ga_ref, va_ref, gga_ref,
                 qb_ref, kb_ref, gb_ref, vb_ref, ggb_ref):
    hb = _rmsnorm(x_ref[...], wn_ref[...]).astype(BF16)

    def proj(w_ref, lo, width):
        return jnp.dot(hb, w_ref[:, lo:lo + width], preferred_element_type=F32)

    qa_ref[...] = proj(wa_ref, _C_GQ, GLA_QK) * (GLA_DK ** -0.5)
    ka_ref[...] = proj(wa_ref, _C_GK, GLA_QK)
    va_ref[...] = proj(wa_ref, _C_GV, GLA_V).astype(BF16)
    gga_ref[...] = proj(wa_ref, _C_GG, GLA_V)
    glr = proj(wlr_ref, 0, LANES).astype(BF16)
    gk = jnp.dot(glr, wgk_ref[...], preferred_element_type=F32) + bgk_ref[...]
    ga_ref[...] = _log_sigmoid(gk) * (1.0 / GLA_GATE_NORM)

    lbl = lbl_ref[...]
    e = jnp.exp(lbl - jnp.max(lbl, axis=0, keepdims=True))
    lb = e[0:1, :] / jnp.sum(e, axis=0, keepdims=True)
    f = lb + (1.0 - lb) * _sigmoid(proj(wb_ref, _C_HF, HGRN_F))
    qb_ref[...] = _silu(proj(wb_ref, _C_HQ, HGRN_F))
    kb_ref[...] = 1.0 - f
    gb_ref[...] = jnp.log(f)
    vb_ref[...] = proj(wb_ref, _C_HI, HGRN_V).astype(BF16)
    ggb_ref[...] = proj(wb_ref, _C_HG, HGRN_V)


def _proj_call(x2, wn, w_a, w_lr, w_b, wgk_p, bgk, lbl):
    m = x2.shape[0]
    tm = PROJ_ROWS
    row = lambda width: pl.BlockSpec((tm, width), lambda i: (i, 0))
    full = lambda a: pl.BlockSpec(a.shape, lambda i: (0,) * a.ndim)
    out_shapes = (
        jax.ShapeDtypeStruct((m, GLA_QK), F32), jax.ShapeDtypeStruct((m, GLA_QK), F32),
        jax.ShapeDtypeStruct((m, GLA_QK), F32), jax.ShapeDtypeStruct((m, GLA_V), BF16),
        jax.ShapeDtypeStruct((m, GLA_V), F32),
        jax.ShapeDtypeStruct((m, HGRN_F), F32), jax.ShapeDtypeStruct((m, HGRN_F), F32),
        jax.ShapeDtypeStruct((m, HGRN_F), F32), jax.ShapeDtypeStruct((m, HGRN_V), BF16),
        jax.ShapeDtypeStruct((m, HGRN_V), F32),
    )
    return pl.pallas_call(
        _proj_kernel,
        grid=(m // tm,),
        in_specs=[row(D_MODEL), full(wn), full(w_a), full(w_lr), full(w_b), full(wgk_p),
                  full(bgk), full(lbl)],
        out_specs=tuple(row(s.shape[1]) for s in out_shapes),
        out_shape=out_shapes,
        compiler_params=pltpu.CompilerParams(
            dimension_semantics=("parallel",), vmem_limit_bytes=VMEM_LIMIT_BYTES),
        name="in_proj_gates",
    )(x2, wn, w_a, w_lr, w_b, wgk_p, bgk, lbl)


def _att_kernel(q_ref, k_ref, g_ref, v_ref, o_ref, b_ref, st_ref, *, heads, dk, dv):
    rows = q_ref.shape[0]
    nchunk = rows // CHUNK
    gw = max(dk, LANES)
    hpg = gw // dk
    ngroups = heads // hpg

    @pl.when(pl.program_id(1) == 0)
    def _():
        st_ref[...] = jnp.zeros_like(st_ref)

    ri = lax.broadcasted_iota(jnp.int32, (CUMSUM_ROWS, CUMSUM_ROWS), 0)
    ci = lax.broadcasted_iota(jnp.int32, (CUMSUM_ROWS, CUMSUM_ROWS), 1)
    tri = jnp.where((ri // CHUNK == ci // CHUNK) & (ci <= ri), 1.0, 0.0).astype(BF16)
    for r0 in range(0, rows, CUMSUM_ROWS):
        blk = slice(r0, r0 + CUMSUM_ROWS)
        g = g_ref[blk, :]
        g_hi = g.astype(BF16)
        r1 = g - g_hi.astype(F32)
        g_mid = r1.astype(BF16)
        g_lo = (r1 - g_mid.astype(F32)).astype(BF16)
        b_ref[blk, :] = (jnp.dot(tri, g_hi, preferred_element_type=F32)
                         + jnp.dot(tri, g_mid, preferred_element_type=F32)
                         + jnp.dot(tri, g_lo, preferred_element_type=F32))

    ii = lax.broadcasted_iota(jnp.int32, (CHUNK, CHUNK), 0)
    jj = lax.broadcasted_iota(jnp.int32, (CHUNK, CHUNK), 1)
    lag_of = ii // SUB - jj // SUB
    masks = [(lag_of == l) & (jj <= ii) for l in range(NLAG)]
    lane = lax.broadcasted_iota(jnp.int32, (1, gw), 1)

    prep = {}
    for gi in range(ngroups):
        lanes = slice(gi * gw, (gi + 1) * gw)
        for c in range(nchunk):
            r0 = c * CHUNK
            rws = slice(r0, r0 + CHUNK)
            b = b_ref[rws, lanes]
            q = q_ref[rws, lanes]
            k = k_ref[rws, lanes]
            ends = [b_ref[r0 + SUB * r + SUB - 1:r0 + SUB * r + SUB, lanes] for r in range(NLAG)]

            def ref_rows(lag):
                return jnp.concatenate(
                    [jnp.broadcast_to(ends[r - lag], (SUB, gw)) for r in range(lag, NLAG)], axis=0)

            q_lag = jnp.concatenate(
                [(q[SUB * l:, :] * jnp.exp(b[SUB * l:, :] - ref_rows(l))).astype(BF16)
                 for l in range(NLAG)], axis=0)
            q_in = q * jnp.exp(b)
            k_hat = k * jnp.exp(ref_rows(0) - b)
            k_dec = (k * jnp.exp(ends[NLAG - 1] - b)).astype(BF16)
            dec = jnp.exp(ends[NLAG - 1])
            prep[gi, c] = (q_lag, q_in, k_hat, k_dec, dec)

    def head_operands(gi, hh, c):
        _, q_in, k_hat, _, _ = prep[gi, c]
        if hpg == 1:
            return k_hat.astype(BF16), q_in.astype(BF16)
        in_head = (lane >= hh * dk) & (lane < (hh + 1) * dk)
        return (jnp.where(in_head, k_hat, 0.0).astype(BF16),
                jnp.where(in_head, q_in, 0.0).astype(BF16))

    units = [(gi, hh, c) for gi in range(ngroups) for hh in range(hpg) for c in range(nchunk)]
    v_of = lambda h, c: v_ref[c * CHUNK:(c + 1) * CHUNK, h * dv:(h + 1) * dv]

    probs, qin, upd = {}, {}, {}
    for gi, hh, c in units:
        h = gi * hpg + hh
        k_h, qin[gi, hh, c] = head_operands(gi, hh, c)
        s_all = lax.dot_general(prep[gi, c][0], k_h, (((1,), (1,)), ((), ())),
                                preferred_element_type=F32)
        p = jnp.where(masks[0], s_all[0:CHUNK, :], 0.0)
        for l in range(1, NLAG):
            s_l = jnp.concatenate(
                [jnp.zeros((SUB * l, CHUNK), F32),
                 s_all[LAG_OFFSETS[l]:LAG_OFFSETS[l] + CHUNK - SUB * l, :]], axis=0)
            p = jnp.where(masks[l], s_l, p)
        probs[gi, hh, c] = p.astype(BF16)
        upd[gi, hh, c] = lax.dot_general(v_of(h, c), prep[gi, c][3], (((0,), (0,)), ((), ())),
                                         preferred_element_type=F32)

    states = {}
    for gi in range(ngroups):
        for hh in range(hpg):
            h = gi * hpg + hh
            st = st_ref[h]
            for c in range(nchunk):
                states[gi, hh, c] = st.astype(BF16)
                st = st * prep[gi, c][4] + upd[gi, hh, c]
            st_ref[h] = st

    for gi, hh, c in units:
        h = gi * hpg + hh
        o = (jnp.dot(probs[gi, hh, c], v_of(h, c), preferred_element_type=F32)
             + lax.dot_general(qin[gi, hh, c], states[gi, hh, c], (((1,), (1,)), ((), ())),
                               preferred_element_type=F32))
        o_ref[c * CHUNK:(c + 1) * CHUNK, h * dv:(h + 1) * dv] = o


def _att_call(q, k, g, v, *, batch, heads, dk, dv, name):
    m = q.shape[0]
    t = m // batch
    tb = ATT_ROWS
    nt = t // tb
    w = heads * dk
    gw = max(dk, LANES)
    qspec = pl.BlockSpec((tb, w), lambda b, i: (b * nt + i, 0))
    vspec = pl.BlockSpec((tb, heads * dv), lambda b, i: (b * nt + i, 0))
    return pl.pallas_call(
        functools.partial(_att_kernel, heads=heads, dk=dk, dv=dv),
        grid=(batch, nt),
        in_specs=[qspec, qspec, qspec, vspec],
        out_specs=vspec,
        out_shape=jax.ShapeDtypeStruct((m, heads * dv), F32),
        scratch_shapes=[pltpu.VMEM((tb, w), F32), pltpu.VMEM((heads, dv, gw), F32)],
        compiler_params=pltpu.CompilerParams(
            dimension_semantics=("arbitrary", "arbitrary"), vmem_limit_bytes=VMEM_LIMIT_BYTES),
        name=name,
    )(q, k, g, v)


def _head_norm_gate(o, gate, w, heads, dv):
    parts = []
    for h in range(heads):
        oh = o[:, h * dv:(h + 1) * dv]
        parts.append(oh * lax.rsqrt(jnp.mean(oh * oh, axis=-1, keepdims=True) + EPS))
    y = jnp.concatenate(parts, axis=-1) * w
    return (y * _silu(gate)).astype(BF16)


def _tail_kernel(x_ref, oa_ref, ob_ref, gga_ref, ggb_ref, nwa_ref, nwb_ref,
                 w_out_ref, post_mix_ref, pre_mlp_ref, post_mlp_ref, w_up_ref, w_down_ref,
                 out_ref):
    sub = x_ref.shape[0] // TAIL_SUBTILES
    tiles = [slice(s * sub, (s + 1) * sub) for s in range(TAIL_SUBTILES)]
    hs, us = {}, {}

    def out_proj(s):
        r = tiles[s]
        ya = _head_norm_gate(oa_ref[r, :], gga_ref[r, :], nwa_ref[...], GLA_HEADS, GLA_DV)
        yb = _head_norm_gate(ob_ref[r, :], ggb_ref[r, :], nwb_ref[...], HGRN_HEADS, HGRN_DV)
        mix = (jnp.dot(ya, w_out_ref[0:GLA_V, :], preferred_element_type=F32)
               + jnp.dot(yb, w_out_ref[GLA_V:GLA_V + HGRN_V, :], preferred_element_type=F32))
        hs[s] = x_ref[r, :] + _rmsnorm(mix, post_mix_ref[...])

    def up(s):
        us[s] = jnp.dot(_rmsnorm(hs[s], pre_mlp_ref[...]).astype(BF16), w_up_ref[...],
                        preferred_element_type=F32)

    def down(s):
        relu = jnp.maximum(us.pop(s), 0.0)
        m = jnp.dot((relu * relu).astype(BF16), w_down_ref[...], preferred_element_type=F32)
        out_ref[tiles[s], :] = hs.pop(s) + _rmsnorm(m, post_mlp_ref[...])

    for stage in (out_proj, up, down):
        for s in range(TAIL_SUBTILES):
            stage(s)


def _tail_call(x2, oa, ob, gga, ggb, nwa, nwb, w_out, post_mix, pre_mlp, post_mlp, w_up, w_down):
    m = x2.shape[0]
    tm = TAIL_ROWS
    row = lambda width: pl.BlockSpec((tm, width), lambda i: (i, 0))
    full = lambda a: pl.BlockSpec(a.shape, lambda i: (0,) * a.ndim)
    resident = lambda a: pl.BlockSpec(a.shape, lambda i: (0,) * a.ndim,
                                      pipeline_mode=pl.Buffered(1))
    return pl.pallas_call(
        _tail_kernel,
        grid=(m // tm,),
        in_specs=[row(D_MODEL), row(GLA_V), row(HGRN_V), row(GLA_V), row(HGRN_V),
                  full(nwa), full(nwb), resident(w_out), full(post_mix), full(pre_mlp),
                  full(post_mlp), resident(w_up), resident(w_down)],
        out_specs=row(D_MODEL),
        out_shape=jax.ShapeDtypeStruct((m, D_MODEL), F32),
        compiler_params=pltpu.CompilerParams(
            dimension_semantics=("parallel",), vmem_limit_bytes=VMEM_LIMIT_BYTES),
        name="out_proj_mlp",
    )(x2, oa, ob, gga, ggb, nwa, nwb, w_out, post_mix, pre_mlp, post_mlp, w_up, w_down)


def kernel(x, w_in, w_gk_up, b_gk, gla_norm_w, hgrn_norm_w, hgrn_lower_bounds, w_out,
           pre_mix_norm, post_mix_norm, pre_mlp_norm, post_mlp_norm, w_up, w_down):
    batch, seq, d = x.shape
    x2 = x.reshape(batch * seq, d)
    l = 0
    wi = w_in[l]
    w_a = wi[:, :GLA_COLS].astype(BF16)
    w_lr = jnp.pad(wi[:, GLA_COLS:GLA_COLS + GLA_GATE_RANK],
                   ((0, 0), (0, LANES - GLA_GATE_RANK))).astype(BF16)
    w_b = wi[:, GLA_COLS + GLA_GATE_RANK:].astype(BF16)
    wgk_p = jnp.pad(w_gk_up[l], ((0, LANES - GLA_GATE_RANK), (0, 0))).astype(BF16)
    row2 = lambda a: a.reshape(1, -1)

    qa, ka, ga, va, gga, qb, kb, gb, vb, ggb = _proj_call(
        x2, row2(pre_mix_norm[l]), w_a, w_lr, w_b, wgk_p, row2(b_gk[l]), hgrn_lower_bounds)
    oa = _att_call(qa, ka, ga, va, batch=batch, heads=GLA_HEADS, dk=GLA_DK, dv=GLA_DV,
                   name="gla_attention")
    ob = _att_call(qb, kb, gb, vb, batch=batch, heads=HGRN_HEADS, dk=HGRN_DF, dv=HGRN_DV,
                   name="hgrn_attention")
    out = _tail_call(x2, oa, ob, gga, ggb, row2(gla_norm_w[l]), row2(hgrn_norm_w[l]),
                     w_out[l].astype(BF16), row2(post_mix_norm[l]), row2(pre_mlp_norm[l]),
                     row2(post_mlp_norm[l]), w_up[l].astype(BF16), w_down[l].astype(BF16))
    return out.reshape(batch, seq, d)
```

```python
import functools

import jax
import jax.numpy as jnp
from jax import lax
from jax.experimental import pallas as pl
from jax.experimental.pallas import tpu as pltpu

F32 = jnp.float32
BF16 = jnp.bfloat16

D_MODEL = 1024
GLA_HEADS, GLA_DK, GLA_DV = 4, 64, 128
GLA_QK = GLA_HEADS * GLA_DK
GLA_V = GLA_HEADS * GLA_DV
GLA_GATE_RANK = 16
GLA_GATE_NORM = 16.0
HGRN_HEADS, HGRN_DF, HGRN_DV = 4, 128, 128
HGRN_F = HGRN_HEADS * HGRN_DF
HGRN_V = HGRN_HEADS * HGRN_DV
D_FF = 4 * D_MODEL
EPS = 1e-6
LOG2_E = 1.4426950408889634

LANES = 128
VMEM_LIMIT_BYTES = 56 * 1024 * 1024

_C_GQ = 0
_C_GK = _C_GQ + GLA_QK
_C_GV = _C_GK + GLA_QK
_C_GG = _C_GV + GLA_V
GLA_COLS = _C_GG + GLA_V
_C_HQ = 0
_C_HF = _C_HQ + HGRN_F
_C_HI = _C_HF + HGRN_F
_C_HG = _C_HI + HGRN_V
HGRN_COLS = _C_HG + HGRN_V

PROJ_ROWS = 512
ATT_ROWS = 512
CUMSUM_ROWS = 256
CHUNK = 64
SUB = 16
NLAG = CHUNK // SUB
LAG_OFFSETS = [sum(CHUNK - SUB * m for m in range(l)) for l in range(NLAG + 1)]
LAG_ROWS = LAG_OFFSETS[NLAG]
TAIL_ROWS = 512
TAIL_SUBTILES = 2


def _rmsnorm(x, w):
    return x * lax.rsqrt(jnp.mean(x * x, axis=-1, keepdims=True) + EPS) * w


def _sigmoid(x):
    return 1.0 / (1.0 + jnp.exp(-x))


def _silu(x):
    return x * _sigmoid(x)


def _log_sigmoid(x):
    return -(jnp.maximum(-x, 0.0) + jnp.log(1.0 + jnp.exp(-jnp.abs(x))))


def _proj_kernel(x_ref, wn_ref, wa_ref, wlr_ref, wb_ref, wgk_ref, bgk_ref, lbl_ref,
                 qa_ref, ka_ref, ga_ref, va_ref, gga_ref,
                 qb_ref, kb_ref, gb_ref, vb_ref, ggb_ref):
    hb = _rmsnorm(x_ref[...], wn_ref[...]).astype(BF16)

    def proj(w_ref, lo, width):
        return jnp.dot(hb, w_ref[:, lo:lo + width], preferred_element_type=F32)

    qa_ref[...] = proj(wa_ref, _C_GQ, GLA_QK) * (GLA_DK ** -0.5)
    ka_ref[...] = proj(wa_ref, _C_GK, GLA_QK)
    va_ref[...] = proj(wa_ref, _C_GV, GLA_V).astype(BF16)
    gga_ref[...] = proj(wa_ref, _C_GG, GLA_V)
    glr = proj(wlr_ref, 0, LANES).astype(BF16)
    gk = jnp.dot(glr, wgk_ref[...], preferred_element_type=F32) + bgk_ref[...]
    ga_ref[...] = _log_sigmoid(gk) * (LOG2_E / GLA_GATE_NORM)

    lbl = lbl_ref[...]
    e = jnp.exp(lbl - jnp.max(lbl, axis=0, keepdims=True))
    lb = e[0:1, :] / jnp.sum(e, axis=0, keepdims=True)
    f = lb + (1.0 - lb) * _sigmoid(proj(wb_ref, _C_HF, HGRN_F))
    qb_ref[...] = _silu(proj(wb_ref, _C_HQ, HGRN_F))
    kb_ref[...] = 1.0 - f
    gb_ref[...] = jnp.log(f) * LOG2_E
    vb_ref[...] = proj(wb_ref, _C_HI, HGRN_V).astype(BF16)
    ggb_ref[...] = proj(wb_ref, _C_HG, HGRN_V)


def _proj_call(x2, wn, w_a, w_lr, w_b, wgk_p, bgk, lbl):
    m = x2.shape[0]
    tm = PROJ_ROWS
    row = lambda width: pl.BlockSpec((tm, width), lambda i: (i, 0))
    full = lambda a: pl.BlockSpec(a.shape, lambda i: (0,) * a.ndim)
    out_shapes = (
        jax.ShapeDtypeStruct((m, GLA_QK), F32), jax.ShapeDtypeStruct((m, GLA_QK), F32),
        jax.ShapeDtypeStruct((m, GLA_QK), F32), jax.ShapeDtypeStruct((m, GLA_V), BF16),
        jax.ShapeDtypeStruct((m, GLA_V), F32),
        jax.ShapeDtypeStruct((m, HGRN_F), F32), jax.ShapeDtypeStruct((m, HGRN_F), F32),
        jax.ShapeDtypeStruct((m, HGRN_F), F32), jax.ShapeDtypeStruct((m, HGRN_V), BF16),
        jax.ShapeDtypeStruct((m, HGRN_V), F32),
    )
    return pl.pallas_call(
        _proj_kernel,
        grid=(m // tm,),
        in_specs=[row(D_MODEL), full(wn), full(w_a), full(w_lr), full(w_b), full(wgk_p),
                  full(bgk), full(lbl)],
        out_specs=tuple(row(s.shape[1]) for s in out_shapes),
        out_shape=out_shapes,
        compiler_params=pltpu.CompilerParams(
            dimension_semantics=("parallel",), vmem_limit_bytes=VMEM_LIMIT_BYTES),
        name="in_proj_gates",
    )(x2, wn, w_a, w_lr, w_b, wgk_p, bgk, lbl)


def _mixer_program(q_ref, k_ref, g_ref, v_ref, o_ref, b_ref, st_ref, *, heads, dk, dv):
    gw = max(dk, LANES)
    hpg = gw // dk
    ngroups = heads // hpg

    @pl.when(pl.program_id(1) == 0)
    def _():
        st_ref[...] = jnp.zeros_like(st_ref)

    ri = lax.broadcasted_iota(jnp.int32, (CUMSUM_ROWS, CUMSUM_ROWS), 0)
    ci = lax.broadcasted_iota(jnp.int32, (CUMSUM_ROWS, CUMSUM_ROWS), 1)
    tri = jnp.where((ri // CHUNK == ci // CHUNK) & (ci <= ri), 1.0, 0.0).astype(BF16)

    def cumsum_block(r0):
        blk = slice(r0, r0 + CUMSUM_ROWS)
        g = g_ref[blk, :]
        g_hi = g.astype(BF16)
        g_lo = (g - g_hi.astype(F32)).astype(BF16)
        b_ref[blk, :] = (jnp.dot(tri, g_hi, preferred_element_type=F32)
                         + jnp.dot(tri, g_lo, preferred_element_type=F32))

    si = lax.broadcasted_iota(jnp.int32, (SUB, CHUNK), 0)
    sj = lax.broadcasted_iota(jnp.int32, (SUB, CHUNK), 1)
    col_block = [sj // SUB == r for r in range(NLAG)]
    diag_block = [col_block[r] & (sj - SUB * r <= si) for r in range(NLAG)]
    lane = lax.broadcasted_iota(jnp.int32, (1, gw), 1)

    head_ids = [(gi, hh) for gi in range(ngroups) for hh in range(hpg)]
    v_of = lambda h, c: v_ref[c * CHUNK:(c + 1) * CHUNK, h * dv:(h + 1) * dv]
    state = {(gi, hh): st_ref[gi * hpg + hh] for gi, hh in head_ids}
    probs, qin, start_state = {}, {}, {}

    def scores_and_state(c):
        r0 = c * CHUNK
        rws = slice(r0, r0 + CHUNK)
        for gi in range(ngroups):
            lanes = slice(gi * gw, (gi + 1) * gw)
            b = b_ref[rws, lanes]
            q = q_ref[rws, lanes]
            k = k_ref[rws, lanes]
            ends = {r: b_ref[r0 + SUB * r + SUB - 1:r0 + SUB * r + SUB, lanes] for r in range(NLAG)}
            ends[-1] = jnp.zeros((1, gw), F32)
            blk = lambda a, r: a[SUB * r:SUB * (r + 1), :]
            rows_of = lambda f: jnp.concatenate(
                [jnp.broadcast_to(f(r), (SUB, gw)) for r in range(NLAG)], axis=0)

            q0 = q * jnp.exp2(b - rows_of(lambda r: ends[r]))
            q1 = q * jnp.exp2(b - rows_of(lambda r: ends[r - 1]))
            k_hat = k * jnp.exp2(rows_of(lambda r: ends[r]) - b)
            lag_rows = [q0.astype(BF16), q1[SUB:, :].astype(BF16)]
            for l in range(2, NLAG):
                lag_rows.append(jnp.concatenate(
                    [blk(q1, r) * jnp.exp2(ends[r - 1] - ends[r - l]) for r in range(l, NLAG)],
                    axis=0).astype(BF16))
            q_lag = jnp.concatenate(lag_rows, axis=0)
            q_in = jnp.concatenate(
                [blk(q1, r) * jnp.exp2(ends[r - 1]) for r in range(NLAG)], axis=0)
            k_dec = jnp.concatenate(
                [blk(k_hat, r) * jnp.exp2(ends[NLAG - 1] - ends[r]) for r in range(NLAG)],
                axis=0).astype(BF16)
            dec = jnp.exp2(ends[NLAG - 1])

            for hh in range(hpg):
                h = gi * hpg + hh
                if hpg == 1:
                    k_h, qin[gi, hh, c] = k_hat.astype(BF16), q_in.astype(BF16)
                else:
                    in_head = (lane >= hh * dk) & (lane < (hh + 1) * dk)
                    k_h = jnp.where(in_head, k_hat, 0.0).astype(BF16)
                    qin[gi, hh, c] = jnp.where(in_head, q_in, 0.0).astype(BF16)
                s_all = lax.dot_general(q_lag, k_h, (((1,), (1,)), ((), ())),
                                        preferred_element_type=F32)
                p_rows = []
                for r in range(NLAG):
                    p_r = jnp.where(diag_block[r], blk(s_all, r), 0.0)
                    for l in range(1, r + 1):
                        src = LAG_OFFSETS[l] + SUB * (r - l)
                        p_r = jnp.where(col_block[r - l], s_all[src:src + SUB, :], p_r)
                    p_rows.append(p_r)
                probs[gi, hh, c] = jnp.concatenate(p_rows, axis=0).astype(BF16)
                upd = lax.dot_general(v_of(h, c), k_dec, (((0,), (0,)), ((), ())),
                                      preferred_element_type=F32)
                start_state[gi, hh, c] = state[gi, hh].astype(BF16)
                state[gi, hh] = state[gi, hh] * dec + upd

    def outputs(c):
        for gi, hh in head_ids:
            h = gi * hpg + hh
            o = (jnp.dot(probs.pop((gi, hh, c)), v_of(h, c), preferred_element_type=F32)
                 + lax.dot_general(qin.pop((gi, hh, c)), start_state.pop((gi, hh, c)),
                                   (((1,), (1,)), ((), ())), preferred_element_type=F32))
            o_ref[c * CHUNK:(c + 1) * CHUNK, h * dv:(h + 1) * dv] = o

    def store_state():
        for gi, hh in head_ids:
            st_ref[gi * hpg + hh] = state[gi, hh]

    return cumsum_block, scores_and_state, outputs, store_state


_MIXERS = (dict(heads=GLA_HEADS, dk=GLA_DK, dv=GLA_DV), dict(heads=HGRN_HEADS, dk=HGRN_DF, dv=HGRN_DV))


def _att_kernel(qa_ref, ka_ref, ga_ref, va_ref, qb_ref, kb_ref, gb_ref, vb_ref,
                oa_ref, ob_ref, ba_ref, sta_ref, bb_ref, stb_ref):
    nchunk = qa_ref.shape[0] // CHUNK
    programs = [
        _mixer_program(qa_ref, ka_ref, ga_ref, va_ref, oa_ref, ba_ref, sta_ref, **_MIXERS[0]),
        _mixer_program(qb_ref, kb_ref, gb_ref, vb_ref, ob_ref, bb_ref, stb_ref, **_MIXERS[1]),
    ]
    chunks_per_block = CUMSUM_ROWS // CHUNK
    for cumsum_block, _, _, _ in programs:
        cumsum_block(0)
    for c in range(nchunk):
        for cumsum_block, scores_and_state, outputs, _ in programs:
            scores_and_state(c)
            if c >= 1:
                outputs(c - 1)
            if (c + 2) % chunks_per_block == 0 and (c + 2) < nchunk:
                cumsum_block((c + 2) * CHUNK)
    for _, _, outputs, store_state in programs:
        outputs(nchunk - 1)
        store_state()


def _att_call(qa, ka, ga, va, qb, kb, gb, vb, *, batch):
    m = qa.shape[0]
    nt = m // batch // ATT_ROWS
    spec = lambda a: pl.BlockSpec((ATT_ROWS, a.shape[1]), lambda b, i: (b * nt + i, 0))
    scratch = []
    for q, mix in ((qa, _MIXERS[0]), (qb, _MIXERS[1])):
        scratch += [pltpu.VMEM((ATT_ROWS, q.shape[1]), F32),
                    pltpu.VMEM((mix["heads"], mix["dv"], max(mix["dk"], LANES)), F32)]
    ins = (qa, ka, ga, va, qb, kb, gb, vb)
    return pl.pallas_call(
        _att_kernel,
        grid=(batch, nt),
        in_specs=[spec(a) for a in ins],
        out_specs=(spec(va), spec(vb)),
        out_shape=(jax.ShapeDtypeStruct(va.shape, F32), jax.ShapeDtypeStruct(vb.shape, F32)),
        scratch_shapes=scratch,
        compiler_params=pltpu.CompilerParams(
            dimension_semantics=("arbitrary", "arbitrary"), vmem_limit_bytes=VMEM_LIMIT_BYTES),
        name="gla_hgrn_attention",
    )(*ins)


def _head_norm_gate(o, gate, w, heads, dv):
    parts = []
    for h in range(heads):
        oh = o[:, h * dv:(h + 1) * dv]
        parts.append(oh * lax.rsqrt(jnp.mean(oh * oh, axis=-1, keepdims=True) + EPS))
    y = jnp.concatenate(parts, axis=-1) * w
    return (y * _silu(gate)).astype(BF16)


def _tail_kernel(x_ref, oa_ref, ob_ref, gga_ref, ggb_ref, nwa_ref, nwb_ref,
                 w_out_ref, post_mix_ref, pre_mlp_ref, post_mlp_ref, w_up_ref, w_down_ref,
                 out_ref):
    sub = x_ref.shape[0] // TAIL_SUBTILES
    tiles = [slice(s * sub, (s + 1) * sub) for s in range(TAIL_SUBTILES)]
    hs, us = {}, {}

    def out_proj(s):
        r = tiles[s]
        ya = _head_norm_gate(oa_ref[r, :], gga_ref[r, :], nwa_ref[...], GLA_HEADS, GLA_DV)
        yb = _head_norm_gate(ob_ref[r, :], ggb_ref[r, :], nwb_ref[...], HGRN_HEADS, HGRN_DV)
        mix = (jnp.dot(ya, w_out_ref[0:GLA_V, :], preferred_element_type=F32)
               + jnp.dot(yb, w_out_ref[GLA_V:GLA_V + HGRN_V, :], preferred_element_type=F32))
        hs[s] = x_ref[r, :] + _rmsnorm(mix, post_mix_ref[...])

    def up(s):
        us[s] = jnp.dot(_rmsnorm(hs[s], pre_mlp_ref[...]).astype(BF16), w_up_ref[...],
                        preferred_element_type=F32)

    def down(s):
        relu = jnp.maximum(us.pop(s), 0.0)
        m = jnp.dot((relu * relu).astype(BF16), w_down_ref[...], preferred_element_type=F32)
        out_ref[tiles[s], :] = hs.pop(s) + _rmsnorm(m, post_mlp_ref[...])

    for stage in (out_proj, up, down):
        for s in range(TAIL_SUBTILES):
            stage(s)


def _tail_call(x2, oa, ob, gga, ggb, nwa, nwb, w_out, post_mix, pre_mlp, post_mlp, w_up, w_down):
    m = x2.shape[0]
    tm = TAIL_ROWS
    row = lambda width: pl.BlockSpec((tm, width), lambda i: (i, 0))
    full = lambda a: pl.BlockSpec(a.shape, lambda i: (0,) * a.ndim)
    resident = lambda a: pl.BlockSpec(a.shape, lambda i: (0,) * a.ndim,
                                      pipeline_mode=pl.Buffered(1))
    return pl.pallas_call(
        _tail_kernel,
        grid=(m // tm,),
        in_specs=[row(D_MODEL), row(GLA_V), row(HGRN_V), row(GLA_V), row(HGRN_V),
                  full(nwa), full(nwb), resident(w_out), full(post_mix), full(pre_mlp),
                  full(post_mlp), resident(w_up), resident(w_down)],
        out_specs=row(D_MODEL),
        out_shape=jax.ShapeDtypeStruct((m, D_MODEL), F32),
        compiler_params=pltpu.CompilerParams(
            dimension_semantics=("parallel",), vmem_limit_bytes=VMEM_LIMIT_BYTES),
        name="out_proj_mlp",
    )(x2, oa, ob, gga, ggb, nwa, nwb, w_out, post_mix, pre_mlp, post_mlp, w_up, w_down)


def kernel(x, w_in, w_gk_up, b_gk, gla_norm_w, hgrn_norm_w, hgrn_lower_bounds, w_out,
           pre_mix_norm, post_mix_norm, pre_mlp_norm, post_mlp_norm, w_up, w_down):
    batch, seq, d = x.shape
    x2 = x.reshape(batch * seq, d)
    l = 0
    wi = w_in[l]
    w_a = wi[:, :GLA_COLS].astype(BF16)
    w_lr = jnp.pad(wi[:, GLA_COLS:GLA_COLS + GLA_GATE_RANK],
                   ((0, 0), (0, LANES - GLA_GATE_RANK))).astype(BF16)
    w_b = wi[:, GLA_COLS + GLA_GATE_RANK:].astype(BF16)
    wgk_p = jnp.pad(w_gk_up[l], ((0, LANES - GLA_GATE_RANK), (0, 0))).astype(BF16)
    row2 = lambda a: a.reshape(1, -1)

    qa, ka, ga, va, gga, qb, kb, gb, vb, ggb = _proj_call(
        x2, row2(pre_mix_norm[l]), w_a, w_lr, w_b, wgk_p, row2(b_gk[l]), hgrn_lower_bounds)
    oa, ob = _att_call(qa, ka, ga, va, qb, kb, gb, vb, batch=batch)
    out = _tail_call(x2, oa, ob, gga, ggb, row2(gla_norm_w[l]), row2(hgrn_norm_w[l]),
                     w_out[l].astype(BF16), row2(post_mix_norm[l]), row2(pre_mlp_norm[l]),
                     row2(post_mlp_norm[l]), w_up[l].astype(BF16), w_down[l].astype(BF16))
    return out.reshape(batch, seq, d)
```

```python
import functools

import jax
import jax.numpy as jnp
from jax import lax
from jax.experimental import pallas as pl
from jax.experimental.pallas import tpu as pltpu

F32 = jnp.float32
BF16 = jnp.bfloat16

D_MODEL = 1024
GLA_HEADS, GLA_DK, GLA_DV = 4, 64, 128
GLA_QK = GLA_HEADS * GLA_DK
GLA_V = GLA_HEADS * GLA_DV
GLA_GATE_RANK = 16
GLA_GATE_NORM = 16.0
HGRN_HEADS, HGRN_DF, HGRN_DV = 4, 128, 128
HGRN_F = HGRN_HEADS * HGRN_DF
HGRN_V = HGRN_HEADS * HGRN_DV
D_FF = 4 * D_MODEL
EPS = 1e-6
LOG2_E = 1.4426950408889634

LANES = 128
VMEM_LIMIT_BYTES = 56 * 1024 * 1024

_C_GQ = 0
_C_GK = _C_GQ + GLA_QK
_C_GV = _C_GK + GLA_QK
_C_GG = _C_GV + GLA_V
GLA_COLS = _C_GG + GLA_V
_C_HQ = 0
_C_HF = _C_HQ + HGRN_F
_C_HI = _C_HF + HGRN_F
_C_HG = _C_HI + HGRN_V
HGRN_COLS = _C_HG + HGRN_V

PROJ_ROWS = 512
ATT_ROWS = 512
CUMSUM_ROWS = 256
CHUNK = 64
SUB = 16
NLAG = CHUNK // SUB
LAG_OFFSETS = [sum(CHUNK - SUB * m for m in range(l)) for l in range(NLAG + 1)]
LAG_ROWS = LAG_OFFSETS[NLAG]
TAIL_ROWS = 512
TAIL_SUBTILES = 2


def _rmsnorm(x, w):
    return x * lax.rsqrt(jnp.mean(x * x, axis=-1, keepdims=True) + EPS) * w


def _sigmoid(x):
    return 1.0 / (1.0 + jnp.exp(-x))


def _silu(x):
    return x * _sigmoid(x)


def _log_sigmoid(x):
    return -(jnp.maximum(-x, 0.0) + jnp.log(1.0 + jnp.exp(-jnp.abs(x))))


def _proj_kernel(x_ref, wn_ref, wa_ref, wbraw_ref, wgk_ref, bgk_ref, lbl_ref,
                 qa_ref, ka_ref, ga_ref, va_ref, gga_ref,
                 qb_ref, kb_ref, gb_ref, vb_ref, ggb_ref, wb_ref):
    @pl.when(pl.program_id(0) == 0)
    def _():
        wb_ref[...] = wbraw_ref[:, GLA_GATE_RANK:GLA_GATE_RANK + HGRN_COLS]

    hb = _rmsnorm(x_ref[...], wn_ref[...]).astype(BF16)

    def proj(w_ref, lo, width):
        return jnp.dot(hb, w_ref[:, lo:lo + width], preferred_element_type=F32)

    qa_ref[...] = proj(wa_ref, _C_GQ, GLA_QK) * (GLA_DK ** -0.5)
    ka_ref[...] = proj(wa_ref, _C_GK, GLA_QK)
    va_ref[...] = proj(wa_ref, _C_GV, GLA_V).astype(BF16)
    gga_ref[...] = proj(wa_ref, _C_GG, GLA_V)
    glr = proj(wbraw_ref, 0, LANES).astype(BF16)
    gk = jnp.dot(glr, wgk_ref[...], preferred_element_type=F32) + bgk_ref[...]
    ga_ref[...] = _log_sigmoid(gk) * (LOG2_E / GLA_GATE_NORM)

    lbl = lbl_ref[...]
    e = jnp.exp(lbl - jnp.max(lbl, axis=0, keepdims=True))
    lb = e[0:1, :] / jnp.sum(e, axis=0, keepdims=True)
    f = lb + (1.0 - lb) * _sigmoid(proj(wb_ref, _C_HF, HGRN_F))
    qb_ref[...] = _silu(proj(wb_ref, _C_HQ, HGRN_F))
    kb_ref[...] = 1.0 - f
    gb_ref[...] = jnp.log(f) * LOG2_E
    vb_ref[...] = proj(wb_ref, _C_HI, HGRN_V).astype(BF16)
    ggb_ref[...] = proj(wb_ref, _C_HG, HGRN_V)


def _proj_call(x2, wn, w_a, w_braw, wgk_p, bgk, lbl):
    m = x2.shape[0]
    tm = PROJ_ROWS
    row = lambda width: pl.BlockSpec((tm, width), lambda i: (i, 0))
    full = lambda a: pl.BlockSpec(a.shape, lambda i: (0,) * a.ndim)
    out_shapes = (
        jax.ShapeDtypeStruct((m, GLA_QK), F32), jax.ShapeDtypeStruct((m, GLA_QK), F32),
        jax.ShapeDtypeStruct((m, GLA_QK), F32), jax.ShapeDtypeStruct((m, GLA_V), BF16),
        jax.ShapeDtypeStruct((m, GLA_V), F32),
        jax.ShapeDtypeStruct((m, HGRN_F), F32), jax.ShapeDtypeStruct((m, HGRN_F), F32),
        jax.ShapeDtypeStruct((m, HGRN_F), F32), jax.ShapeDtypeStruct((m, HGRN_V), BF16),
        jax.ShapeDtypeStruct((m, HGRN_V), F32),
    )
    return pl.pallas_call(
        _proj_kernel,
        grid=(m // tm,),
        in_specs=[row(D_MODEL), full(wn), full(w_a), full(w_braw), full(wgk_p),
                  full(bgk), full(lbl)],
        out_specs=tuple(row(s.shape[1]) for s in out_shapes),
        out_shape=out_shapes,
        scratch_shapes=[pltpu.VMEM((D_MODEL, HGRN_COLS), BF16)],
        compiler_params=pltpu.CompilerParams(
            dimension_semantics=("arbitrary",), vmem_limit_bytes=VMEM_LIMIT_BYTES),
        name="in_proj_gates",
    )(x2, wn, w_a, w_braw, wgk_p, bgk, lbl)


def _mixer_program(q_ref, k_ref, g_ref, v_ref, o_ref, b_ref, st_ref, *, heads, dk, dv):
    gw = max(dk, LANES)
    hpg = gw // dk
    ngroups = heads // hpg

    @pl.when(pl.program_id(1) == 0)
    def _():
        st_ref[...] = jnp.zeros_like(st_ref)

    ri = lax.broadcasted_iota(jnp.int32, (CUMSUM_ROWS, CUMSUM_ROWS), 0)
    ci = lax.broadcasted_iota(jnp.int32, (CUMSUM_ROWS, CUMSUM_ROWS), 1)
    tri = jnp.where((ri // CHUNK == ci // CHUNK) & (ci <= ri), 1.0, 0.0).astype(BF16)

    def cumsum_block(r0):
        blk = slice(r0, r0 + CUMSUM_ROWS)
        g = g_ref[blk, :]
        g_hi = g.astype(BF16)
        g_lo = (g - g_hi.astype(F32)).astype(BF16)
        b_ref[blk, :] = (jnp.dot(tri, g_hi, preferred_element_type=F32)
                         + jnp.dot(tri, g_lo, preferred_element_type=F32))

    si = lax.broadcasted_iota(jnp.int32, (SUB, CHUNK), 0)
    sj = lax.broadcasted_iota(jnp.int32, (SUB, CHUNK), 1)
    col_block = [sj // SUB == r for r in range(NLAG)]
    diag_block = [col_block[r] & (sj - SUB * r <= si) for r in range(NLAG)]
    lane = lax.broadcasted_iota(jnp.int32, (1, gw), 1)

    head_ids = [(gi, hh) for gi in range(ngroups) for hh in range(hpg)]
    v_of = lambda h, c: v_ref[c * CHUNK:(c + 1) * CHUNK, h * dv:(h + 1) * dv]
    state = {(gi, hh): st_ref[gi * hpg + hh] for gi, hh in head_ids}
    probs, qin, start_state = {}, {}, {}

    def scores_and_state(c):
        r0 = c * CHUNK
        rws = slice(r0, r0 + CHUNK)
        for gi in range(ngroups):
            lanes = slice(gi * gw, (gi + 1) * gw)
            b = b_ref[rws, lanes]
            q = q_ref[rws, lanes]
            k = k_ref[rws, lanes]
            ends = {r: b_ref[r0 + SUB * r + SUB - 1:r0 + SUB * r + SUB, lanes] for r in range(NLAG)}
            ends[-1] = jnp.zeros((1, gw), F32)
            blk = lambda a, r: a[SUB * r:SUB * (r + 1), :]
            rows_of = lambda f: jnp.concatenate(
                [jnp.broadcast_to(f(r), (SUB, gw)) for r in range(NLAG)], axis=0)

            q0 = q * jnp.exp2(b - rows_of(lambda r: ends[r]))
            q1 = q * jnp.exp2(b - rows_of(lambda r: ends[r - 1]))
            k_hat = k * jnp.exp2(rows_of(lambda r: ends[r]) - b)
            lag_rows = [q0.astype(BF16), q1[SUB:, :].astype(BF16)]
            for l in range(2, NLAG):
                lag_rows.append(jnp.concatenate(
                    [blk(q1, r) * jnp.exp2(ends[r - 1] - ends[r - l]) for r in range(l, NLAG)],
                    axis=0).astype(BF16))
            q_lag = jnp.concatenate(lag_rows, axis=0)
            q_in = jnp.concatenate(
                [blk(q1, r) * jnp.exp2(ends[r - 1]) for r in range(NLAG)], axis=0)
            k_dec = jnp.concatenate(
                [blk(k_hat, r) * jnp.exp2(ends[NLAG - 1] - ends[r]) for r in range(NLAG)],
                axis=0).astype(BF16)
            dec = jnp.exp2(ends[NLAG - 1])

            for hh in range(hpg):
                h = gi * hpg + hh
                if hpg == 1:
                    k_h, qin[gi, hh, c] = k_hat.astype(BF16), q_in.astype(BF16)
                else:
                    in_head = (lane >= hh * dk) & (lane < (hh + 1) * dk)
                    k_h = jnp.where(in_head, k_hat, 0.0).astype(BF16)
                    qin[gi, hh, c] = jnp.where(in_head, q_in, 0.0).astype(BF16)
                s_all = lax.dot_general(q_lag, k_h, (((1,), (1,)), ((), ())),
                                        preferred_element_type=F32)
                p_rows = []
                for r in range(NLAG):
                    p_r = jnp.where(diag_block[r], blk(s_all, r), 0.0)
                    for l in range(1, r + 1):
                        src = LAG_OFFSETS[l] + SUB * (r - l)
                        p_r = jnp.where(col_block[r - l], s_all[src:src + SUB, :], p_r)
                    p_rows.append(p_r)
                probs[gi, hh, c] = jnp.concatenate(p_rows, axis=0).astype(BF16)
                upd = lax.dot_general(v_of(h, c), k_dec, (((0,), (0,)), ((), ())),
                                      preferred_element_type=F32)
                start_state[gi, hh, c] = state[gi, hh].astype(BF16)
                state[gi, hh] = state[gi, hh] * dec + upd

    def outputs(c):
        for gi, hh in head_ids:
            h = gi * hpg + hh
            o = (jnp.dot(probs.pop((gi, hh, c)), v_of(h, c), preferred_element_type=F32)
                 + lax.dot_general(qin.pop((gi, hh, c)), start_state.pop((gi, hh, c)),
                                   (((1,), (1,)), ((), ())), preferred_element_type=F32))
            o_ref[c * CHUNK:(c + 1) * CHUNK, h * dv:(h + 1) * dv] = o

    def store_state():
        for gi, hh in head_ids:
            st_ref[gi * hpg + hh] = state[gi, hh]

    return cumsum_block, scores_and_state, outputs, store_state


_MIXERS = (dict(heads=GLA_HEADS, dk=GLA_DK, dv=GLA_DV), dict(heads=HGRN_HEADS, dk=HGRN_DF, dv=HGRN_DV))


def _att_kernel(qa_ref, ka_ref, ga_ref, va_ref, qb_ref, kb_ref, gb_ref, vb_ref,
                oa_ref, ob_ref, ba_ref, sta_ref, bb_ref, stb_ref):
    nchunk = qa_ref.shape[0] // CHUNK
    programs = [
        _mixer_program(qa_ref, ka_ref, ga_ref, va_ref, oa_ref, ba_ref, sta_ref, **_MIXERS[0]),
        _mixer_program(qb_ref, kb_ref, gb_ref, vb_ref, ob_ref, bb_ref, stb_ref, **_MIXERS[1]),
    ]
    chunks_per_block = CUMSUM_ROWS // CHUNK
    for cumsum_block, _, _, _ in programs:
        cumsum_block(0)
    for c in range(nchunk):
        for cumsum_block, scores_and_state, outputs, _ in programs:
            scores_and_state(c)
            if c >= 1:
                outputs(c - 1)
            if (c + 2) % chunks_per_block == 0 and (c + 2) < nchunk:
                cumsum_block((c + 2) * CHUNK)
    for _, _, outputs, store_state in programs:
        outputs(nchunk - 1)
        store_state()


def _att_call(qa, ka, ga, va, qb, kb, gb, vb, *, batch):
    m = qa.shape[0]
    nt = m // batch // ATT_ROWS
    spec = lambda a: pl.BlockSpec((ATT_ROWS, a.shape[1]), lambda b, i: (b * nt + i, 0))
    scratch = []
    for q, mix in ((qa, _MIXERS[0]), (qb, _MIXERS[1])):
        scratch += [pltpu.VMEM((ATT_ROWS, q.shape[1]), F32),
                    pltpu.VMEM((mix["heads"], mix["dv"], max(mix["dk"], LANES)), F32)]
    ins = (qa, ka, ga, va, qb, kb, gb, vb)
    return pl.pallas_call(
        _att_kernel,
        grid=(batch, nt),
        in_specs=[spec(a) for a in ins],
        out_specs=(spec(va), spec(vb)),
        out_shape=(jax.ShapeDtypeStruct(va.shape, F32), jax.ShapeDtypeStruct(vb.shape, F32)),
        scratch_shapes=scratch,
        compiler_params=pltpu.CompilerParams(
            dimension_semantics=("arbitrary", "arbitrary"), vmem_limit_bytes=VMEM_LIMIT_BYTES),
        name="gla_hgrn_attention",
    )(*ins)


def _head_norm_gate(o, gate, w, heads, dv):
    parts = []
    for h in range(heads):
        oh = o[:, h * dv:(h + 1) * dv]
        parts.append(oh * lax.rsqrt(jnp.mean(oh * oh, axis=-1, keepdims=True) + EPS))
    y = jnp.concatenate(parts, axis=-1) * w
    return (y * _silu(gate)).astype(BF16)


def _tail_kernel(x_ref, oa_ref, ob_ref, gga_ref, ggb_ref, nwa_ref, nwb_ref,
                 w_out_ref, post_mix_ref, pre_mlp_ref, post_mlp_ref, w_up_ref, w_down_ref,
                 out_ref):
    sub = x_ref.shape[0] // TAIL_SUBTILES
    tiles = [slice(s * sub, (s + 1) * sub) for s in range(TAIL_SUBTILES)]
    hs, us = {}, {}

    def out_proj(s):
        r = tiles[s]
        ya = _head_norm_gate(oa_ref[r, :], gga_ref[r, :], nwa_ref[...], GLA_HEADS, GLA_DV)
        yb = _head_norm_gate(ob_ref[r, :], ggb_ref[r, :], nwb_ref[...], HGRN_HEADS, HGRN_DV)
        mix = (jnp.dot(ya, w_out_ref[0:GLA_V, :], preferred_element_type=F32)
               + jnp.dot(yb, w_out_ref[GLA_V:GLA_V + HGRN_V, :], preferred_element_type=F32))
        hs[s] = x_ref[r, :] + _rmsnorm(mix, post_mix_ref[...])

    def up(s):
        us[s] = jnp.dot(_rmsnorm(hs[s], pre_mlp_ref[...]).astype(BF16), w_up_ref[...],
                        preferred_element_type=F32)

    def down(s):
        relu = jnp.maximum(us.pop(s), 0.0)
        m = jnp.dot((relu * relu).astype(BF16), w_down_ref[...], preferred_element_type=F32)
        out_ref[tiles[s], :] = hs.pop(s) + _rmsnorm(m, post_mlp_ref[...])

    for stage in (out_proj, up, down):
        for s in range(TAIL_SUBTILES):
            stage(s)


def _tail_call(x2, oa, ob, gga, ggb, nwa, nwb, w_out, post_mix, pre_mlp, post_mlp, w_up, w_down):
    m = x2.shape[0]
    tm = TAIL_ROWS
    row = lambda width: pl.BlockSpec((tm, width), lambda i: (i, 0))
    full = lambda a: pl.BlockSpec(a.shape, lambda i: (0,) * a.ndim)
    resident = lambda a: pl.BlockSpec(a.shape, lambda i: (0,) * a.ndim,
                                      pipeline_mode=pl.Buffered(1))
    return pl.pallas_call(
        _tail_kernel,
        grid=(m // tm,),
        in_specs=[row(D_MODEL), row(GLA_V), row(HGRN_V), row(GLA_V), row(HGRN_V),
                  full(nwa), full(nwb), resident(w_out), full(post_mix), full(pre_mlp),
                  full(post_mlp), resident(w_up), resident(w_down)],
        out_specs=row(D_MODEL),
        out_shape=jax.ShapeDtypeStruct((m, D_MODEL), F32),
        compiler_params=pltpu.CompilerParams(
            dimension_semantics=("parallel",), vmem_limit_bytes=VMEM_LIMIT_BYTES),
        name="out_proj_mlp",
    )(x2, oa, ob, gga, ggb, nwa, nwb, w_out, post_mix, pre_mlp, post_mlp, w_up, w_down)


def kernel(x, w_in, w_gk_up, b_gk, gla_norm_w, hgrn_norm_w, hgrn_lower_bounds, w_out,
           pre_mix_norm, post_mix_norm, pre_mlp_norm, post_mlp_norm, w_up, w_down):
    batch, seq, d = x.shape
    x2 = x.reshape(batch * seq, d)
    l = 0
    wi = w_in[l]
    w_a = wi[:, :GLA_COLS].astype(BF16)
    w_braw = wi[:, GLA_COLS:].astype(BF16)
    wgk_p = jnp.pad(w_gk_up[l], ((0, LANES - GLA_GATE_RANK), (0, 0))).astype(BF16)
    row2 = lambda a: a.reshape(1, -1)

    qa, ka, ga, va, gga, qb, kb, gb, vb, ggb = _proj_call(
        x2, row2(pre_mix_norm[l]), w_a, w_braw, wgk_p, row2(b_gk[l]), hgrn_lower_bounds)
    oa, ob = _att_call(qa, ka, ga, va, qb, kb, gb, vb, batch=batch)
    out = _tail_call(x2, oa, ob, gga, ggb, row2(gla_norm_w[l]), row2(hgrn_norm_w[l]),
                     w_out[l].astype(BF16), row2(post_mix_norm[l]), row2(pre_mlp_norm[l]),
                     row2(post_mlp_norm[l]), w_up[l].astype(BF16), w_down[l].astype(BF16))
    return out.reshape(batch, seq, d)
```

```python
import functools

import jax
import jax.numpy as jnp
from jax import lax
from jax.experimental import pallas as pl
from jax.experimental.pallas import tpu as pltpu

F32 = jnp.float32
BF16 = jnp.bfloat16

D_MODEL = 1024
GLA_HEADS, GLA_DK, GLA_DV = 4, 64, 128
GLA_QK = GLA_HEADS * GLA_DK
GLA_V = GLA_HEADS * GLA_DV
GLA_GATE_RANK = 16
GLA_GATE_NORM = 16.0
HGRN_HEADS, HGRN_DF, HGRN_DV = 4, 128, 128
HGRN_F = HGRN_HEADS * HGRN_DF
HGRN_V = HGRN_HEADS * HGRN_DV
D_FF = 4 * D_MODEL
EPS = 1e-6
LOG2_E = 1.4426950408889634

LANES = 128
VMEM_LIMIT_BYTES = 56 * 1024 * 1024

_C_GQ = 0
_C_GK = _C_GQ + GLA_QK
_C_GV = _C_GK + GLA_QK
_C_GG = _C_GV + GLA_V
GLA_COLS = _C_GG + GLA_V
_C_HQ = 0
_C_HF = _C_HQ + HGRN_F
_C_HI = _C_HF + HGRN_F
_C_HG = _C_HI + HGRN_V
HGRN_COLS = _C_HG + HGRN_V

PROJ_ROWS = 512
TRANSPOSE_ROWS = 512
ATT_ROWS = 512
CUMSUM_ROWS = 256
CHUNK = 64
SUB = 16
NLAG = CHUNK // SUB
LAG_OFFSETS = [sum(CHUNK - SUB * m for m in range(l)) for l in range(NLAG + 1)]
LAG_ROWS = LAG_OFFSETS[NLAG]
TAIL_ROWS = 512
TAIL_SUBTILES = 2


def _rmsnorm(x, w):
    return x * lax.rsqrt(jnp.mean(x * x, axis=-1, keepdims=True) + EPS) * w


def _sigmoid(x):
    return 1.0 / (1.0 + jnp.exp(-x))


def _silu(x):
    return x * _sigmoid(x)


def _log_sigmoid(x):
    return -(jnp.maximum(-x, 0.0) + jnp.log(1.0 + jnp.exp(-jnp.abs(x))))


def _proj_kernel(x_ref, wn_ref, wt_ref, wgk_ref, bgk_ref, lbl_ref, *rest, n_cast):
    cast_in, rest = rest[:n_cast], rest[n_cast:]
    (qa_ref, ka_ref, ga_ref, va_ref, gga_ref, qb_ref, kb_ref, gb_ref, vb_ref, ggb_ref), rest = (
        rest[:10], rest[10:])
    cast_out, (wa_ref, wbraw_ref, wb_ref) = rest[:n_cast], rest[n_cast:]
    for src, dst in zip(cast_in, cast_out):
        dst[...] = src[...].astype(BF16)

    @pl.when(pl.program_id(0) == 0)
    def _():
        for r in range(0, GLA_COLS, TRANSPOSE_ROWS):
            wa_ref[:, r:r + TRANSPOSE_ROWS] = wt_ref[r:r + TRANSPOSE_ROWS, :].T
        wbraw_ref[...] = wt_ref[GLA_COLS:GLA_COLS + LANES, :].T
        hgrn0 = GLA_COLS + GLA_GATE_RANK
        for r in range(0, HGRN_COLS, TRANSPOSE_ROWS):
            wb_ref[:, r:r + TRANSPOSE_ROWS] = wt_ref[hgrn0 + r:hgrn0 + r + TRANSPOSE_ROWS, :].T

    hb = _rmsnorm(x_ref[...], wn_ref[...]).astype(BF16)

    def proj(w_ref, lo, width):
        return jnp.dot(hb, w_ref[:, lo:lo + width], preferred_element_type=F32)

    qa_ref[...] = proj(wa_ref, _C_GQ, GLA_QK) * (GLA_DK ** -0.5)
    ka_ref[...] = proj(wa_ref, _C_GK, GLA_QK)
    va_ref[...] = proj(wa_ref, _C_GV, GLA_V).astype(BF16)
    gga_ref[...] = proj(wa_ref, _C_GG, GLA_V)
    glr = proj(wbraw_ref, 0, LANES).astype(BF16)
    gk = jnp.dot(glr, wgk_ref[...], preferred_element_type=F32) + bgk_ref[...]
    ga_ref[...] = _log_sigmoid(gk) * (LOG2_E / GLA_GATE_NORM)

    lbl = lbl_ref[...]
    e = jnp.exp(lbl - jnp.max(lbl, axis=0, keepdims=True))
    lb = e[0:1, :] / jnp.sum(e, axis=0, keepdims=True)
    f = lb + (1.0 - lb) * _sigmoid(proj(wb_ref, _C_HF, HGRN_F))
    qb_ref[...] = _silu(proj(wb_ref, _C_HQ, HGRN_F))
    kb_ref[...] = 1.0 - f
    gb_ref[...] = jnp.log(f) * LOG2_E
    vb_ref[...] = proj(wb_ref, _C_HI, HGRN_V).astype(BF16)
    ggb_ref[...] = proj(wb_ref, _C_HG, HGRN_V)


def _proj_call(x2, wn, w_t, wgk_p, bgk, lbl, cast_weights):
    m = x2.shape[0]
    tm = PROJ_ROWS
    steps = m // tm
    row = lambda width: pl.BlockSpec((tm, width), lambda i: (i, 0))
    full = lambda a: pl.BlockSpec(a.shape, lambda i: (0,) * a.ndim)
    resident = lambda a: pl.BlockSpec(a.shape, lambda i: (0,) * a.ndim, pipeline_mode=pl.Buffered(1))
    slab = lambda a: pl.BlockSpec((a.shape[0] // steps, a.shape[1]), lambda i: (i, 0))
    out_shapes = (
        jax.ShapeDtypeStruct((m, GLA_QK), F32), jax.ShapeDtypeStruct((m, GLA_QK), F32),
        jax.ShapeDtypeStruct((m, GLA_QK), F32), jax.ShapeDtypeStruct((m, GLA_V), BF16),
        jax.ShapeDtypeStruct((m, GLA_V), F32),
        jax.ShapeDtypeStruct((m, HGRN_F), F32), jax.ShapeDtypeStruct((m, HGRN_F), F32),
        jax.ShapeDtypeStruct((m, HGRN_F), F32), jax.ShapeDtypeStruct((m, HGRN_V), BF16),
        jax.ShapeDtypeStruct((m, HGRN_V), F32),
    )
    cast_shapes = tuple(jax.ShapeDtypeStruct(w.shape, BF16) for w in cast_weights)
    outs = pl.pallas_call(
        functools.partial(_proj_kernel, n_cast=len(cast_weights)),
        grid=(steps,),
        in_specs=[row(D_MODEL), full(wn), resident(w_t), full(wgk_p), full(bgk), full(lbl)]
                 + [slab(w) for w in cast_weights],
        out_specs=tuple(row(s.shape[1]) for s in out_shapes) + tuple(slab(w) for w in cast_weights),
        out_shape=out_shapes + cast_shapes,
        scratch_shapes=[pltpu.VMEM((D_MODEL, GLA_COLS), BF16), pltpu.VMEM((D_MODEL, LANES), BF16),
                        pltpu.VMEM((D_MODEL, HGRN_COLS), BF16)],
        compiler_params=pltpu.CompilerParams(
            dimension_semantics=("arbitrary",), vmem_limit_bytes=VMEM_LIMIT_BYTES),
        name="in_proj_gates",
    )(x2, wn, w_t, wgk_p, bgk, lbl, *cast_weights)
    return outs[:len(out_shapes)], outs[len(out_shapes):]


def _mixer_program(q_ref, k_ref, g_ref, v_ref, o_ref, b_ref, st_ref, *, heads, dk, dv):
    gw = max(dk, LANES)
    hpg = gw // dk
    ngroups = heads // hpg

    @pl.when(pl.program_id(1) == 0)
    def _():
        st_ref[...] = jnp.zeros_like(st_ref)

    ri = lax.broadcasted_iota(jnp.int32, (CUMSUM_ROWS, CUMSUM_ROWS), 0)
    ci = lax.broadcasted_iota(jnp.int32, (CUMSUM_ROWS, CUMSUM_ROWS), 1)
    tri = jnp.where((ri // CHUNK == ci // CHUNK) & (ci <= ri), 1.0, 0.0).astype(BF16)

    def cumsum_block(r0):
        blk = slice(r0, r0 + CUMSUM_ROWS)
        g = g_ref[blk, :]
        g_hi = g.astype(BF16)
        g_lo = (g - g_hi.astype(F32)).astype(BF16)
        b_ref[blk, :] = (jnp.dot(tri, g_hi, preferred_element_type=F32)
                         + jnp.dot(tri, g_lo, preferred_element_type=F32))

    si = lax.broadcasted_iota(jnp.int32, (SUB, CHUNK), 0)
    sj = lax.broadcasted_iota(jnp.int32, (SUB, CHUNK), 1)
    col_block = [sj // SUB == r for r in range(NLAG)]
    diag_block = [col_block[r] & (sj - SUB * r <= si) for r in range(NLAG)]
    lane = lax.broadcasted_iota(jnp.int32, (1, gw), 1)

    head_ids = [(gi, hh) for gi in range(ngroups) for hh in range(hpg)]
    v_of = lambda h, c: v_ref[c * CHUNK:(c + 1) * CHUNK, h * dv:(h + 1) * dv]
    state = {(gi, hh): st_ref[gi * hpg + hh] for gi, hh in head_ids}
    probs, qin, start_state = {}, {}, {}

    def scores_and_state(c):
        r0 = c * CHUNK
        rws = slice(r0, r0 + CHUNK)
        for gi in range(ngroups):
            lanes = slice(gi * gw, (gi + 1) * gw)
            b = b_ref[rws, lanes]
            q = q_ref[rws, lanes]
            k = k_ref[rws, lanes]
            ends = {r: b_ref[r0 + SUB * r + SUB - 1:r0 + SUB * r + SUB, lanes] for r in range(NLAG)}
            ends[-1] = jnp.zeros((1, gw), F32)
            blk = lambda a, r: a[SUB * r:SUB * (r + 1), :]
            rows_of = lambda f: jnp.concatenate(
                [jnp.broadcast_to(f(r), (SUB, gw)) for r in range(NLAG)], axis=0)

            q0 = q * jnp.exp2(b - rows_of(lambda r: ends[r]))
            q1 = q * jnp.exp2(b - rows_of(lambda r: ends[r - 1]))
            k_hat = k * jnp.exp2(rows_of(lambda r: ends[r]) - b)
            lag_rows = [q0.astype(BF16), q1[SUB:, :].astype(BF16)]
            for l in range(2, NLAG):
                lag_rows.append(jnp.concatenate(
                    [blk(q1, r) * jnp.exp2(ends[r - 1] - ends[r - l]) for r in range(l, NLAG)],
                    axis=0).astype(BF16))
            q_lag = jnp.concatenate(lag_rows, axis=0)
            q_in = jnp.concatenate(
                [blk(q1, r) * jnp.exp2(ends[r - 1]) for r in range(NLAG)], axis=0)
            k_dec = jnp.concatenate(
                [blk(k_hat, r) * jnp.exp2(ends[NLAG - 1] - ends[r]) for r in range(NLAG)],
                axis=0).astype(BF16)
            dec = jnp.exp2(ends[NLAG - 1])

            for hh in range(hpg):
                h = gi * hpg + hh
                if hpg == 1:
                    k_h, qin[gi, hh, c] = k_hat.astype(BF16), q_in.astype(BF16)
                else:
                    in_head = (lane >= hh * dk) & (lane < (hh + 1) * dk)
                    k_h = jnp.where(in_head, k_hat, 0.0).astype(BF16)
                    qin[gi, hh, c] = jnp.where(in_head, q_in, 0.0).astype(BF16)
                s_all = lax.dot_general(q_lag, k_h, (((1,), (1,)), ((), ())),
                                        preferred_element_type=F32)
                p_rows = []
                for r in range(NLAG):
                    p_r = jnp.where(diag_block[r], blk(s_all, r), 0.0)
                    for l in range(1, r + 1):
                        src = LAG_OFFSETS[l] + SUB * (r - l)
                        p_r = jnp.where(col_block[r - l], s_all[src:src + SUB, :], p_r)
                    p_rows.append(p_r)
                probs[gi, hh, c] = jnp.concatenate(p_rows, axis=0).astype(BF16)
                upd = lax.dot_general(v_of(h, c), k_dec, (((0,), (0,)), ((), ())),
                                      preferred_element_type=F32)
                start_state[gi, hh, c] = state[gi, hh].astype(BF16)
                state[gi, hh] = state[gi, hh] * dec + upd

    def outputs(c):
        for gi, hh in head_ids:
            h = gi * hpg + hh
            o = (jnp.dot(probs.pop((gi, hh, c)), v_of(h, c), preferred_element_type=F32)
                 + lax.dot_general(qin.pop((gi, hh, c)), start_state.pop((gi, hh, c)),
                                   (((1,), (1,)), ((), ())), preferred_element_type=F32))
            o_ref[c * CHUNK:(c + 1) * CHUNK, h * dv:(h + 1) * dv] = o

    def store_state():
        for gi, hh in head_ids:
            st_ref[gi * hpg + hh] = state[gi, hh]

    return cumsum_block, scores_and_state, outputs, store_state


_MIXERS = (dict(heads=GLA_HEADS, dk=GLA_DK, dv=GLA_DV), dict(heads=HGRN_HEADS, dk=HGRN_DF, dv=HGRN_DV))


def _att_kernel(qa_ref, ka_ref, ga_ref, va_ref, qb_ref, kb_ref, gb_ref, vb_ref,
                oa_ref, ob_ref, ba_ref, sta_ref, bb_ref, stb_ref):
    nchunk = qa_ref.shape[0] // CHUNK
    programs = [
        _mixer_program(qa_ref, ka_ref, ga_ref, va_ref, oa_ref, ba_ref, sta_ref, **_MIXERS[0]),
        _mixer_program(qb_ref, kb_ref, gb_ref, vb_ref, ob_ref, bb_ref, stb_ref, **_MIXERS[1]),
    ]
    chunks_per_block = CUMSUM_ROWS // CHUNK
    for cumsum_block, _, _, _ in programs:
        cumsum_block(0)
    for c in range(nchunk):
        for cumsum_block, scores_and_state, outputs, _ in programs:
            scores_and_state(c)
            if c >= 1:
                outputs(c - 1)
            if (c + 2) % chunks_per_block == 0 and (c + 2) < nchunk:
                cumsum_block((c + 2) * CHUNK)
    for _, _, outputs, store_state in programs:
        outputs(nchunk - 1)
        store_state()


def _att_call(qa, ka, ga, va, qb, kb, gb, vb, *, batch):
    m = qa.shape[0]
    nt = m // batch // ATT_ROWS
    spec = lambda a: pl.BlockSpec((ATT_ROWS, a.shape[1]), lambda b, i: (b * nt + i, 0))
    scratch = []
    for q, mix in ((qa, _MIXERS[0]), (qb, _MIXERS[1])):
        scratch += [pltpu.VMEM((ATT_ROWS, q.shape[1]), F32),
                    pltpu.VMEM((mix["heads"], mix["dv"], max(mix["dk"], LANES)), F32)]
    ins = (qa, ka, ga, va, qb, kb, gb, vb)
    return pl.pallas_call(
        _att_kernel,
        grid=(batch, nt),
        in_specs=[spec(a) for a in ins],
        out_specs=(spec(va), spec(vb)),
        out_shape=(jax.ShapeDtypeStruct(va.shape, F32), jax.ShapeDtypeStruct(vb.shape, F32)),
        scratch_shapes=scratch,
        compiler_params=pltpu.CompilerParams(
            dimension_semantics=("arbitrary", "arbitrary"), vmem_limit_bytes=VMEM_LIMIT_BYTES),
        name="gla_hgrn_attention",
    )(*ins)


def _head_norm_gate(o, gate, w, heads, dv):
    parts = []
    for h in range(heads):
        oh = o[:, h * dv:(h + 1) * dv]
        parts.append(oh * lax.rsqrt(jnp.mean(oh * oh, axis=-1, keepdims=True) + EPS))
    y = jnp.concatenate(parts, axis=-1) * w
    return (y * _silu(gate)).astype(BF16)


def _tail_kernel(x_ref, oa_ref, ob_ref, gga_ref, ggb_ref, nwa_ref, nwb_ref,
                 w_out_ref, post_mix_ref, pre_mlp_ref, post_mlp_ref, w_up_ref, w_down_ref,
                 out_ref):
    sub = x_ref.shape[0] // TAIL_SUBTILES
    tiles = [slice(s * sub, (s + 1) * sub) for s in range(TAIL_SUBTILES)]
    hs, us = {}, {}

    def out_proj(s):
        r = tiles[s]
        ya = _head_norm_gate(oa_ref[r, :], gga_ref[r, :], nwa_ref[...], GLA_HEADS, GLA_DV)
        yb = _head_norm_gate(ob_ref[r, :], ggb_ref[r, :], nwb_ref[...], HGRN_HEADS, HGRN_DV)
        mix = (jnp.dot(ya, w_out_ref[0:GLA_V, :], preferred_element_type=F32)
               + jnp.dot(yb, w_out_ref[GLA_V:GLA_V + HGRN_V, :], preferred_element_type=F32))
        hs[s] = x_ref[r, :] + _rmsnorm(mix, post_mix_ref[...])

    def up(s):
        us[s] = jnp.dot(_rmsnorm(hs[s], pre_mlp_ref[...]).astype(BF16), w_up_ref[...],
                        preferred_element_type=F32)

    def down(s):
        relu = jnp.maximum(us.pop(s), 0.0)
        m = jnp.dot((relu * relu).astype(BF16), w_down_ref[...], preferred_element_type=F32)
        out_ref[tiles[s], :] = hs.pop(s) + _rmsnorm(m, post_mlp_ref[...])

    for stage in (out_proj, up, down):
        for s in range(TAIL_SUBTILES):
            stage(s)


def _tail_call(x2, oa, ob, gga, ggb, nwa, nwb, w_out, post_mix, pre_mlp, post_mlp, w_up, w_down):
    m = x2.shape[0]
    tm = TAIL_ROWS
    row = lambda width: pl.BlockSpec((tm, width), lambda i: (i, 0))
    full = lambda a: pl.BlockSpec(a.shape, lambda i: (0,) * a.ndim)
    resident = lambda a: pl.BlockSpec(a.shape, lambda i: (0,) * a.ndim,
                                      pipeline_mode=pl.Buffered(1))
    return pl.pallas_call(
        _tail_kernel,
        grid=(m // tm,),
        in_specs=[row(D_MODEL), row(GLA_V), row(HGRN_V), row(GLA_V), row(HGRN_V),
                  full(nwa), full(nwb), resident(w_out), full(post_mix), full(pre_mlp),
                  full(post_mlp), resident(w_up), resident(w_down)],
        out_specs=row(D_MODEL),
        out_shape=jax.ShapeDtypeStruct((m, D_MODEL), F32),
        compiler_params=pltpu.CompilerParams(
            dimension_semantics=("parallel",), vmem_limit_bytes=VMEM_LIMIT_BYTES),
        name="out_proj_mlp",
    )(x2, oa, ob, gga, ggb, nwa, nwb, w_out, post_mix, pre_mlp, post_mlp, w_up, w_down)


def kernel(x, w_in, w_gk_up, b_gk, gla_norm_w, hgrn_norm_w, hgrn_lower_bounds, w_out,
           pre_mix_norm, post_mix_norm, pre_mlp_norm, post_mlp_norm, w_up, w_down):
    batch, seq, d = x.shape
    x2 = x.reshape(batch * seq, d)
    l = 0
    w_t = jnp.transpose(w_in[l]).astype(BF16)
    wgk_p = jnp.pad(w_gk_up[l], ((0, LANES - GLA_GATE_RANK), (0, 0))).astype(BF16)
    row2 = lambda a: a.reshape(1, -1)

    (qa, ka, ga, va, gga, qb, kb, gb, vb, ggb), (w_out_b, w_up_b, w_down_b) = _proj_call(
        x2, row2(pre_mix_norm[l]), w_t, wgk_p, row2(b_gk[l]), hgrn_lower_bounds,
        (w_out[l], w_up[l], w_down[l]))
    oa, ob = _att_call(qa, ka, ga, va, qb, kb, gb, vb, batch=batch)
    out = _tail_call(x2, oa, ob, gga, ggb, row2(gla_norm_w[l]), row2(hgrn_norm_w[l]),
                     w_out_b, row2(post_mix_norm[l]), row2(pre_mlp_norm[l]),
                     row2(post_mlp_norm[l]), w_up_b, w_down_b)
    return out.reshape(batch, seq, d)
```

```python
import functools

import jax
import jax.numpy as jnp
from jax import lax
from jax.experimental import pallas as pl
from jax.experimental.pallas import tpu as pltpu

F32 = jnp.float32
BF16 = jnp.bfloat16

D_MODEL = 1024
GLA_HEADS, GLA_DK, GLA_DV = 4, 64, 128
GLA_QK = GLA_HEADS * GLA_DK
GLA_V = GLA_HEADS * GLA_DV
GLA_GATE_RANK = 16
GLA_GATE_NORM = 16.0
HGRN_HEADS, HGRN_DF, HGRN_DV = 4, 128, 128
HGRN_F = HGRN_HEADS * HGRN_DF
HGRN_V = HGRN_HEADS * HGRN_DV
D_FF = 4 * D_MODEL
EPS = 1e-6
LOG2_E = 1.4426950408889634

LANES = 128
VMEM_LIMIT_BYTES = 56 * 1024 * 1024

_C_GQ = 0
_C_GK = _C_GQ + GLA_QK
_C_GV = _C_GK + GLA_QK
_C_GG = _C_GV + GLA_V
GLA_COLS = _C_GG + GLA_V
_C_HQ = 0
_C_HF = _C_HQ + HGRN_F
_C_HI = _C_HF + HGRN_F
_C_HG = _C_HI + HGRN_V
HGRN_COLS = _C_HG + HGRN_V

MIXER_ROWS = 512
TRANSPOSE_ROWS = 512
CUMSUM_ROWS = 256
CHUNK = 64
SUB = 16
NLAG = CHUNK // SUB
LAG_OFFSETS = [sum(CHUNK - SUB * m for m in range(l)) for l in range(NLAG + 1)]
LAG_ROWS = LAG_OFFSETS[NLAG]
TAIL_ROWS = 512
TAIL_SUBTILES = 2


def _rmsnorm(x, w):
    return x * lax.rsqrt(jnp.mean(x * x, axis=-1, keepdims=True) + EPS) * w


def _sigmoid(x):
    return 1.0 / (1.0 + jnp.exp(-x))


def _silu(x):
    return x * _sigmoid(x)


def _log_sigmoid(x):
    return -(jnp.maximum(-x, 0.0) + jnp.log(1.0 + jnp.exp(-jnp.abs(x))))


def _transpose_projection_weight(wt_ref, wa_ref, wbraw_ref, wb_ref):
    for r in range(0, GLA_COLS, TRANSPOSE_ROWS):
        wa_ref[:, r:r + TRANSPOSE_ROWS] = wt_ref[r:r + TRANSPOSE_ROWS, :].T
    wbraw_ref[...] = wt_ref[GLA_COLS:GLA_COLS + LANES, :].T
    hgrn0 = GLA_COLS + GLA_GATE_RANK
    for r in range(0, HGRN_COLS, TRANSPOSE_ROWS):
        wb_ref[:, r:r + TRANSPOSE_ROWS] = wt_ref[hgrn0 + r:hgrn0 + r + TRANSPOSE_ROWS, :].T


def _projection_program(x_ref, wn_ref, wa_ref, wbraw_ref, wb_ref, wgk_ref, bgk_ref, lbl_ref,
                        gla, hgrn):
    hb = _rmsnorm(x_ref[...], wn_ref[...]).astype(BF16)

    def proj(w_ref, lo, width):
        return jnp.dot(hb, w_ref[:, lo:lo + width], preferred_element_type=F32)

    def decays_and_operands():
        lbl = lbl_ref[...]
        e = jnp.exp(lbl - jnp.max(lbl, axis=0, keepdims=True))
        lb = e[0:1, :] / jnp.sum(e, axis=0, keepdims=True)
        f = lb + (1.0 - lb) * _sigmoid(proj(wb_ref, _C_HF, HGRN_F))
        hgrn["k"][...] = 1.0 - f
        hgrn["g"][...] = jnp.log(f) * LOG2_E
        glr = proj(wbraw_ref, 0, LANES).astype(BF16)
        gla["q"][...] = proj(wa_ref, _C_GQ, GLA_QK) * (GLA_DK ** -0.5)
        gla["k"][...] = proj(wa_ref, _C_GK, GLA_QK)
        gla["v"][...] = proj(wa_ref, _C_GV, GLA_V).astype(BF16)
        gk = jnp.dot(glr, wgk_ref[...], preferred_element_type=F32) + bgk_ref[...]
        gla["g"][...] = _log_sigmoid(gk) * (LOG2_E / GLA_GATE_NORM)
        hgrn["q"][...] = _silu(proj(wb_ref, _C_HQ, HGRN_F))
        hgrn["v"][...] = proj(wb_ref, _C_HI, HGRN_V).astype(BF16)

    def output_gates():
        gla["gate"][...] = proj(wa_ref, _C_GG, GLA_V)
        hgrn["gate"][...] = proj(wb_ref, _C_HG, HGRN_V)

    return decays_and_operands, output_gates


def _mixer_program(q_ref, k_ref, g_ref, v_ref, o_ref, b_ref, st_ref, *, heads, dk, dv):
    gw = max(dk, LANES)
    hpg = gw // dk
    ngroups = heads // hpg

    @pl.when(pl.program_id(1) == 0)
    def _():
        st_ref[...] = jnp.zeros_like(st_ref)

    ri = lax.broadcasted_iota(jnp.int32, (CUMSUM_ROWS, CUMSUM_ROWS), 0)
    ci = lax.broadcasted_iota(jnp.int32, (CUMSUM_ROWS, CUMSUM_ROWS), 1)
    tri = jnp.where((ri // CHUNK == ci // CHUNK) & (ci <= ri), 1.0, 0.0).astype(BF16)

    def cumsum_block(r0):
        blk = slice(r0, r0 + CUMSUM_ROWS)
        g = g_ref[blk, :]
        g_hi = g.astype(BF16)
        g_lo = (g - g_hi.astype(F32)).astype(BF16)
        b_ref[blk, :] = (jnp.dot(tri, g_hi, preferred_element_type=F32)
                         + jnp.dot(tri, g_lo, preferred_element_type=F32))

    si = lax.broadcasted_iota(jnp.int32, (SUB, CHUNK), 0)
    sj = lax.broadcasted_iota(jnp.int32, (SUB, CHUNK), 1)
    col_block = [sj // SUB == r for r in range(NLAG)]
    diag_block = [col_block[r] & (sj - SUB * r <= si) for r in range(NLAG)]
    lane = lax.broadcasted_iota(jnp.int32, (1, gw), 1)

    head_ids = [(gi, hh) for gi in range(ngroups) for hh in range(hpg)]
    v_of = lambda h, c: v_ref[c * CHUNK:(c + 1) * CHUNK, h * dv:(h + 1) * dv]
    state = {(gi, hh): st_ref[gi * hpg + hh] for gi, hh in head_ids}
    probs, qin, start_state = {}, {}, {}

    def scores_and_state(c):
        r0 = c * CHUNK
        rws = slice(r0, r0 + CHUNK)
        for gi in range(ngroups):
            lanes = slice(gi * gw, (gi + 1) * gw)
            b = b_ref[rws, lanes]
            q = q_ref[rws, lanes]
            k = k_ref[rws, lanes]
            ends = {r: b_ref[r0 + SUB * r + SUB - 1:r0 + SUB * r + SUB, lanes] for r in range(NLAG)}
            ends[-1] = jnp.zeros((1, gw), F32)
            blk = lambda a, r: a[SUB * r:SUB * (r + 1), :]
            rows_of = lambda f: jnp.concatenate(
                [jnp.broadcast_to(f(r), (SUB, gw)) for r in range(NLAG)], axis=0)

            q0 = q * jnp.exp2(b - rows_of(lambda r: ends[r]))
            q1 = q * jnp.exp2(b - rows_of(lambda r: ends[r - 1]))
            k_hat = k * jnp.exp2(rows_of(lambda r: ends[r]) - b)
            lag_rows = [q0.astype(BF16), q1[SUB:, :].astype(BF16)]
            for l in range(2, NLAG):
                lag_rows.append(jnp.concatenate(
                    [blk(q1, r) * jnp.exp2(ends[r - 1] - ends[r - l]) for r in range(l, NLAG)],
                    axis=0).astype(BF16))
            q_lag = jnp.concatenate(lag_rows, axis=0)
            q_in = jnp.concatenate(
                [blk(q1, r) * jnp.exp2(ends[r - 1]) for r in range(NLAG)], axis=0)
            k_dec = jnp.concatenate(
                [blk(k_hat, r) * jnp.exp2(ends[NLAG - 1] - ends[r]) for r in range(NLAG)],
                axis=0).astype(BF16)
            dec = jnp.exp2(ends[NLAG - 1])

            for hh in range(hpg):
                h = gi * hpg + hh
                if hpg == 1:
                    k_h, qin[gi, hh, c] = k_hat.astype(BF16), q_in.astype(BF16)
                else:
                    in_head = (lane >= hh * dk) & (lane < (hh + 1) * dk)
                    k_h = jnp.where(in_head, k_hat, 0.0).astype(BF16)
                    qin[gi, hh, c] = jnp.where(in_head, q_in, 0.0).astype(BF16)
                s_all = lax.dot_general(q_lag, k_h, (((1,), (1,)), ((), ())),
                                        preferred_element_type=F32)
                p_rows = []
                for r in range(NLAG):
                    p_r = jnp.where(diag_block[r], blk(s_all, r), 0.0)
                    for l in range(1, r + 1):
                        src = LAG_OFFSETS[l] + SUB * (r - l)
                        p_r = jnp.where(col_block[r - l], s_all[src:src + SUB, :], p_r)
                    p_rows.append(p_r)
                probs[gi, hh, c] = jnp.concatenate(p_rows, axis=0).astype(BF16)
                upd = lax.dot_general(v_of(h, c), k_dec, (((0,), (0,)), ((), ())),
                                      preferred_element_type=F32)
                start_state[gi, hh, c] = state[gi, hh].astype(BF16)
                state[gi, hh] = state[gi, hh] * dec + upd

    def outputs(c):
        for gi, hh in head_ids:
            h = gi * hpg + hh
            o = (jnp.dot(probs.pop((gi, hh, c)), v_of(h, c), preferred_element_type=F32)
                 + lax.dot_general(qin.pop((gi, hh, c)), start_state.pop((gi, hh, c)),
                                   (((1,), (1,)), ((), ())), preferred_element_type=F32))
            o_ref[c * CHUNK:(c + 1) * CHUNK, h * dv:(h + 1) * dv] = o

    def store_state():
        for gi, hh in head_ids:
            st_ref[gi * hpg + hh] = state[gi, hh]

    return cumsum_block, scores_and_state, outputs, store_state


_MIXERS = (dict(heads=GLA_HEADS, dk=GLA_DK, dv=GLA_DV), dict(heads=HGRN_HEADS, dk=HGRN_DF, dv=HGRN_DV))


def _mixer_kernel(x_ref, wn_ref, wt_ref, wgk_ref, bgk_ref, lbl_ref, *rest, n_cast):
    cast_in, rest = rest[:n_cast], rest[n_cast:]
    (oa_ref, ob_ref, gga_ref, ggb_ref), rest = rest[:4], rest[4:]
    cast_out, rest = rest[:n_cast], rest[n_cast:]
    (wa_ref, wbraw_ref, wb_ref,
     qa_ref, ka_ref, ga_ref, va_ref, ba_ref, sta_ref,
     qb_ref, kb_ref, gb_ref, vb_ref, bb_ref, stb_ref) = rest

    for src, dst in zip(cast_in, cast_out):
        dst[...] = src[...].astype(BF16)

    @pl.when((pl.program_id(0) == 0) & (pl.program_id(1) == 0))
    def _():
        _transpose_projection_weight(wt_ref, wa_ref, wbraw_ref, wb_ref)

    decays_and_operands, output_gates = _projection_program(
        x_ref, wn_ref, wa_ref, wbraw_ref, wb_ref, wgk_ref, bgk_ref, lbl_ref,
        dict(q=qa_ref, k=ka_ref, g=ga_ref, v=va_ref, gate=gga_ref),
        dict(q=qb_ref, k=kb_ref, g=gb_ref, v=vb_ref, gate=ggb_ref))
    programs = [
        _mixer_program(qa_ref, ka_ref, ga_ref, va_ref, oa_ref, ba_ref, sta_ref, **_MIXERS[0]),
        _mixer_program(qb_ref, kb_ref, gb_ref, vb_ref, ob_ref, bb_ref, stb_ref, **_MIXERS[1]),
    ]
    nchunk = x_ref.shape[0] // CHUNK
    chunks_per_block = CUMSUM_ROWS // CHUNK

    decays_and_operands()
    for cumsum_block, _, _, _ in programs:
        cumsum_block(0)
    output_gates()
    for c in range(nchunk):
        for cumsum_block, scores_and_state, outputs, _ in programs:
            scores_and_state(c)
            if c >= 1:
                outputs(c - 1)
            if (c + 2) % chunks_per_block == 0 and (c + 2) < nchunk:
                cumsum_block((c + 2) * CHUNK)
    for _, _, outputs, store_state in programs:
        outputs(nchunk - 1)
        store_state()


def _mixer_call(x2, wn, w_t, wgk_p, bgk, lbl, cast_weights, *, batch):
    m = x2.shape[0]
    tm = MIXER_ROWS
    nt = m // batch // tm
    steps = batch * nt
    row = lambda width: pl.BlockSpec((tm, width), lambda b, i: (b * nt + i, 0))
    full = lambda a: pl.BlockSpec(a.shape, lambda b, i: (0,) * a.ndim)
    resident = lambda a: pl.BlockSpec(a.shape, lambda b, i: (0,) * a.ndim,
                                      pipeline_mode=pl.Buffered(1))
    slab = lambda a: pl.BlockSpec((a.shape[0] // steps, a.shape[1]), lambda b, i: (b * nt + i, 0))
    out_shapes = (jax.ShapeDtypeStruct((m, GLA_V), F32), jax.ShapeDtypeStruct((m, HGRN_V), F32),
                  jax.ShapeDtypeStruct((m, GLA_V), F32), jax.ShapeDtypeStruct((m, HGRN_V), F32))
    cast_shapes = tuple(jax.ShapeDtypeStruct(w.shape, BF16) for w in cast_weights)
    scratch = [pltpu.VMEM((D_MODEL, GLA_COLS), BF16), pltpu.VMEM((D_MODEL, LANES), BF16),
               pltpu.VMEM((D_MODEL, HGRN_COLS), BF16)]
    for mix in _MIXERS:
        w, gw = mix["heads"] * mix["dk"], max(mix["dk"], LANES)
        scratch += [pltpu.VMEM((tm, w), F32), pltpu.VMEM((tm, w), F32), pltpu.VMEM((tm, w), F32),
                    pltpu.VMEM((tm, mix["heads"] * mix["dv"]), BF16),
                    pltpu.VMEM((tm, w), F32), pltpu.VMEM((mix["heads"], mix["dv"], gw), F32)]
    outs = pl.pallas_call(
        functools.partial(_mixer_kernel, n_cast=len(cast_weights)),
        grid=(batch, nt),
        in_specs=[row(D_MODEL), full(wn), resident(w_t), full(wgk_p), full(bgk), full(lbl)]
                 + [slab(w) for w in cast_weights],
        out_specs=tuple(row(s.shape[1]) for s in out_shapes) + tuple(slab(w) for w in cast_weights),
        out_shape=out_shapes + cast_shapes,
        scratch_shapes=scratch,
        compiler_params=pltpu.CompilerParams(
            dimension_semantics=("arbitrary", "arbitrary"), vmem_limit_bytes=VMEM_LIMIT_BYTES),
        name="proj_gla_hgrn_mixers",
    )(x2, wn, w_t, wgk_p, bgk, lbl, *cast_weights)
    return outs[:len(out_shapes)], outs[len(out_shapes):]


def _head_norm_gate(o, gate, w, heads, dv):
    parts = []
    for h in range(heads):
        oh = o[:, h * dv:(h + 1) * dv]
        parts.append(oh * lax.rsqrt(jnp.mean(oh * oh, axis=-1, keepdims=True) + EPS))
    y = jnp.concatenate(parts, axis=-1) * w
    return (y * _silu(gate)).astype(BF16)


def _tail_kernel(x_ref, oa_ref, ob_ref, gga_ref, ggb_ref, nwa_ref, nwb_ref,
                 w_out_ref, post_mix_ref, pre_mlp_ref, post_mlp_ref, w_up_ref, w_down_ref,
                 out_ref):
    sub = x_ref.shape[0] // TAIL_SUBTILES
    tiles = [slice(s * sub, (s + 1) * sub) for s in range(TAIL_SUBTILES)]
    hs, us = {}, {}

    def out_proj(s):
        r = tiles[s]
        ya = _head_norm_gate(oa_ref[r, :], gga_ref[r, :], nwa_ref[...], GLA_HEADS, GLA_DV)
        yb = _head_norm_gate(ob_ref[r, :], ggb_ref[r, :], nwb_ref[...], HGRN_HEADS, HGRN_DV)
        mix = (jnp.dot(ya, w_out_ref[0:GLA_V, :], preferred_element_type=F32)
               + jnp.dot(yb, w_out_ref[GLA_V:GLA_V + HGRN_V, :], preferred_element_type=F32))
        hs[s] = x_ref[r, :] + _rmsnorm(mix, post_mix_ref[...])

    def up(s):
        us[s] = jnp.dot(_rmsnorm(hs[s], pre_mlp_ref[...]).astype(BF16), w_up_ref[...],
                        preferred_element_type=F32)

    def down(s):
        relu = jnp.maximum(us.pop(s), 0.0)
        m = jnp.dot((relu * relu).astype(BF16), w_down_ref[...], preferred_element_type=F32)
        out_ref[tiles[s], :] = hs.pop(s) + _rmsnorm(m, post_mlp_ref[...])

    for stage in (out_proj, up, down):
        for s in range(TAIL_SUBTILES):
            stage(s)


def _tail_call(x2, oa, ob, gga, ggb, nwa, nwb, w_out, post_mix, pre_mlp, post_mlp, w_up, w_down):
    m = x2.shape[0]
    tm = TAIL_ROWS
    row = lambda width: pl.BlockSpec((tm, width), lambda i: (i, 0))
    full = lambda a: pl.BlockSpec(a.shape, lambda i: (0,) * a.ndim)
    resident = lambda a: pl.BlockSpec(a.shape, lambda i: (0,) * a.ndim,
                                      pipeline_mode=pl.Buffered(1))
    return pl.pallas_call(
        _tail_kernel,
        grid=(m // tm,),
        in_specs=[row(D_MODEL), row(GLA_V), row(HGRN_V), row(GLA_V), row(HGRN_V),
                  full(nwa), full(nwb), resident(w_out), full(post_mix), full(pre_mlp),
                  full(post_mlp), resident(w_up), resident(w_down)],
        out_specs=row(D_MODEL),
        out_shape=jax.ShapeDtypeStruct((m, D_MODEL), F32),
        compiler_params=pltpu.CompilerParams(
            dimension_semantics=("parallel",), vmem_limit_bytes=VMEM_LIMIT_BYTES),
        name="out_proj_mlp",
    )(x2, oa, ob, gga, ggb, nwa, nwb, w_out, post_mix, pre_mlp, post_mlp, w_up, w_down)


def kernel(x, w_in, w_gk_up, b_gk, gla_norm_w, hgrn_norm_w, hgrn_lower_bounds, w_out,
           pre_mix_norm, post_mix_norm, pre_mlp_norm, post_mlp_norm, w_up, w_down):
    batch, seq, d = x.shape
    x2 = x.reshape(batch * seq, d)
    l = 0
    w_t = jnp.transpose(w_in[l]).astype(BF16)
    wgk_p = jnp.pad(w_gk_up[l], ((0, LANES - GLA_GATE_RANK), (0, 0))).astype(BF16)
    row2 = lambda a: a.reshape(1, -1)

    (oa, ob, gga, ggb), (w_out_b, w_up_b, w_down_b) = _mixer_call(
        x2, row2(pre_mix_norm[l]), w_t, wgk_p, row2(b_gk[l]), hgrn_lower_bounds,
        (w_out[l], w_up[l], w_down[l]), batch=batch)
    out = _tail_call(x2, oa, ob, gga, ggb, row2(gla_norm_w[l]), row2(hgrn_norm_w[l]),
                     w_out_b, row2(post_mix_norm[l]), row2(pre_mlp_norm[l]),
                     row2(post_mlp_norm[l]), w_up_b, w_down_b)
    return out.reshape(batch, seq, d)
```

```python
import functools

import jax
import jax.numpy as jnp
from jax import lax
from jax.experimental import pallas as pl
from jax.experimental.pallas import tpu as pltpu

F32 = jnp.float32
BF16 = jnp.bfloat16

D_MODEL = 1024
GLA_HEADS, GLA_DK, GLA_DV = 4, 64, 128
GLA_QK = GLA_HEADS * GLA_DK
GLA_V = GLA_HEADS * GLA_DV
GLA_GATE_RANK = 16
GLA_GATE_NORM = 16.0
HGRN_HEADS, HGRN_DF, HGRN_DV = 4, 128, 128
HGRN_F = HGRN_HEADS * HGRN_DF
HGRN_V = HGRN_HEADS * HGRN_DV
D_FF = 4 * D_MODEL
EPS = 1e-6
LOG2_E = 1.4426950408889634

LANES = 128
VMEM_LIMIT_BYTES = 56 * 1024 * 1024

_C_GQ = 0
_C_GK = _C_GQ + GLA_QK
_C_GV = _C_GK + GLA_QK
_C_GG = _C_GV + GLA_V
GLA_COLS = _C_GG + GLA_V
_C_HQ = 0
_C_HF = _C_HQ + HGRN_F
_C_HI = _C_HF + HGRN_F
_C_HG = _C_HI + HGRN_V
HGRN_COLS = _C_HG + HGRN_V

MIXER_ROWS = 512
TRANSPOSE_ROWS = 512
CUMSUM_ROWS = 256
CHUNK = 64
SUB = 16
NLAG = CHUNK // SUB
LAG_OFFSETS = [sum(CHUNK - SUB * m for m in range(l)) for l in range(NLAG + 1)]
LAG_ROWS = LAG_OFFSETS[NLAG]
TAIL_ROWS = 512
TAIL_SUBTILES = 2


def _rmsnorm(x, w):
    return x * lax.rsqrt(jnp.mean(x * x, axis=-1, keepdims=True) + EPS) * w


def _sigmoid(x):
    return 1.0 / (1.0 + jnp.exp(-x))


def _silu(x):
    return x * _sigmoid(x)


def _log_sigmoid(x):
    return -(jnp.maximum(-x, 0.0) + jnp.log(1.0 + jnp.exp(-jnp.abs(x))))


def _transpose_projection_weight(wt_ref, wa_ref, wbraw_ref, wb_ref):
    for r in range(0, GLA_COLS, TRANSPOSE_ROWS):
        wa_ref[:, r:r + TRANSPOSE_ROWS] = wt_ref[r:r + TRANSPOSE_ROWS, :].T
    wbraw_ref[...] = wt_ref[GLA_COLS:GLA_COLS + LANES, :].T
    hgrn0 = GLA_COLS + GLA_GATE_RANK
    for r in range(0, HGRN_COLS, TRANSPOSE_ROWS):
        wb_ref[:, r:r + TRANSPOSE_ROWS] = wt_ref[hgrn0 + r:hgrn0 + r + TRANSPOSE_ROWS, :].T


def _projection_program(x_ref, wn_ref, wa_ref, wbraw_ref, wb_ref, wgk_ref, bgk_ref, lbl_ref,
                        gla, hgrn):
    x = x_ref[...]
    xg = (x * wn_ref[...]).astype(BF16)
    rstd = lax.rsqrt(jnp.mean(x * x, axis=-1, keepdims=True) + EPS)

    def proj(w_ref, lo, width):
        return rstd * jnp.dot(xg, w_ref[:, lo:lo + width], preferred_element_type=F32)

    def decays_and_operands():
        lbl = lbl_ref[...]
        e = jnp.exp(lbl - jnp.max(lbl, axis=0, keepdims=True))
        lb = e[0:1, :] / jnp.sum(e, axis=0, keepdims=True)
        f = lb + (1.0 - lb) * _sigmoid(proj(wb_ref, _C_HF, HGRN_F))
        hgrn["k"][...] = 1.0 - f
        hgrn["g"][...] = jnp.log(f) * LOG2_E
        glr = proj(wbraw_ref, 0, LANES).astype(BF16)
        gla["q"][...] = proj(wa_ref, _C_GQ, GLA_QK) * (GLA_DK ** -0.5)
        gla["k"][...] = proj(wa_ref, _C_GK, GLA_QK)
        gla["v"][...] = proj(wa_ref, _C_GV, GLA_V).astype(BF16)
        gk = jnp.dot(glr, wgk_ref[...], preferred_element_type=F32) + bgk_ref[...]
        gla["g"][...] = _log_sigmoid(gk) * (LOG2_E / GLA_GATE_NORM)
        hgrn["q"][...] = _silu(proj(wb_ref, _C_HQ, HGRN_F))
        hgrn["v"][...] = proj(wb_ref, _C_HI, HGRN_V).astype(BF16)

    def output_gates():
        gla["gate"][...] = proj(wa_ref, _C_GG, GLA_V)
        hgrn["gate"][...] = proj(wb_ref, _C_HG, HGRN_V)

    return decays_and_operands, output_gates


def _mixer_program(q_ref, k_ref, g_ref, v_ref, o_ref, b_ref, st_ref, *, heads, dk, dv):
    gw = max(dk, LANES)
    hpg = gw // dk
    ngroups = heads // hpg

    ri = lax.broadcasted_iota(jnp.int32, (CUMSUM_ROWS, CUMSUM_ROWS), 0)
    ci = lax.broadcasted_iota(jnp.int32, (CUMSUM_ROWS, CUMSUM_ROWS), 1)
    tri = jnp.where((ri // CHUNK == ci // CHUNK) & (ci <= ri), 1.0, 0.0).astype(BF16)

    def cumsum_block(r0):
        blk = slice(r0, r0 + CUMSUM_ROWS)
        g = g_ref[blk, :]
        g_hi = g.astype(BF16)
        g_lo = (g - g_hi.astype(F32)).astype(BF16)
        b_ref[blk, :] = (jnp.dot(tri, g_hi, preferred_element_type=F32)
                         + jnp.dot(tri, g_lo, preferred_element_type=F32))

    si = lax.broadcasted_iota(jnp.int32, (SUB, CHUNK), 0)
    sj = lax.broadcasted_iota(jnp.int32, (SUB, CHUNK), 1)
    col_block = [sj // SUB == r for r in range(NLAG)]
    diag_block = [col_block[r] & (sj - SUB * r <= si) for r in range(NLAG)]
    lane = lax.broadcasted_iota(jnp.int32, (1, gw), 1)

    head_ids = [(gi, hh) for gi in range(ngroups) for hh in range(hpg)]
    v_of = lambda h, c: v_ref[c * CHUNK:(c + 1) * CHUNK, h * dv:(h + 1) * dv]
    state = {(gi, hh): st_ref[gi * hpg + hh] for gi, hh in head_ids}
    probs, qin, start_state = {}, {}, {}

    def scores_and_state(c):
        r0 = c * CHUNK
        rws = slice(r0, r0 + CHUNK)
        for gi in range(ngroups):
            lanes = slice(gi * gw, (gi + 1) * gw)
            b = b_ref[rws, lanes]
            q = q_ref[rws, lanes]
            k = k_ref[rws, lanes]
            ends = {r: b_ref[r0 + SUB * r + SUB - 1:r0 + SUB * r + SUB, lanes] for r in range(NLAG)}
            ends[-1] = jnp.zeros((1, gw), F32)
            blk = lambda a, r: a[SUB * r:SUB * (r + 1), :]
            rows_of = lambda f: jnp.concatenate(
                [jnp.broadcast_to(f(r), (SUB, gw)) for r in range(NLAG)], axis=0)

            q0 = q * jnp.exp2(b - rows_of(lambda r: ends[r]))
            q1 = q * jnp.exp2(b - rows_of(lambda r: ends[r - 1]))
            k_hat = k * jnp.exp2(rows_of(lambda r: ends[r]) - b)
            lag_rows = [q0.astype(BF16), q1[SUB:, :].astype(BF16)]
            for l in range(2, NLAG):
                lag_rows.append(jnp.concatenate(
                    [blk(q1, r) * jnp.exp2(ends[r - 1] - ends[r - l]) for r in range(l, NLAG)],
                    axis=0).astype(BF16))
            q_lag = jnp.concatenate(lag_rows, axis=0)
            q_in = jnp.concatenate(
                [blk(q1, r) * jnp.exp2(ends[r - 1]) for r in range(NLAG)], axis=0)
            k_dec = jnp.concatenate(
                [blk(k_hat, r) * jnp.exp2(ends[NLAG - 1] - ends[r]) for r in range(NLAG)],
                axis=0).astype(BF16)
            dec = jnp.exp2(ends[NLAG - 1])

            for hh in range(hpg):
                h = gi * hpg + hh
                if hpg == 1:
                    k_h, qin[gi, hh, c] = k_hat.astype(BF16), q_in.astype(BF16)
                else:
                    in_head = (lane >= hh * dk) & (lane < (hh + 1) * dk)
                    k_h = jnp.where(in_head, k_hat, 0.0).astype(BF16)
                    qin[gi, hh, c] = jnp.where(in_head, q_in, 0.0).astype(BF16)
                s_all = lax.dot_general(q_lag, k_h, (((1,), (1,)), ((), ())),
                                        preferred_element_type=F32)
                p_rows = []
                for r in range(NLAG):
                    p_r = jnp.where(diag_block[r], blk(s_all, r), 0.0)
                    for l in range(1, r + 1):
                        src = LAG_OFFSETS[l] + SUB * (r - l)
                        p_r = jnp.where(col_block[r - l], s_all[src:src + SUB, :], p_r)
                    p_rows.append(p_r)
                probs[gi, hh, c] = jnp.concatenate(p_rows, axis=0).astype(BF16)
                upd = lax.dot_general(v_of(h, c), k_dec, (((0,), (0,)), ((), ())),
                                      preferred_element_type=F32)
                start_state[gi, hh, c] = state[gi, hh].astype(BF16)
                state[gi, hh] = state[gi, hh] * dec + upd

    def outputs(c):
        for gi, hh in head_ids:
            h = gi * hpg + hh
            o = (jnp.dot(probs.pop((gi, hh, c)), v_of(h, c), preferred_element_type=F32)
                 + lax.dot_general(qin.pop((gi, hh, c)), start_state.pop((gi, hh, c)),
                                   (((1,), (1,)), ((), ())), preferred_element_type=F32))
            o_ref[c * CHUNK:(c + 1) * CHUNK, h * dv:(h + 1) * dv] = o

    def store_state():
        for gi, hh in head_ids:
            st_ref[gi * hpg + hh] = state[gi, hh]

    return cumsum_block, scores_and_state, outputs, store_state


_MIXERS = (dict(heads=GLA_HEADS, dk=GLA_DK, dv=GLA_DV), dict(heads=HGRN_HEADS, dk=HGRN_DF, dv=HGRN_DV))


def _mixer_kernel(x_ref, wn_ref, wt_ref, wgk_ref, bgk_ref, lbl_ref, *rest, n_cast):
    cast_in, rest = rest[:n_cast], rest[n_cast:]
    (oa_ref, ob_ref, gga_ref, ggb_ref), rest = rest[:4], rest[4:]
    cast_out, rest = rest[:n_cast], rest[n_cast:]
    (wa_ref, wbraw_ref, wb_ref,
     qa_ref, ka_ref, ga_ref, va_ref, ba_ref, sta_ref,
     qb_ref, kb_ref, gb_ref, vb_ref, bb_ref, stb_ref) = rest

    @pl.when((pl.program_id(0) == 0) & (pl.program_id(1) == 0))
    def _():
        _transpose_projection_weight(wt_ref, wa_ref, wbraw_ref, wb_ref)

    @pl.when(pl.program_id(1) == 0)
    def _():
        sta_ref[...] = jnp.zeros_like(sta_ref)
        stb_ref[...] = jnp.zeros_like(stb_ref)

    for src, dst in zip(cast_in, cast_out):
        dst[...] = src[...].astype(BF16)

    decays_and_operands, output_gates = _projection_program(
        x_ref, wn_ref, wa_ref, wbraw_ref, wb_ref, wgk_ref, bgk_ref, lbl_ref,
        dict(q=qa_ref, k=ka_ref, g=ga_ref, v=va_ref, gate=gga_ref),
        dict(q=qb_ref, k=kb_ref, g=gb_ref, v=vb_ref, gate=ggb_ref))
    programs = [
        _mixer_program(qa_ref, ka_ref, ga_ref, va_ref, oa_ref, ba_ref, sta_ref, **_MIXERS[0]),
        _mixer_program(qb_ref, kb_ref, gb_ref, vb_ref, ob_ref, bb_ref, stb_ref, **_MIXERS[1]),
    ]
    nchunk = x_ref.shape[0] // CHUNK
    chunks_per_block = CUMSUM_ROWS // CHUNK

    decays_and_operands()
    for cumsum_block, _, _, _ in programs:
        cumsum_block(0)
    output_gates()
    for c in range(nchunk):
        for cumsum_block, scores_and_state, outputs, _ in programs:
            scores_and_state(c)
            if c >= 1:
                outputs(c - 1)
            if (c + 2) % chunks_per_block == 0 and (c + 2) < nchunk:
                cumsum_block((c + 2) * CHUNK)
    for _, _, outputs, store_state in programs:
        outputs(nchunk - 1)
        store_state()


def _mixer_call(x2, wn, w_t, wgk_p, bgk, lbl, cast_weights, *, batch):
    m = x2.shape[0]
    tm = MIXER_ROWS
    nt = m // batch // tm
    steps = batch * nt
    row = lambda width: pl.BlockSpec((tm, width), lambda b, i: (b * nt + i, 0))
    full = lambda a: pl.BlockSpec(a.shape, lambda b, i: (0,) * a.ndim)
    resident = lambda a: pl.BlockSpec(a.shape, lambda b, i: (0,) * a.ndim,
                                      pipeline_mode=pl.Buffered(1))
    slab = lambda a: pl.BlockSpec((a.shape[0] // steps, a.shape[1]), lambda b, i: (b * nt + i, 0))
    out_shapes = (jax.ShapeDtypeStruct((m, GLA_V), F32), jax.ShapeDtypeStruct((m, HGRN_V), F32),
                  jax.ShapeDtypeStruct((m, GLA_V), F32), jax.ShapeDtypeStruct((m, HGRN_V), F32))
    cast_shapes = tuple(jax.ShapeDtypeStruct(w.shape, BF16) for w in cast_weights)
    scratch = [pltpu.VMEM((D_MODEL, GLA_COLS), BF16), pltpu.VMEM((D_MODEL, LANES), BF16),
               pltpu.VMEM((D_MODEL, HGRN_COLS), BF16)]
    for mix in _MIXERS:
        w, gw = mix["heads"] * mix["dk"], max(mix["dk"], LANES)
        scratch += [pltpu.VMEM((tm, w), F32), pltpu.VMEM((tm, w), F32), pltpu.VMEM((tm, w), F32),
                    pltpu.VMEM((tm, mix["heads"] * mix["dv"]), BF16),
                    pltpu.VMEM((tm, w), F32), pltpu.VMEM((mix["heads"], mix["dv"], gw), F32)]
    outs = pl.pallas_call(
        functools.partial(_mixer_kernel, n_cast=len(cast_weights)),
        grid=(batch, nt),
        in_specs=[row(D_MODEL), full(wn), resident(w_t), full(wgk_p), full(bgk), full(lbl)]
                 + [slab(w) for w in cast_weights],
        out_specs=tuple(row(s.shape[1]) for s in out_shapes) + tuple(slab(w) for w in cast_weights),
        out_shape=out_shapes + cast_shapes,
        scratch_shapes=scratch,
        compiler_params=pltpu.CompilerParams(
            dimension_semantics=("arbitrary", "arbitrary"), vmem_limit_bytes=VMEM_LIMIT_BYTES),
        name="proj_gla_hgrn_mixers",
    )(x2, wn, w_t, wgk_p, bgk, lbl, *cast_weights)
    return outs[:len(out_shapes)], outs[len(out_shapes):]


def _head_norm_gate(o, gate, w, heads, dv):
    parts = []
    for h in range(heads):
        oh = o[:, h * dv:(h + 1) * dv]
        parts.append(oh * lax.rsqrt(jnp.mean(oh * oh, axis=-1, keepdims=True) + EPS))
    y = jnp.concatenate(parts, axis=-1) * w
    return (y * _silu(gate)).astype(BF16)


def _tail_kernel(x_ref, oa_ref, ob_ref, gga_ref, ggb_ref, nwa_ref, nwb_ref,
                 w_out_ref, post_mix_ref, pre_mlp_ref, post_mlp_ref, w_up_ref, w_down_ref,
                 out_ref):
    sub = x_ref.shape[0] // TAIL_SUBTILES
    tiles = [slice(s * sub, (s + 1) * sub) for s in range(TAIL_SUBTILES)]
    hs, us = {}, {}

    def out_proj(s):
        r = tiles[s]
        ya = _head_norm_gate(oa_ref[r, :], gga_ref[r, :], nwa_ref[...], GLA_HEADS, GLA_DV)
        yb = _head_norm_gate(ob_ref[r, :], ggb_ref[r, :], nwb_ref[...], HGRN_HEADS, HGRN_DV)
        mix = (jnp.dot(ya, w_out_ref[0:GLA_V, :], preferred_element_type=F32)
               + jnp.dot(yb, w_out_ref[GLA_V:GLA_V + HGRN_V, :], preferred_element_type=F32))
        hs[s] = x_ref[r, :] + _rmsnorm(mix, post_mix_ref[...])

    def up(s):
        us[s] = jnp.dot(_rmsnorm(hs[s], pre_mlp_ref[...]).astype(BF16), w_up_ref[...],
                        preferred_element_type=F32)

    def down(s):
        relu = jnp.maximum(us.pop(s), 0.0)
        m = jnp.dot((relu * relu).astype(BF16), w_down_ref[...], preferred_element_type=F32)
        out_ref[tiles[s], :] = hs.pop(s) + _rmsnorm(m, post_mlp_ref[...])

    for stage in (out_proj, up, down):
        for s in range(TAIL_SUBTILES):
            stage(s)


def _tail_call(x2, oa, ob, gga, ggb, nwa, nwb, w_out, post_mix, pre_mlp, post_mlp, w_up, w_down):
    m = x2.shape[0]
    tm = TAIL_ROWS
    row = lambda width: pl.BlockSpec((tm, width), lambda i: (i, 0))
    full = lambda a: pl.BlockSpec(a.shape, lambda i: (0,) * a.ndim)
    resident = lambda a: pl.BlockSpec(a.shape, lambda i: (0,) * a.ndim,
                                      pipeline_mode=pl.Buffered(1))
    return pl.pallas_call(
        _tail_kernel,
        grid=(m // tm,),
        in_specs=[row(D_MODEL), row(GLA_V), row(HGRN_V), row(GLA_V), row(HGRN_V),
                  full(nwa), full(nwb), resident(w_out), full(post_mix), full(pre_mlp),
                  full(post_mlp), resident(w_up), resident(w_down)],
        out_specs=row(D_MODEL),
        out_shape=jax.ShapeDtypeStruct((m, D_MODEL), F32),
        compiler_params=pltpu.CompilerParams(
            dimension_semantics=("parallel",), vmem_limit_bytes=VMEM_LIMIT_BYTES),
        name="out_proj_mlp",
    )(x2, oa, ob, gga, ggb, nwa, nwb, w_out, post_mix, pre_mlp, post_mlp, w_up, w_down)


def kernel(x, w_in, w_gk_up, b_gk, gla_norm_w, hgrn_norm_w, hgrn_lower_bounds, w_out,
           pre_mix_norm, post_mix_norm, pre_mlp_norm, post_mlp_norm, w_up, w_down):
    batch, seq, d = x.shape
    x2 = x.reshape(batch * seq, d)
    l = 0
    w_t = jnp.transpose(w_in[l]).astype(BF16)
    wgk_p = jnp.pad(w_gk_up[l], ((0, LANES - GLA_GATE_RANK), (0, 0))).astype(BF16)
    row2 = lambda a: a.reshape(1, -1)

    (oa, ob, gga, ggb), (w_out_b, w_up_b, w_down_b) = _mixer_call(
        x2, row2(pre_mix_norm[l]), w_t, wgk_p, row2(b_gk[l]), hgrn_lower_bounds,
        (w_out[l], w_up[l], w_down[l]), batch=batch)
    out = _tail_call(x2, oa, ob, gga, ggb, row2(gla_norm_w[l]), row2(hgrn_norm_w[l]),
                     w_out_b, row2(post_mix_norm[l]), row2(pre_mlp_norm[l]),
                     row2(post_mlp_norm[l]), w_up_b, w_down_b)
    return out.reshape(batch, seq, d)
```

```python
import functools

import jax
import jax.numpy as jnp
from jax import lax
from jax.experimental import pallas as pl
from jax.experimental.pallas import tpu as pltpu

F32 = jnp.float32
BF16 = jnp.bfloat16

D_MODEL = 1024
GLA_HEADS, GLA_DK, GLA_DV = 4, 64, 128
GLA_QK = GLA_HEADS * GLA_DK
GLA_V = GLA_HEADS * GLA_DV
GLA_GATE_RANK = 16
GLA_GATE_NORM = 16.0
HGRN_HEADS, HGRN_DF, HGRN_DV = 4, 128, 128
HGRN_F = HGRN_HEADS * HGRN_DF
HGRN_V = HGRN_HEADS * HGRN_DV
D_FF = 4 * D_MODEL
EPS = 1e-6
LOG2_E = 1.4426950408889634

LANES = 128
VMEM_LIMIT_BYTES = 56 * 1024 * 1024

_C_GQ = 0
_C_GK = _C_GQ + GLA_QK
_C_GV = _C_GK + GLA_QK
_C_GG = _C_GV + GLA_V
GLA_COLS = _C_GG + GLA_V
_C_HQ = 0
_C_HF = _C_HQ + HGRN_F
_C_HI = _C_HF + HGRN_F
_C_HG = _C_HI + HGRN_V
HGRN_COLS = _C_HG + HGRN_V

MIXER_ROWS = 512
TRANSPOSE_ROWS = 512
CUMSUM_ROWS = 256
CHUNK = 64
SUB = 16
NLAG = CHUNK // SUB
LAG_OFFSETS = [sum(CHUNK - SUB * m for m in range(l)) for l in range(NLAG + 1)]
LAG_ROWS = LAG_OFFSETS[NLAG]
TAIL_ROWS = 1024
TAIL_SUBTILES = 2


def _rmsnorm(x, w):
    return x * lax.rsqrt(jnp.mean(x * x, axis=-1, keepdims=True) + EPS) * w


def _sigmoid(x):
    return 1.0 / (1.0 + jnp.exp(-x))


def _silu(x):
    return x * _sigmoid(x)


def _log_sigmoid(x):
    return -(jnp.maximum(-x, 0.0) + jnp.log(1.0 + jnp.exp(-jnp.abs(x))))


def _transpose_projection_weight(wt_ref, wa_ref, wbraw_ref, wb_ref):
    for r in range(0, GLA_COLS, TRANSPOSE_ROWS):
        wa_ref[:, r:r + TRANSPOSE_ROWS] = wt_ref[r:r + TRANSPOSE_ROWS, :].T
    wbraw_ref[...] = wt_ref[GLA_COLS:GLA_COLS + LANES, :].T
    hgrn0 = GLA_COLS + GLA_GATE_RANK
    for r in range(0, HGRN_COLS, TRANSPOSE_ROWS):
        wb_ref[:, r:r + TRANSPOSE_ROWS] = wt_ref[hgrn0 + r:hgrn0 + r + TRANSPOSE_ROWS, :].T


def _projection_program(x_ref, wn_ref, wa_ref, wbraw_ref, wb_ref, wgk_ref, bgk_ref, lbl_ref,
                        gla, hgrn):
    x = x_ref[...]
    xg = (x * wn_ref[...]).astype(BF16)
    rstd = lax.rsqrt(jnp.mean(x * x, axis=-1, keepdims=True) + EPS)

    def proj(w_ref, lo, width):
        return rstd * jnp.dot(xg, w_ref[:, lo:lo + width], preferred_element_type=F32)

    def decays_and_operands():
        lbl = lbl_ref[...]
        e = jnp.exp(lbl - jnp.max(lbl, axis=0, keepdims=True))
        lb = e[0:1, :] / jnp.sum(e, axis=0, keepdims=True)
        f = lb + (1.0 - lb) * _sigmoid(proj(wb_ref, _C_HF, HGRN_F))
        hgrn["k"][...] = 1.0 - f
        hgrn["g"][...] = jnp.log(f) * LOG2_E
        glr = proj(wbraw_ref, 0, LANES).astype(BF16)
        gla["q"][...] = proj(wa_ref, _C_GQ, GLA_QK) * (GLA_DK ** -0.5)
        gla["k"][...] = proj(wa_ref, _C_GK, GLA_QK)
        gla["v"][...] = proj(wa_ref, _C_GV, GLA_V).astype(BF16)
        gk = jnp.dot(glr, wgk_ref[...], preferred_element_type=F32) + bgk_ref[...]
        gla["g"][...] = _log_sigmoid(gk) * (LOG2_E / GLA_GATE_NORM)
        hgrn["q"][...] = _silu(proj(wb_ref, _C_HQ, HGRN_F))
        hgrn["v"][...] = proj(wb_ref, _C_HI, HGRN_V).astype(BF16)

    def output_gates():
        gla["gate"][...] = proj(wa_ref, _C_GG, GLA_V)
        hgrn["gate"][...] = proj(wb_ref, _C_HG, HGRN_V)

    return decays_and_operands, output_gates


def _mixer_program(q_ref, k_ref, g_ref, v_ref, gate_ref, nw_ref, y_ref, y_col0, b_ref, st_ref, *,
                   heads, dk, dv):
    gw = max(dk, LANES)
    hpg = gw // dk
    ngroups = heads // hpg

    ri = lax.broadcasted_iota(jnp.int32, (CUMSUM_ROWS, CUMSUM_ROWS), 0)
    ci = lax.broadcasted_iota(jnp.int32, (CUMSUM_ROWS, CUMSUM_ROWS), 1)
    tri = jnp.where((ri // CHUNK == ci // CHUNK) & (ci <= ri), 1.0, 0.0).astype(BF16)

    def cumsum_block(r0):
        blk = slice(r0, r0 + CUMSUM_ROWS)
        g = g_ref[blk, :]
        g_hi = g.astype(BF16)
        g_lo = (g - g_hi.astype(F32)).astype(BF16)
        b_ref[blk, :] = (jnp.dot(tri, g_hi, preferred_element_type=F32)
                         + jnp.dot(tri, g_lo, preferred_element_type=F32))

    si = lax.broadcasted_iota(jnp.int32, (SUB, CHUNK), 0)
    sj = lax.broadcasted_iota(jnp.int32, (SUB, CHUNK), 1)
    col_block = [sj // SUB == r for r in range(NLAG)]
    diag_block = [col_block[r] & (sj - SUB * r <= si) for r in range(NLAG)]
    lane = lax.broadcasted_iota(jnp.int32, (1, gw), 1)

    head_ids = [(gi, hh) for gi in range(ngroups) for hh in range(hpg)]
    v_of = lambda h, c: v_ref[c * CHUNK:(c + 1) * CHUNK, h * dv:(h + 1) * dv]
    state = {(gi, hh): st_ref[gi * hpg + hh] for gi, hh in head_ids}
    probs, qin, start_state = {}, {}, {}

    def scores_and_state(c):
        r0 = c * CHUNK
        rws = slice(r0, r0 + CHUNK)
        for gi in range(ngroups):
            lanes = slice(gi * gw, (gi + 1) * gw)
            b = b_ref[rws, lanes]
            q = q_ref[rws, lanes]
            k = k_ref[rws, lanes]
            ends = {r: b_ref[r0 + SUB * r + SUB - 1:r0 + SUB * r + SUB, lanes] for r in range(NLAG)}
            ends[-1] = jnp.zeros((1, gw), F32)
            blk = lambda a, r: a[SUB * r:SUB * (r + 1), :]
            rows_of = lambda f: jnp.concatenate(
                [jnp.broadcast_to(f(r), (SUB, gw)) for r in range(NLAG)], axis=0)

            q0 = q * jnp.exp2(b - rows_of(lambda r: ends[r]))
            q1 = q * jnp.exp2(b - rows_of(lambda r: ends[r - 1]))
            k_hat = k * jnp.exp2(rows_of(lambda r: ends[r]) - b)
            lag_rows = [q0.astype(BF16), q1[SUB:, :].astype(BF16)]
            for l in range(2, NLAG):
                lag_rows.append(jnp.concatenate(
                    [blk(q1, r) * jnp.exp2(ends[r - 1] - ends[r - l]) for r in range(l, NLAG)],
                    axis=0).astype(BF16))
            q_lag = jnp.concatenate(lag_rows, axis=0)
            q_in = jnp.concatenate(
                [blk(q1, r) * jnp.exp2(ends[r - 1]) for r in range(NLAG)], axis=0)
            k_dec = jnp.concatenate(
                [blk(k_hat, r) * jnp.exp2(ends[NLAG - 1] - ends[r]) for r in range(NLAG)],
                axis=0).astype(BF16)
            dec = jnp.exp2(ends[NLAG - 1])

            for hh in range(hpg):
                h = gi * hpg + hh
                if hpg == 1:
                    k_h, qin[gi, hh, c] = k_hat.astype(BF16), q_in.astype(BF16)
                else:
                    in_head = (lane >= hh * dk) & (lane < (hh + 1) * dk)
                    k_h = jnp.where(in_head, k_hat, 0.0).astype(BF16)
                    qin[gi, hh, c] = jnp.where(in_head, q_in, 0.0).astype(BF16)
                s_all = lax.dot_general(q_lag, k_h, (((1,), (1,)), ((), ())),
                                        preferred_element_type=F32)
                p_rows = []
                for r in range(NLAG):
                    p_r = jnp.where(diag_block[r], blk(s_all, r), 0.0)
                    for l in range(1, r + 1):
                        src = LAG_OFFSETS[l] + SUB * (r - l)
                        p_r = jnp.where(col_block[r - l], s_all[src:src + SUB, :], p_r)
                    p_rows.append(p_r)
                probs[gi, hh, c] = jnp.concatenate(p_rows, axis=0).astype(BF16)
                upd = lax.dot_general(v_of(h, c), k_dec, (((0,), (0,)), ((), ())),
                                      preferred_element_type=F32)
                start_state[gi, hh, c] = state[gi, hh].astype(BF16)
                state[gi, hh] = state[gi, hh] * dec + upd

    def outputs(c):
        for gi, hh in head_ids:
            h = gi * hpg + hh
            o = (jnp.dot(probs.pop((gi, hh, c)), v_of(h, c), preferred_element_type=F32)
                 + lax.dot_general(qin.pop((gi, hh, c)), start_state.pop((gi, hh, c)),
                                   (((1,), (1,)), ((), ())), preferred_element_type=F32))
            rws, cols = slice(c * CHUNK, (c + 1) * CHUNK), slice(h * dv, (h + 1) * dv)
            y = o * lax.rsqrt(jnp.mean(o * o, axis=-1, keepdims=True) + EPS) * nw_ref[:, cols]
            y = y * _silu(gate_ref[rws, cols])
            y_ref[rws, y_col0 + h * dv:y_col0 + (h + 1) * dv] = y.astype(BF16)

    def store_state():
        for gi, hh in head_ids:
            st_ref[gi * hpg + hh] = state[gi, hh]

    return cumsum_block, scores_and_state, outputs, store_state


_MIXERS = (dict(heads=GLA_HEADS, dk=GLA_DK, dv=GLA_DV), dict(heads=HGRN_HEADS, dk=HGRN_DF, dv=HGRN_DV))


def _mixer_kernel(x_ref, wn_ref, wt_ref, wgk_ref, bgk_ref, lbl_ref, nwa_ref, nwb_ref, *rest,
                  n_cast):
    cast_in, rest = rest[:n_cast], rest[n_cast:]
    y_ref, rest = rest[0], rest[1:]
    cast_out, rest = rest[:n_cast], rest[n_cast:]
    (wa_ref, wbraw_ref, wb_ref,
     qa_ref, ka_ref, ga_ref, va_ref, gga_ref, ba_ref, sta_ref,
     qb_ref, kb_ref, gb_ref, vb_ref, ggb_ref, bb_ref, stb_ref) = rest

    @pl.when((pl.program_id(0) == 0) & (pl.program_id(1) == 0))
    def _():
        _transpose_projection_weight(wt_ref, wa_ref, wbraw_ref, wb_ref)

    @pl.when(pl.program_id(1) == 0)
    def _():
        sta_ref[...] = jnp.zeros_like(sta_ref)
        stb_ref[...] = jnp.zeros_like(stb_ref)

    for src, dst in zip(cast_in, cast_out):
        dst[...] = src[...].astype(BF16)

    decays_and_operands, output_gates = _projection_program(
        x_ref, wn_ref, wa_ref, wbraw_ref, wb_ref, wgk_ref, bgk_ref, lbl_ref,
        dict(q=qa_ref, k=ka_ref, g=ga_ref, v=va_ref, gate=gga_ref),
        dict(q=qb_ref, k=kb_ref, g=gb_ref, v=vb_ref, gate=ggb_ref))
    programs = [
        _mixer_program(qa_ref, ka_ref, ga_ref, va_ref, gga_ref, nwa_ref, y_ref, 0,
                       ba_ref, sta_ref, **_MIXERS[0]),
        _mixer_program(qb_ref, kb_ref, gb_ref, vb_ref, ggb_ref, nwb_ref, y_ref, GLA_V,
                       bb_ref, stb_ref, **_MIXERS[1]),
    ]
    nchunk = x_ref.shape[0] // CHUNK
    chunks_per_block = CUMSUM_ROWS // CHUNK

    decays_and_operands()
    for cumsum_block, _, _, _ in programs:
        cumsum_block(0)
    output_gates()
    for c in range(nchunk):
        for cumsum_block, scores_and_state, outputs, _ in programs:
            scores_and_state(c)
            if c >= 1:
                outputs(c - 1)
            if (c + 2) % chunks_per_block == 0 and (c + 2) < nchunk:
                cumsum_block((c + 2) * CHUNK)
    for _, _, outputs, store_state in programs:
        outputs(nchunk - 1)
        store_state()


def _mixer_call(x2, wn, w_t, wgk_p, bgk, lbl, nwa, nwb, cast_weights, *, batch):
    m = x2.shape[0]
    tm = MIXER_ROWS
    nt = m // batch // tm
    steps = batch * nt
    row = lambda width: pl.BlockSpec((tm, width), lambda b, i: (b * nt + i, 0))
    full = lambda a: pl.BlockSpec(a.shape, lambda b, i: (0,) * a.ndim)
    resident = lambda a: pl.BlockSpec(a.shape, lambda b, i: (0,) * a.ndim,
                                      pipeline_mode=pl.Buffered(1))
    slab = lambda a: pl.BlockSpec((a.shape[0] // steps, a.shape[1]), lambda b, i: (b * nt + i, 0))
    out_shapes = (jax.ShapeDtypeStruct((m, GLA_V + HGRN_V), BF16),)
    cast_shapes = tuple(jax.ShapeDtypeStruct(w.shape, BF16) for w in cast_weights)
    scratch = [pltpu.VMEM((D_MODEL, GLA_COLS), BF16), pltpu.VMEM((D_MODEL, LANES), BF16),
               pltpu.VMEM((D_MODEL, HGRN_COLS), BF16)]
    for mix in _MIXERS:
        w, gw = mix["heads"] * mix["dk"], max(mix["dk"], LANES)
        wv = mix["heads"] * mix["dv"]
        scratch += [pltpu.VMEM((tm, w), F32), pltpu.VMEM((tm, w), F32), pltpu.VMEM((tm, w), F32),
                    pltpu.VMEM((tm, wv), BF16), pltpu.VMEM((tm, wv), F32),
                    pltpu.VMEM((tm, w), F32), pltpu.VMEM((mix["heads"], mix["dv"], gw), F32)]
    outs = pl.pallas_call(
        functools.partial(_mixer_kernel, n_cast=len(cast_weights)),
        grid=(batch, nt),
        in_specs=[row(D_MODEL), full(wn), resident(w_t), full(wgk_p), full(bgk), full(lbl),
                  full(nwa), full(nwb)] + [slab(w) for w in cast_weights],
        out_specs=tuple(row(s.shape[1]) for s in out_shapes) + tuple(slab(w) for w in cast_weights),
        out_shape=out_shapes + cast_shapes,
        scratch_shapes=scratch,
        compiler_params=pltpu.CompilerParams(
            dimension_semantics=("arbitrary", "arbitrary"), vmem_limit_bytes=VMEM_LIMIT_BYTES),
        name="proj_gla_hgrn_mixers",
    )(x2, wn, w_t, wgk_p, bgk, lbl, nwa, nwb, *cast_weights)
    return outs[0], outs[1:]


def _tail_kernel(x_ref, y_ref, w_out_ref, post_mix_ref, pre_mlp_ref, post_mlp_ref,
                 w_up_ref, w_down_ref, out_ref):
    sub = x_ref.shape[0] // TAIL_SUBTILES
    tiles = [slice(s * sub, (s + 1) * sub) for s in range(TAIL_SUBTILES)]
    hs, us = {}, {}

    def out_proj(s):
        r = tiles[s]
        mix = jnp.dot(y_ref[r, :], w_out_ref[...], preferred_element_type=F32)
        hs[s] = x_ref[r, :] + _rmsnorm(mix, post_mix_ref[...])

    def up(s):
        us[s] = jnp.dot(_rmsnorm(hs[s], pre_mlp_ref[...]).astype(BF16), w_up_ref[...],
                        preferred_element_type=F32)

    def down(s):
        relu = jnp.maximum(us.pop(s), 0.0)
        m = jnp.dot((relu * relu).astype(BF16), w_down_ref[...], preferred_element_type=F32)
        out_ref[tiles[s], :] = hs.pop(s) + _rmsnorm(m, post_mlp_ref[...])

    for stage in (out_proj, up, down):
        for s in range(TAIL_SUBTILES):
            stage(s)


def _tail_call(x2, y, w_out, post_mix, pre_mlp, post_mlp, w_up, w_down):
    m = x2.shape[0]
    tm = TAIL_ROWS
    row = lambda width: pl.BlockSpec((tm, width), lambda i: (i, 0))
    full = lambda a: pl.BlockSpec(a.shape, lambda i: (0,) * a.ndim)
    resident = lambda a: pl.BlockSpec(a.shape, lambda i: (0,) * a.ndim,
                                      pipeline_mode=pl.Buffered(1))
    return pl.pallas_call(
        _tail_kernel,
        grid=(m // tm,),
        in_specs=[row(D_MODEL), row(GLA_V + HGRN_V), resident(w_out), full(post_mix), full(pre_mlp),
                  full(post_mlp), resident(w_up), resident(w_down)],
        out_specs=row(D_MODEL),
        out_shape=jax.ShapeDtypeStruct((m, D_MODEL), F32),
        compiler_params=pltpu.CompilerParams(
            dimension_semantics=("parallel",), vmem_limit_bytes=VMEM_LIMIT_BYTES),
        name="out_proj_mlp",
    )(x2, y, w_out, post_mix, pre_mlp, post_mlp, w_up, w_down)


def kernel(x, w_in, w_gk_up, b_gk, gla_norm_w, hgrn_norm_w, hgrn_lower_bounds, w_out,
           pre_mix_norm, post_mix_norm, pre_mlp_norm, post_mlp_norm, w_up, w_down):
    batch, seq, d = x.shape
    x2 = x.reshape(batch * seq, d)
    l = 0
    w_t = jnp.transpose(w_in[l]).astype(BF16)
    wgk_p = jnp.pad(w_gk_up[l], ((0, LANES - GLA_GATE_RANK), (0, 0))).astype(BF16)
    row2 = lambda a: a.reshape(1, -1)

    y, (w_out_b, w_up_b, w_down_b) = _mixer_call(
        x2, row2(pre_mix_norm[l]), w_t, wgk_p, row2(b_gk[l]), hgrn_lower_bounds,
        row2(gla_norm_w[l]), row2(hgrn_norm_w[l]), (w_out[l], w_up[l], w_down[l]), batch=batch)
    out = _tail_call(x2, y, w_out_b, row2(post_mix_norm[l]), row2(pre_mlp_norm[l]),
                     row2(post_mlp_norm[l]), w_up_b, w_down_b)
    return out.reshape(batch, seq, d)
```

```python
import functools

import jax
import jax.numpy as jnp
from jax import lax
from jax.experimental import pallas as pl
from jax.experimental.pallas import tpu as pltpu

F32 = jnp.float32
BF16 = jnp.bfloat16

D_MODEL = 1024
GLA_HEADS, GLA_DK, GLA_DV = 4, 64, 128
GLA_QK = GLA_HEADS * GLA_DK
GLA_V = GLA_HEADS * GLA_DV
GLA_GATE_RANK = 16
GLA_GATE_NORM = 16.0
HGRN_HEADS, HGRN_DF, HGRN_DV = 4, 128, 128
HGRN_F = HGRN_HEADS * HGRN_DF
HGRN_V = HGRN_HEADS * HGRN_DV
D_FF = 4 * D_MODEL
EPS = 1e-6
LOG2_E = 1.4426950408889634

LANES = 128
VMEM_LIMIT_BYTES = 56 * 1024 * 1024

_C_GQ = 0
_C_GK = _C_GQ + GLA_QK
_C_GV = _C_GK + GLA_QK
_C_GG = _C_GV + GLA_V
GLA_COLS = _C_GG + GLA_V
_C_HQ = 0
_C_HF = _C_HQ + HGRN_F
_C_HI = _C_HF + HGRN_F
_C_HG = _C_HI + HGRN_V
HGRN_COLS = _C_HG + HGRN_V

MIXER_ROWS = 512
TRANSPOSE_ROWS = 512
CUMSUM_ROWS = 256
CHUNK = 64
SUB = 16
NLAG = CHUNK // SUB
LAG_OFFSETS = [sum(CHUNK - SUB * m for m in range(l)) for l in range(NLAG + 1)]
LAG_ROWS = LAG_OFFSETS[NLAG]
TAIL_ROWS = 1024
TAIL_SUBTILES = 2


def _rmsnorm(x, w):
    return x * lax.rsqrt(jnp.mean(x * x, axis=-1, keepdims=True) + EPS) * w


def _sigmoid(x):
    return 1.0 / (1.0 + jnp.exp(-x))


def _silu(x):
    return x * _sigmoid(x)


def _log_sigmoid(x):
    return -(jnp.maximum(-x, 0.0) + jnp.log(1.0 + jnp.exp(-jnp.abs(x))))


def _transpose_projection_weight(wt_ref, wa_ref, wbraw_ref, wb_ref):
    for r in range(0, GLA_COLS, TRANSPOSE_ROWS):
        wa_ref[:, r:r + TRANSPOSE_ROWS] = wt_ref[r:r + TRANSPOSE_ROWS, :].T
    wbraw_ref[...] = wt_ref[GLA_COLS:GLA_COLS + LANES, :].T
    hgrn0 = GLA_COLS + GLA_GATE_RANK
    for r in range(0, HGRN_COLS, TRANSPOSE_ROWS):
        wb_ref[:, r:r + TRANSPOSE_ROWS] = wt_ref[hgrn0 + r:hgrn0 + r + TRANSPOSE_ROWS, :].T


def _projection_program(x_ref, wn_ref, wa_ref, wbraw_ref, wb_ref, wgk_ref, bgk_ref, lbl_ref,
                        gla, hgrn):
    x = x_ref[...]
    xg = (x * wn_ref[...]).astype(BF16)
    rstd = lax.rsqrt(jnp.mean(x * x, axis=-1, keepdims=True) + EPS)

    def proj(w_ref, lo, width):
        return rstd * jnp.dot(xg, w_ref[:, lo:lo + width], preferred_element_type=F32)

    def decays_and_operands():
        lbl = lbl_ref[...]
        e = jnp.exp(lbl - jnp.max(lbl, axis=0, keepdims=True))
        lb = e[0:1, :] / jnp.sum(e, axis=0, keepdims=True)
        f = lb + (1.0 - lb) * _sigmoid(proj(wb_ref, _C_HF, HGRN_F))
        hgrn["k"][...] = 1.0 - f
        hgrn["g"][...] = jnp.log(f) * LOG2_E
        glr = proj(wbraw_ref, 0, LANES).astype(BF16)
        gla["q"][...] = proj(wa_ref, _C_GQ, GLA_QK) * (GLA_DK ** -0.5)
        gla["k"][...] = proj(wa_ref, _C_GK, GLA_QK)
        gla["v"][...] = proj(wa_ref, _C_GV, GLA_V).astype(BF16)
        gk = jnp.dot(glr, wgk_ref[...], preferred_element_type=F32) + bgk_ref[...]
        gla["g"][...] = _log_sigmoid(gk) * (LOG2_E / GLA_GATE_NORM)
        hgrn["q"][...] = _silu(proj(wb_ref, _C_HQ, HGRN_F))
        hgrn["v"][...] = proj(wb_ref, _C_HI, HGRN_V).astype(BF16)

    def output_gates():
        gla["gate"][...] = proj(wa_ref, _C_GG, GLA_V)
        hgrn["gate"][...] = proj(wb_ref, _C_HG, HGRN_V)

    return decays_and_operands, output_gates


def _mixer_program(q_ref, k_ref, g_ref, v_ref, gate_ref, nw_ref, y_ref, y_col0, b_ref, st_ref, *,
                   heads, dk, dv):
    gw = max(dk, LANES)
    hpg = gw // dk
    ngroups = heads // hpg

    ri = lax.broadcasted_iota(jnp.int32, (CUMSUM_ROWS, CUMSUM_ROWS), 0)
    ci = lax.broadcasted_iota(jnp.int32, (CUMSUM_ROWS, CUMSUM_ROWS), 1)
    tri = jnp.where((ri // CHUNK == ci // CHUNK) & (ci <= ri), 1.0, 0.0).astype(BF16)

    def cumsum_block(r0):
        blk = slice(r0, r0 + CUMSUM_ROWS)
        g = g_ref[blk, :]
        g_hi = g.astype(BF16)
        g_lo = (g - g_hi.astype(F32)).astype(BF16)
        b_ref[blk, :] = (jnp.dot(tri, g_hi, preferred_element_type=F32)
                         + jnp.dot(tri, g_lo, preferred_element_type=F32))

    si = lax.broadcasted_iota(jnp.int32, (SUB, CHUNK), 0)
    sj = lax.broadcasted_iota(jnp.int32, (SUB, CHUNK), 1)
    col_block = [sj // SUB == r for r in range(NLAG)]
    diag_block = [col_block[r] & (sj - SUB * r <= si) for r in range(NLAG)]
    lane = lax.broadcasted_iota(jnp.int32, (1, gw), 1)

    head_ids = [(gi, hh) for gi in range(ngroups) for hh in range(hpg)]
    v_of = lambda h, c: v_ref[c * CHUNK:(c + 1) * CHUNK, h * dv:(h + 1) * dv]
    state = {(gi, hh): st_ref[gi * hpg + hh] for gi, hh in head_ids}
    probs, qin, start_state = {}, {}, {}

    def scores_and_state(c):
        r0 = c * CHUNK
        rws = slice(r0, r0 + CHUNK)
        for gi in range(ngroups):
            lanes = slice(gi * gw, (gi + 1) * gw)
            b = b_ref[rws, lanes]
            q = q_ref[rws, lanes]
            k = k_ref[rws, lanes]
            ends = {r: b_ref[r0 + SUB * r + SUB - 1:r0 + SUB * r + SUB, lanes] for r in range(NLAG)}
            ends[-1] = jnp.zeros((1, gw), F32)
            blk = lambda a, r: a[SUB * r:SUB * (r + 1), :]
            rows_of = lambda f: jnp.concatenate(
                [jnp.broadcast_to(f(r), (SUB, gw)) for r in range(NLAG)], axis=0)

            q0 = q * jnp.exp2(b - rows_of(lambda r: ends[r]))
            q1 = q * jnp.exp2(b - rows_of(lambda r: ends[r - 1]))
            k_hat = k * jnp.exp2(rows_of(lambda r: ends[r]) - b)
            lag_rows = [q0.astype(BF16), q1[SUB:, :].astype(BF16)]
            for l in range(2, NLAG):
                lag_rows.append(jnp.concatenate(
                    [blk(q1, r) * jnp.exp2(ends[r - 1] - ends[r - l]) for r in range(l, NLAG)],
                    axis=0).astype(BF16))
            q_lag = jnp.concatenate(lag_rows, axis=0)
            q_in = jnp.concatenate(
                [blk(q1, r) * jnp.exp2(ends[r - 1]) for r in range(NLAG)], axis=0)
            k_dec = jnp.concatenate(
                [blk(k_hat, r) * jnp.exp2(ends[NLAG - 1] - ends[r]) for r in range(NLAG)],
                axis=0).astype(BF16)
            dec = jnp.exp2(ends[NLAG - 1])

            for hh in range(hpg):
                h = gi * hpg + hh
                if hpg == 1:
                    k_h, qin[gi, hh, c] = k_hat.astype(BF16), q_in.astype(BF16)
                else:
                    in_head = (lane >= hh * dk) & (lane < (hh + 1) * dk)
                    k_h = jnp.where(in_head, k_hat, 0.0).astype(BF16)
                    qin[gi, hh, c] = jnp.where(in_head, q_in, 0.0).astype(BF16)
                s_all = lax.dot_general(q_lag, k_h, (((1,), (1,)), ((), ())),
                                        preferred_element_type=F32)
                p_rows = []
                for r in range(NLAG):
                    p_r = jnp.where(diag_block[r], blk(s_all, r), 0.0)
                    for l in range(1, r + 1):
                        src = LAG_OFFSETS[l] + SUB * (r - l)
                        p_r = jnp.where(col_block[r - l], s_all[src:src + SUB, :], p_r)
                    p_rows.append(p_r)
                probs[gi, hh, c] = jnp.concatenate(p_rows, axis=0).astype(BF16)
                upd = lax.dot_general(v_of(h, c), k_dec, (((0,), (0,)), ((), ())),
                                      preferred_element_type=F32)
                start_state[gi, hh, c] = state[gi, hh].T.astype(BF16)
                state[gi, hh] = state[gi, hh] * dec + upd

    def outputs(c):
        for gi, hh in head_ids:
            h = gi * hpg + hh
            lhs = jnp.concatenate([qin.pop((gi, hh, c)), probs.pop((gi, hh, c))], axis=1)
            rhs = jnp.concatenate([start_state.pop((gi, hh, c)), v_of(h, c)], axis=0)
            o = jnp.dot(lhs, rhs, preferred_element_type=F32)
            rws, cols = slice(c * CHUNK, (c + 1) * CHUNK), slice(h * dv, (h + 1) * dv)
            y = o * lax.rsqrt(jnp.mean(o * o, axis=-1, keepdims=True) + EPS) * nw_ref[:, cols]
            y = y * _silu(gate_ref[rws, cols])
            y_ref[rws, y_col0 + h * dv:y_col0 + (h + 1) * dv] = y.astype(BF16)

    def store_state():
        for gi, hh in head_ids:
            st_ref[gi * hpg + hh] = state[gi, hh]

    return cumsum_block, scores_and_state, outputs, store_state


_MIXERS = (dict(heads=GLA_HEADS, dk=GLA_DK, dv=GLA_DV), dict(heads=HGRN_HEADS, dk=HGRN_DF, dv=HGRN_DV))


def _mixer_kernel(x_ref, wn_ref, wt_ref, wgk_ref, bgk_ref, lbl_ref, nwa_ref, nwb_ref, *rest,
                  n_cast):
    cast_in, rest = rest[:n_cast], rest[n_cast:]
    y_ref, rest = rest[0], rest[1:]
    cast_out, rest = rest[:n_cast], rest[n_cast:]
    (wa_ref, wbraw_ref, wb_ref,
     qa_ref, ka_ref, ga_ref, va_ref, gga_ref, ba_ref, sta_ref,
     qb_ref, kb_ref, gb_ref, vb_ref, ggb_ref, bb_ref, stb_ref) = rest

    @pl.when((pl.program_id(0) == 0) & (pl.program_id(1) == 0))
    def _():
        _transpose_projection_weight(wt_ref, wa_ref, wbraw_ref, wb_ref)

    @pl.when(pl.program_id(1) == 0)
    def _():
        sta_ref[...] = jnp.zeros_like(sta_ref)
        stb_ref[...] = jnp.zeros_like(stb_ref)

    for src, dst in zip(cast_in, cast_out):
        dst[...] = src[...].astype(BF16)

    decays_and_operands, output_gates = _projection_program(
        x_ref, wn_ref, wa_ref, wbraw_ref, wb_ref, wgk_ref, bgk_ref, lbl_ref,
        dict(q=qa_ref, k=ka_ref, g=ga_ref, v=va_ref, gate=gga_ref),
        dict(q=qb_ref, k=kb_ref, g=gb_ref, v=vb_ref, gate=ggb_ref))
    programs = [
        _mixer_program(qa_ref, ka_ref, ga_ref, va_ref, gga_ref, nwa_ref, y_ref, 0,
                       ba_ref, sta_ref, **_MIXERS[0]),
        _mixer_program(qb_ref, kb_ref, gb_ref, vb_ref, ggb_ref, nwb_ref, y_ref, GLA_V,
                       bb_ref, stb_ref, **_MIXERS[1]),
    ]
    nchunk = x_ref.shape[0] // CHUNK
    chunks_per_block = CUMSUM_ROWS // CHUNK

    decays_and_operands()
    for cumsum_block, _, _, _ in programs:
        cumsum_block(0)
    output_gates()
    for c in range(nchunk):
        for cumsum_block, scores_and_state, outputs, _ in programs:
            scores_and_state(c)
            if c >= 1:
                outputs(c - 1)
            if (c + 2) % chunks_per_block == 0 and (c + 2) < nchunk:
                cumsum_block((c + 2) * CHUNK)
    for _, _, outputs, store_state in programs:
        outputs(nchunk - 1)
        store_state()


def _mixer_call(x2, wn, w_t, wgk_p, bgk, lbl, nwa, nwb, cast_weights, *, batch):
    m = x2.shape[0]
    tm = MIXER_ROWS
    nt = m // batch // tm
    steps = batch * nt
    row = lambda width: pl.BlockSpec((tm, width), lambda b, i: (b * nt + i, 0))
    full = lambda a: pl.BlockSpec(a.shape, lambda b, i: (0,) * a.ndim)
    resident = lambda a: pl.BlockSpec(a.shape, lambda b, i: (0,) * a.ndim,
                                      pipeline_mode=pl.Buffered(1))
    slab = lambda a: pl.BlockSpec((a.shape[0] // steps, a.shape[1]), lambda b, i: (b * nt + i, 0))
    out_shapes = (jax.ShapeDtypeStruct((m, GLA_V + HGRN_V), BF16),)
    cast_shapes = tuple(jax.ShapeDtypeStruct(w.shape, BF16) for w in cast_weights)
    scratch = [pltpu.VMEM((D_MODEL, GLA_COLS), BF16), pltpu.VMEM((D_MODEL, LANES), BF16),
               pltpu.VMEM((D_MODEL, HGRN_COLS), BF16)]
    for mix in _MIXERS:
        w, gw = mix["heads"] * mix["dk"], max(mix["dk"], LANES)
        wv = mix["heads"] * mix["dv"]
        scratch += [pltpu.VMEM((tm, w), F32), pltpu.VMEM((tm, w), F32), pltpu.VMEM((tm, w), F32),
                    pltpu.VMEM((tm, wv), BF16), pltpu.VMEM((tm, wv), F32),
                    pltpu.VMEM((tm, w), F32), pltpu.VMEM((mix["heads"], mix["dv"], gw), F32)]
    outs = pl.pallas_call(
        functools.partial(_mixer_kernel, n_cast=len(cast_weights)),
        grid=(batch, nt),
        in_specs=[row(D_MODEL), full(wn), resident(w_t), full(wgk_p), full(bgk), full(lbl),
                  full(nwa), full(nwb)] + [slab(w) for w in cast_weights],
        out_specs=tuple(row(s.shape[1]) for s in out_shapes) + tuple(slab(w) for w in cast_weights),
        out_shape=out_shapes + cast_shapes,
        scratch_shapes=scratch,
        compiler_params=pltpu.CompilerParams(
            dimension_semantics=("arbitrary", "arbitrary"), vmem_limit_bytes=VMEM_LIMIT_BYTES),
        name="proj_gla_hgrn_mixers",
    )(x2, wn, w_t, wgk_p, bgk, lbl, nwa, nwb, *cast_weights)
    return outs[0], outs[1:]


def _tail_kernel(x_ref, y_ref, w_out_ref, post_mix_ref, pre_mlp_ref, post_mlp_ref,
                 w_up_ref, w_down_ref, out_ref):
    sub = x_ref.shape[0] // TAIL_SUBTILES
    tiles = [slice(s * sub, (s + 1) * sub) for s in range(TAIL_SUBTILES)]
    hs, us = {}, {}

    def out_proj(s):
        r = tiles[s]
        mix = jnp.dot(y_ref[r, :], w_out_ref[...], preferred_element_type=F32)
        hs[s] = x_ref[r, :] + _rmsnorm(mix, post_mix_ref[...])

    def up(s):
        us[s] = jnp.dot(_rmsnorm(hs[s], pre_mlp_ref[...]).astype(BF16), w_up_ref[...],
                        preferred_element_type=F32)

    def down(s):
        relu = jnp.maximum(us.pop(s), 0.0)
        m = jnp.dot((relu * relu).astype(BF16), w_down_ref[...], preferred_element_type=F32)
        out_ref[tiles[s], :] = hs.pop(s) + _rmsnorm(m, post_mlp_ref[...])

    for stage in (out_proj, up, down):
        for s in range(TAIL_SUBTILES):
            stage(s)


def _tail_call(x2, y, w_out, post_mix, pre_mlp, post_mlp, w_up, w_down):
    m = x2.shape[0]
    tm = TAIL_ROWS
    row = lambda width: pl.BlockSpec((tm, width), lambda i: (i, 0))
    full = lambda a: pl.BlockSpec(a.shape, lambda i: (0,) * a.ndim)
    resident = lambda a: pl.BlockSpec(a.shape, lambda i: (0,) * a.ndim,
                                      pipeline_mode=pl.Buffered(1))
    return pl.pallas_call(
        _tail_kernel,
        grid=(m // tm,),
        in_specs=[row(D_MODEL), row(GLA_V + HGRN_V), resident(w_out), full(post_mix), full(pre_mlp),
                  full(post_mlp), resident(w_up), resident(w_down)],
        out_specs=row(D_MODEL),
        out_shape=jax.ShapeDtypeStruct((m, D_MODEL), F32),
        compiler_params=pltpu.CompilerParams(
            dimension_semantics=("parallel",), vmem_limit_bytes=VMEM_LIMIT_BYTES),
        name="out_proj_mlp",
    )(x2, y, w_out, post_mix, pre_mlp, post_mlp, w_up, w_down)


def kernel(x, w_in, w_gk_up, b_gk, gla_norm_w, hgrn_norm_w, hgrn_lower_bounds, w_out,
           pre_mix_norm, post_mix_norm, pre_mlp_norm, post_mlp_norm, w_up, w_down):
    batch, seq, d = x.shape
    x2 = x.reshape(batch * seq, d)
    l = 0
    w_t = jnp.transpose(w_in[l]).astype(BF16)
    wgk_p = jnp.pad(w_gk_up[l], ((0, LANES - GLA_GATE_RANK), (0, 0))).astype(BF16)
    row2 = lambda a: a.reshape(1, -1)

    y, (w_out_b, w_up_b, w_down_b) = _mixer_call(
        x2, row2(pre_mix_norm[l]), w_t, wgk_p, row2(b_gk[l]), hgrn_lower_bounds,
        row2(gla_norm_w[l]), row2(hgrn_norm_w[l]), (w_out[l], w_up[l], w_down[l]), batch=batch)
    out = _tail_call(x2, y, w_out_b, row2(post_mix_norm[l]), row2(pre_mlp_norm[l]),
                     row2(post_mlp_norm[l]), w_up_b, w_down_b)
    return out.reshape(batch, seq, d)
```

```python
import functools

import jax
import jax.numpy as jnp
from jax import lax
from jax.experimental import pallas as pl
from jax.experimental.pallas import tpu as pltpu

F32 = jnp.float32
BF16 = jnp.bfloat16

D_MODEL = 1024
GLA_HEADS, GLA_DK, GLA_DV = 4, 64, 128
GLA_QK = GLA_HEADS * GLA_DK
GLA_V = GLA_HEADS * GLA_DV
GLA_GATE_RANK = 16
GLA_GATE_NORM = 16.0
HGRN_HEADS, HGRN_DF, HGRN_DV = 4, 128, 128
HGRN_F = HGRN_HEADS * HGRN_DF
HGRN_V = HGRN_HEADS * HGRN_DV
D_FF = 4 * D_MODEL
EPS = 1e-6
LOG2_E = 1.4426950408889634

LANES = 128
VMEM_LIMIT_BYTES = 56 * 1024 * 1024

_C_GQ = 0
_C_GK = _C_GQ + GLA_QK
_C_GV = _C_GK + GLA_QK
_C_GG = _C_GV + GLA_V
GLA_COLS = _C_GG + GLA_V
_C_HQ = 0
_C_HF = _C_HQ + HGRN_F
_C_HI = _C_HF + HGRN_F
_C_HG = _C_HI + HGRN_V
HGRN_COLS = _C_HG + HGRN_V

MIXER_ROWS = 512
TRANSPOSE_ROWS = 512
CUMSUM_ROWS = 256
CHUNK = 64
SUB = 16
NLAG = CHUNK // SUB
LAG_OFFSETS = [sum(CHUNK - SUB * m for m in range(l)) for l in range(NLAG + 1)]
LAG_ROWS = LAG_OFFSETS[NLAG]
TAIL_ROWS = 1024
TAIL_SUBTILES = 2


def _rmsnorm(x, w):
    return x * lax.rsqrt(jnp.mean(x * x, axis=-1, keepdims=True) + EPS) * w


def _sigmoid(x):
    return 1.0 / (1.0 + jnp.exp(-x))


def _silu(x):
    return x * _sigmoid(x)


def _log_sigmoid(x):
    return -(jnp.maximum(-x, 0.0) + jnp.log(1.0 + jnp.exp(-jnp.abs(x))))


def _transpose_projection_weight(wt_ref, wa_ref, wbraw_ref, wb_ref):
    for r in range(0, GLA_COLS, TRANSPOSE_ROWS):
        wa_ref[:, r:r + TRANSPOSE_ROWS] = wt_ref[r:r + TRANSPOSE_ROWS, :].T
    wbraw_ref[...] = wt_ref[GLA_COLS:GLA_COLS + LANES, :].T
    hgrn0 = GLA_COLS + GLA_GATE_RANK
    for r in range(0, HGRN_COLS, TRANSPOSE_ROWS):
        wb_ref[:, r:r + TRANSPOSE_ROWS] = wt_ref[hgrn0 + r:hgrn0 + r + TRANSPOSE_ROWS, :].T


def _projection_program(x_ref, wn_ref, wa_ref, wbraw_ref, wb_ref, wgk_ref, bgk_ref, lbl_ref,
                        gla, hgrn):
    x = x_ref[...]
    xg = (x * wn_ref[...]).astype(BF16)
    rstd = lax.rsqrt(jnp.mean(x * x, axis=-1, keepdims=True) + EPS)

    def proj(w_ref, lo, width):
        return rstd * jnp.dot(xg, w_ref[:, lo:lo + width], preferred_element_type=F32)

    def decays_and_operands():
        lbl = lbl_ref[...]
        e = jnp.exp(lbl - jnp.max(lbl, axis=0, keepdims=True))
        lb = e[0:1, :] / jnp.sum(e, axis=0, keepdims=True)
        f = lb + (1.0 - lb) * _sigmoid(proj(wb_ref, _C_HF, HGRN_F))
        hgrn["k"][...] = 1.0 - f
        hgrn["g"][...] = jnp.log(f) * LOG2_E
        glr = proj(wbraw_ref, 0, LANES).astype(BF16)
        gla["q"][...] = proj(wa_ref, _C_GQ, GLA_QK) * (GLA_DK ** -0.5)
        gla["k"][...] = proj(wa_ref, _C_GK, GLA_QK)
        gla["v"][...] = proj(wa_ref, _C_GV, GLA_V).astype(BF16)
        gk = jnp.dot(glr, wgk_ref[...], preferred_element_type=F32) + bgk_ref[...]
        gla["g"][...] = _log_sigmoid(gk) * (LOG2_E / GLA_GATE_NORM)
        hgrn["q"][...] = _silu(proj(wb_ref, _C_HQ, HGRN_F))
        hgrn["v"][...] = proj(wb_ref, _C_HI, HGRN_V).astype(BF16)

    def output_gates():
        gla["gate"][...] = proj(wa_ref, _C_GG, GLA_V)
        hgrn["gate"][...] = proj(wb_ref, _C_HG, HGRN_V)

    return decays_and_operands, output_gates


def _mixer_program(q_ref, k_ref, g_ref, v_ref, gate_ref, nw_ref, y_ref, y_col0, b_ref, st_ref, *,
                   heads, dk, dv):
    gw = max(dk, LANES)
    hpg = gw // dk
    ngroups = heads // hpg

    ri = lax.broadcasted_iota(jnp.int32, (CUMSUM_ROWS, CUMSUM_ROWS), 0)
    ci = lax.broadcasted_iota(jnp.int32, (CUMSUM_ROWS, CUMSUM_ROWS), 1)
    tri = jnp.where((ri // CHUNK == ci // CHUNK) & (ci <= ri), 1.0, 0.0).astype(BF16)

    def cumsum_block(r0):
        blk = slice(r0, r0 + CUMSUM_ROWS)
        g = g_ref[blk, :]
        g_hi = g.astype(BF16)
        g_lo = (g - g_hi.astype(F32)).astype(BF16)
        b_ref[blk, :] = (jnp.dot(tri, g_hi, preferred_element_type=F32)
                         + jnp.dot(tri, g_lo, preferred_element_type=F32))

    si = lax.broadcasted_iota(jnp.int32, (SUB, CHUNK), 0)
    sj = lax.broadcasted_iota(jnp.int32, (SUB, CHUNK), 1)
    col_block = [sj // SUB == r for r in range(NLAG)]
    diag_block = [col_block[r] & (sj - SUB * r <= si) for r in range(NLAG)]
    lane = lax.broadcasted_iota(jnp.int32, (1, gw), 1)
    feat = lax.broadcasted_iota(jnp.int32, (gw, 1), 0)
    feat_in_head = [(feat >= hh * dk) & (feat < (hh + 1) * dk) for hh in range(hpg)]

    head_ids = [(gi, hh) for gi in range(ngroups) for hh in range(hpg)]
    v_of = lambda h, c: v_ref[c * CHUNK:(c + 1) * CHUNK, h * dv:(h + 1) * dv]
    state = {(gi, hh): st_ref[gi * hpg + hh] for gi, hh in head_ids}
    probs, qin, start_state = {}, {}, {}

    def scores_and_state(c):
        r0 = c * CHUNK
        rws = slice(r0, r0 + CHUNK)
        for gi in range(ngroups):
            lanes = slice(gi * gw, (gi + 1) * gw)
            b = b_ref[rws, lanes]
            q = q_ref[rws, lanes]
            k = k_ref[rws, lanes]
            ends = {r: b_ref[r0 + SUB * r + SUB - 1:r0 + SUB * r + SUB, lanes] for r in range(NLAG)}
            ends[-1] = jnp.zeros((1, gw), F32)
            blk = lambda a, r: a[SUB * r:SUB * (r + 1), :]
            rows_of = lambda f: jnp.concatenate(
                [jnp.broadcast_to(f(r), (SUB, gw)) for r in range(NLAG)], axis=0)

            q0 = q * jnp.exp2(b - rows_of(lambda r: ends[r]))
            q1 = q * jnp.exp2(b - rows_of(lambda r: ends[r - 1]))
            k_hat = k * jnp.exp2(rows_of(lambda r: ends[r]) - b)
            lag_rows = [q0.astype(BF16), q1[SUB:, :].astype(BF16)]
            for l in range(2, NLAG):
                lag_rows.append(jnp.concatenate(
                    [blk(q1, r) * jnp.exp2(ends[r - 1] - ends[r - l]) for r in range(l, NLAG)],
                    axis=0).astype(BF16))
            q_lag = jnp.concatenate(lag_rows, axis=0)
            q_in = jnp.concatenate(
                [blk(q1, r) * jnp.exp2(ends[r - 1]) for r in range(NLAG)], axis=0)
            k_dec = jnp.concatenate(
                [blk(k_hat, r) * jnp.exp2(ends[NLAG - 1] - ends[r]) for r in range(NLAG)],
                axis=0).astype(BF16)
            dec = jnp.exp2(ends[NLAG - 1])

            k_hat_t = k_hat.T
            for hh in range(hpg):
                h = gi * hpg + hh
                if hpg == 1:
                    k_h, qin[gi, hh, c] = k_hat_t.astype(BF16), q_in.astype(BF16)
                else:
                    in_head = (lane >= hh * dk) & (lane < (hh + 1) * dk)
                    k_h = jnp.where(feat_in_head[hh], k_hat_t, 0.0).astype(BF16)
                    qin[gi, hh, c] = jnp.where(in_head, q_in, 0.0).astype(BF16)
                s_all = jnp.dot(q_lag, k_h, preferred_element_type=F32)
                p_rows = []
                for r in range(NLAG):
                    p_r = jnp.where(diag_block[r], blk(s_all, r), 0.0)
                    for l in range(1, r + 1):
                        src = LAG_OFFSETS[l] + SUB * (r - l)
                        p_r = jnp.where(col_block[r - l], s_all[src:src + SUB, :], p_r)
                    p_rows.append(p_r)
                probs[gi, hh, c] = jnp.concatenate(p_rows, axis=0).astype(BF16)
                upd = lax.dot_general(v_of(h, c), k_dec, (((0,), (0,)), ((), ())),
                                      preferred_element_type=F32)
                start_state[gi, hh, c] = state[gi, hh].T.astype(BF16)
                state[gi, hh] = state[gi, hh] * dec + upd

    def outputs(c):
        for gi, hh in head_ids:
            h = gi * hpg + hh
            lhs = jnp.concatenate([qin.pop((gi, hh, c)), probs.pop((gi, hh, c))], axis=1)
            rhs = jnp.concatenate([start_state.pop((gi, hh, c)), v_of(h, c)], axis=0)
            o = jnp.dot(lhs, rhs, preferred_element_type=F32)
            rws, cols = slice(c * CHUNK, (c + 1) * CHUNK), slice(h * dv, (h + 1) * dv)
            y = o * lax.rsqrt(jnp.mean(o * o, axis=-1, keepdims=True) + EPS) * nw_ref[:, cols]
            y = y * _silu(gate_ref[rws, cols])
            y_ref[rws, y_col0 + h * dv:y_col0 + (h + 1) * dv] = y.astype(BF16)

    def store_state():
        for gi, hh in head_ids:
            st_ref[gi * hpg + hh] = state[gi, hh]

    return cumsum_block, scores_and_state, outputs, store_state


_MIXERS = (dict(heads=GLA_HEADS, dk=GLA_DK, dv=GLA_DV), dict(heads=HGRN_HEADS, dk=HGRN_DF, dv=HGRN_DV))


def _mixer_kernel(x_ref, wn_ref, wt_ref, wgk_ref, bgk_ref, lbl_ref, nwa_ref, nwb_ref, *rest,
                  n_cast):
    cast_in, rest = rest[:n_cast], rest[n_cast:]
    y_ref, rest = rest[0], rest[1:]
    cast_out, rest = rest[:n_cast], rest[n_cast:]
    (wa_ref, wbraw_ref, wb_ref,
     qa_ref, ka_ref, ga_ref, va_ref, gga_ref, ba_ref, sta_ref,
     qb_ref, kb_ref, gb_ref, vb_ref, ggb_ref, bb_ref, stb_ref) = rest

    @pl.when((pl.program_id(0) == 0) & (pl.program_id(1) == 0))
    def _():
        _transpose_projection_weight(wt_ref, wa_ref, wbraw_ref, wb_ref)

    @pl.when(pl.program_id(1) == 0)
    def _():
        sta_ref[...] = jnp.zeros_like(sta_ref)
        stb_ref[...] = jnp.zeros_like(stb_ref)

    for src, dst in zip(cast_in, cast_out):
        dst[...] = src[...].astype(BF16)

    decays_and_operands, output_gates = _projection_program(
        x_ref, wn_ref, wa_ref, wbraw_ref, wb_ref, wgk_ref, bgk_ref, lbl_ref,
        dict(q=qa_ref, k=ka_ref, g=ga_ref, v=va_ref, gate=gga_ref),
        dict(q=qb_ref, k=kb_ref, g=gb_ref, v=vb_ref, gate=ggb_ref))
    programs = [
        _mixer_program(qa_ref, ka_ref, ga_ref, va_ref, gga_ref, nwa_ref, y_ref, 0,
                       ba_ref, sta_ref, **_MIXERS[0]),
        _mixer_program(qb_ref, kb_ref, gb_ref, vb_ref, ggb_ref, nwb_ref, y_ref, GLA_V,
                       bb_ref, stb_ref, **_MIXERS[1]),
    ]
    nchunk = x_ref.shape[0] // CHUNK
    chunks_per_block = CUMSUM_ROWS // CHUNK

    decays_and_operands()
    for cumsum_block, _, _, _ in programs:
        cumsum_block(0)
    output_gates()
    for c in range(nchunk):
        for cumsum_block, scores_and_state, outputs, _ in programs:
            scores_and_state(c)
            if c >= 1:
                outputs(c - 1)
            if (c + 2) % chunks_per_block == 0 and (c + 2) < nchunk:
                cumsum_block((c + 2) * CHUNK)
    for _, _, outputs, store_state in programs:
        outputs(nchunk - 1)
        store_state()


def _mixer_call(x2, wn, w_t, wgk_p, bgk, lbl, nwa, nwb, cast_weights, *, batch):
    m = x2.shape[0]
    tm = MIXER_ROWS
    nt = m // batch // tm
    steps = batch * nt
    row = lambda width: pl.BlockSpec((tm, width), lambda b, i: (b * nt + i, 0))
    full = lambda a: pl.BlockSpec(a.shape, lambda b, i: (0,) * a.ndim)
    resident = lambda a: pl.BlockSpec(a.shape, lambda b, i: (0,) * a.ndim,
                                      pipeline_mode=pl.Buffered(1))
    slab = lambda a: pl.BlockSpec((a.shape[0] // steps, a.shape[1]), lambda b, i: (b * nt + i, 0))
    out_shapes = (jax.ShapeDtypeStruct((m, GLA_V + HGRN_V), BF16),)
    cast_shapes = tuple(jax.ShapeDtypeStruct(w.shape, BF16) for w in cast_weights)
    scratch = [pltpu.VMEM((D_MODEL, GLA_COLS), BF16), pltpu.VMEM((D_MODEL, LANES), BF16),
               pltpu.VMEM((D_MODEL, HGRN_COLS), BF16)]
    for mix in _MIXERS:
        w, gw = mix["heads"] * mix["dk"], max(mix["dk"], LANES)
        wv = mix["heads"] * mix["dv"]
        scratch += [pltpu.VMEM((tm, w), F32), pltpu.VMEM((tm, w), F32), pltpu.VMEM((tm, w), F32),
                    pltpu.VMEM((tm, wv), BF16), pltpu.VMEM((tm, wv), F32),
                    pltpu.VMEM((tm, w), F32), pltpu.VMEM((mix["heads"], mix["dv"], gw), F32)]
    outs = pl.pallas_call(
        functools.partial(_mixer_kernel, n_cast=len(cast_weights)),
        grid=(batch, nt),
        in_specs=[row(D_MODEL), full(wn), resident(w_t), full(wgk_p), full(bgk), full(lbl),
                  full(nwa), full(nwb)] + [slab(w) for w in cast_weights],
        out_specs=tuple(row(s.shape[1]) for s in out_shapes) + tuple(slab(w) for w in cast_weights),
        out_shape=out_shapes + cast_shapes,
        scratch_shapes=scratch,
        compiler_params=pltpu.CompilerParams(
            dimension_semantics=("arbitrary", "arbitrary"), vmem_limit_bytes=VMEM_LIMIT_BYTES),
        name="proj_gla_hgrn_mixers",
    )(x2, wn, w_t, wgk_p, bgk, lbl, nwa, nwb, *cast_weights)
    return outs[0], outs[1:]


def _tail_kernel(x_ref, y_ref, w_out_ref, post_mix_ref, pre_mlp_ref, post_mlp_ref,
                 w_up_ref, w_down_ref, out_ref):
    sub = x_ref.shape[0] // TAIL_SUBTILES
    tiles = [slice(s * sub, (s + 1) * sub) for s in range(TAIL_SUBTILES)]
    hs, us = {}, {}

    def out_proj(s):
        r = tiles[s]
        mix = jnp.dot(y_ref[r, :], w_out_ref[...], preferred_element_type=F32)
        hs[s] = x_ref[r, :] + _rmsnorm(mix, post_mix_ref[...])

    def up(s):
        us[s] = jnp.dot(_rmsnorm(hs[s], pre_mlp_ref[...]).astype(BF16), w_up_ref[...],
                        preferred_element_type=F32)

    def down(s):
        relu = jnp.maximum(us.pop(s), 0.0)
        m = jnp.dot((relu * relu).astype(BF16), w_down_ref[...], preferred_element_type=F32)
        out_ref[tiles[s], :] = hs.pop(s) + _rmsnorm(m, post_mlp_ref[...])

    for stage in (out_proj, up, down):
        for s in range(TAIL_SUBTILES):
            stage(s)


def _tail_call(x2, y, w_out, post_mix, pre_mlp, post_mlp, w_up, w_down):
    m = x2.shape[0]
    tm = TAIL_ROWS
    row = lambda width: pl.BlockSpec((tm, width), lambda i: (i, 0))
    full = lambda a: pl.BlockSpec(a.shape, lambda i: (0,) * a.ndim)
    resident = lambda a: pl.BlockSpec(a.shape, lambda i: (0,) * a.ndim,
                                      pipeline_mode=pl.Buffered(1))
    return pl.pallas_call(
        _tail_kernel,
        grid=(m // tm,),
        in_specs=[row(D_MODEL), row(GLA_V + HGRN_V), resident(w_out), full(post_mix), full(pre_mlp),
                  full(post_mlp), resident(w_up), resident(w_down)],
        out_specs=row(D_MODEL),
        out_shape=jax.ShapeDtypeStruct((m, D_MODEL), F32),
        compiler_params=pltpu.CompilerParams(
            dimension_semantics=("parallel",), vmem_limit_bytes=VMEM_LIMIT_BYTES),
        name="out_proj_mlp",
    )(x2, y, w_out, post_mix, pre_mlp, post_mlp, w_up, w_down)


def kernel(x, w_in, w_gk_up, b_gk, gla_norm_w, hgrn_norm_w, hgrn_lower_bounds, w_out,
           pre_mix_norm, post_mix_norm, pre_mlp_norm, post_mlp_norm, w_up, w_down):
    batch, seq, d = x.shape
    x2 = x.reshape(batch * seq, d)
    l = 0
    w_t = jnp.transpose(w_in[l]).astype(BF16)
    wgk_p = jnp.pad(w_gk_up[l], ((0, LANES - GLA_GATE_RANK), (0, 0))).astype(BF16)
    row2 = lambda a: a.reshape(1, -1)

    y, (w_out_b, w_up_b, w_down_b) = _mixer_call(
        x2, row2(pre_mix_norm[l]), w_t, wgk_p, row2(b_gk[l]), hgrn_lower_bounds,
        row2(gla_norm_w[l]), row2(hgrn_norm_w[l]), (w_out[l], w_up[l], w_down[l]), batch=batch)
    out = _tail_call(x2, y, w_out_b, row2(post_mix_norm[l]), row2(pre_mlp_norm[l]),
                     row2(post_mlp_norm[l]), w_up_b, w_down_b)
    return out.reshape(batch, seq, d)
```

```python
import functools

import jax
import jax.numpy as jnp
from jax import lax
from jax.experimental import pallas as pl
from jax.experimental.pallas import tpu as pltpu

F32 = jnp.float32
BF16 = jnp.bfloat16

D_MODEL = 1024
GLA_HEADS, GLA_DK, GLA_DV = 4, 64, 128
GLA_QK = GLA_HEADS * GLA_DK
GLA_V = GLA_HEADS * GLA_DV
GLA_GATE_RANK = 16
GLA_GATE_NORM = 16.0
HGRN_HEADS, HGRN_DF, HGRN_DV = 4, 128, 128
HGRN_F = HGRN_HEADS * HGRN_DF
HGRN_V = HGRN_HEADS * HGRN_DV
D_FF = 4 * D_MODEL
EPS = 1e-6
LOG2_E = 1.4426950408889634

LANES = 128
VMEM_LIMIT_BYTES = 56 * 1024 * 1024

_C_GQ = 0
_C_GK = _C_GQ + GLA_QK
_C_GV = _C_GK + GLA_QK
_C_GG = _C_GV + GLA_V
GLA_COLS = _C_GG + GLA_V
_C_HQ = 0
_C_HF = _C_HQ + HGRN_F
_C_HI = _C_HF + HGRN_F
_C_HG = _C_HI + HGRN_V
HGRN_COLS = _C_HG + HGRN_V

MIXER_ROWS = 512
TRANSPOSE_ROWS = 512
CUMSUM_ROWS = 256
CHUNK = 64
SUB = 16
NLAG = CHUNK // SUB
LAG_OFFSETS = [sum(CHUNK - SUB * m for m in range(l)) for l in range(NLAG + 1)]
LAG_ROWS = LAG_OFFSETS[NLAG]
TAIL_ROWS = 1024
TAIL_SUBTILES = 2


def _rmsnorm(x, w):
    return x * lax.rsqrt(jnp.mean(x * x, axis=-1, keepdims=True) + EPS) * w


def _sigmoid(x):
    return 1.0 / (1.0 + jnp.exp(-x))


def _silu(x):
    return x * _sigmoid(x)


def _log_sigmoid(x):
    return -(jnp.maximum(-x, 0.0) + jnp.log(1.0 + jnp.exp(-jnp.abs(x))))


def _transpose_projection_weight(wt_ref, wa_ref, wbraw_ref, wb_ref):
    for r in range(0, GLA_COLS, TRANSPOSE_ROWS):
        wa_ref[:, r:r + TRANSPOSE_ROWS] = wt_ref[r:r + TRANSPOSE_ROWS, :].T
    wbraw_ref[...] = wt_ref[GLA_COLS:GLA_COLS + LANES, :].T
    hgrn0 = GLA_COLS + GLA_GATE_RANK
    for r in range(0, HGRN_COLS, TRANSPOSE_ROWS):
        wb_ref[:, r:r + TRANSPOSE_ROWS] = wt_ref[hgrn0 + r:hgrn0 + r + TRANSPOSE_ROWS, :].T


def _projection_program(x_ref, wn_ref, wa_ref, wbraw_ref, wb_ref, wgk_ref, bgk_ref, lbl_ref,
                        gla, hgrn):
    x = x_ref[...]
    xg = (x * wn_ref[...]).astype(BF16)
    rstd = lax.rsqrt(jnp.mean(x * x, axis=-1, keepdims=True) + EPS)

    def proj(w_ref, lo, width):
        return rstd * jnp.dot(xg, w_ref[:, lo:lo + width], preferred_element_type=F32)

    def decays_and_operands():
        lbl = lbl_ref[...]
        e = jnp.exp(lbl - jnp.max(lbl, axis=0, keepdims=True))
        lb = e[0:1, :] / jnp.sum(e, axis=0, keepdims=True)
        f = lb + (1.0 - lb) * _sigmoid(proj(wb_ref, _C_HF, HGRN_F))
        hgrn["k"][...] = 1.0 - f
        hgrn["g"][...] = jnp.log(f) * LOG2_E
        glr = proj(wbraw_ref, 0, LANES).astype(BF16)
        gla["q"][...] = proj(wa_ref, _C_GQ, GLA_QK) * (GLA_DK ** -0.5)
        gla["k"][...] = proj(wa_ref, _C_GK, GLA_QK)
        gla["v"][...] = proj(wa_ref, _C_GV, GLA_V).astype(BF16)
        gk = jnp.dot(glr, wgk_ref[...], preferred_element_type=F32) + bgk_ref[...]
        gla["g"][...] = _log_sigmoid(gk) * (LOG2_E / GLA_GATE_NORM)
        hgrn["q"][...] = _silu(proj(wb_ref, _C_HQ, HGRN_F))
        hgrn["v"][...] = proj(wb_ref, _C_HI, HGRN_V).astype(BF16)

    def output_gates():
        gla["gate"][...] = proj(wa_ref, _C_GG, GLA_V)
        hgrn["gate"][...] = proj(wb_ref, _C_HG, HGRN_V)

    return decays_and_operands, output_gates


def _mixer_program(q_ref, k_ref, g_ref, v_ref, gate_ref, nw_ref, y_ref, y_col0, b_ref, st_ref, *,
                   heads, dk, dv):
    gw = max(dk, LANES)
    hpg = gw // dk
    ngroups = heads // hpg

    ri = lax.broadcasted_iota(jnp.int32, (CUMSUM_ROWS, CUMSUM_ROWS), 0)
    ci = lax.broadcasted_iota(jnp.int32, (CUMSUM_ROWS, CUMSUM_ROWS), 1)
    tri = jnp.where((ri // CHUNK == ci // CHUNK) & (ci <= ri), 1.0, 0.0).astype(BF16)

    def cumsum_block(r0):
        blk = slice(r0, r0 + CUMSUM_ROWS)
        g = g_ref[blk, :]
        g_hi = g.astype(BF16)
        g_lo = (g - g_hi.astype(F32)).astype(BF16)
        b_ref[blk, :] = (jnp.dot(tri, g_hi, preferred_element_type=F32)
                         + jnp.dot(tri, g_lo, preferred_element_type=F32))

    si = lax.broadcasted_iota(jnp.int32, (SUB, CHUNK), 0)
    sj = lax.broadcasted_iota(jnp.int32, (SUB, CHUNK), 1)
    col_block = [sj // SUB == r for r in range(NLAG)]
    diag_block = [col_block[r] & (sj - SUB * r <= si) for r in range(NLAG)]
    lane = lax.broadcasted_iota(jnp.int32, (1, gw), 1)

    head_ids = [(gi, hh) for gi in range(ngroups) for hh in range(hpg)]
    v_of = lambda h, c: v_ref[c * CHUNK:(c + 1) * CHUNK, h * dv:(h + 1) * dv]
    state = {(gi, hh): st_ref[gi * hpg + hh] for gi, hh in head_ids}
    probs, qin, start_state = {}, {}, {}

    def scores_and_state(c):
        r0 = c * CHUNK
        rws = slice(r0, r0 + CHUNK)
        for gi in range(ngroups):
            lanes = slice(gi * gw, (gi + 1) * gw)
            b = b_ref[rws, lanes]
            q = q_ref[rws, lanes]
            k = k_ref[rws, lanes]
            ends = {r: b_ref[r0 + SUB * r + SUB - 1:r0 + SUB * r + SUB, lanes] for r in range(NLAG)}
            ends[-1] = jnp.zeros((1, gw), F32)
            blk = lambda a, r: a[SUB * r:SUB * (r + 1), :]
            rows_of = lambda f: jnp.concatenate(
                [jnp.broadcast_to(f(r), (SUB, gw)) for r in range(NLAG)], axis=0)

            q0 = q * jnp.exp2(b - rows_of(lambda r: ends[r]))
            q1 = q * jnp.exp2(b - rows_of(lambda r: ends[r - 1]))
            k_hat = k * jnp.exp2(rows_of(lambda r: ends[r]) - b)
            lag_rows = [q0.astype(BF16), q1[SUB:, :].astype(BF16)]
            for l in range(2, NLAG):
                lag_rows.append(jnp.concatenate(
                    [blk(q1, r) * jnp.exp2(ends[r - 1] - ends[r - l]) for r in range(l, NLAG)],
                    axis=0).astype(BF16))
            q_lag = jnp.concatenate(lag_rows, axis=0)
            q_in = jnp.concatenate(
                [blk(q1, r) * jnp.exp2(ends[r - 1]) for r in range(NLAG)], axis=0)
            k_dec = jnp.concatenate(
                [blk(k_hat, r) * jnp.exp2(ends[NLAG - 1] - ends[r]) for r in range(NLAG)],
                axis=0).astype(BF16)
            dec = jnp.exp2(ends[NLAG - 1])

            for hh in range(hpg):
                h = gi * hpg + hh
                if hpg == 1:
                    k_h, qin[gi, hh, c] = k_hat.astype(BF16), q_in.astype(BF16)
                else:
                    in_head = (lane >= hh * dk) & (lane < (hh + 1) * dk)
                    k_h = jnp.where(in_head, k_hat, 0.0).astype(BF16)
                    qin[gi, hh, c] = jnp.where(in_head, q_in, 0.0).astype(BF16)
                s_all = lax.dot_general(q_lag, k_h, (((1,), (1,)), ((), ())),
                                        preferred_element_type=F32)
                p_rows = []
                for r in range(NLAG):
                    p_r = jnp.where(diag_block[r], blk(s_all, r), 0.0)
                    for l in range(1, r + 1):
                        src = LAG_OFFSETS[l] + SUB * (r - l)
                        p_r = jnp.where(col_block[r - l], s_all[src:src + SUB, :], p_r)
                    p_rows.append(p_r)
                probs[gi, hh, c] = jnp.concatenate(p_rows, axis=0).astype(BF16)
                upd = lax.dot_general(v_of(h, c), k_dec, (((0,), (0,)), ((), ())),
                                      preferred_element_type=F32)
                start_state[gi, hh, c] = state[gi, hh].astype(BF16).T
                state[gi, hh] = state[gi, hh] * dec + upd

    def outputs(c):
        for gi, hh in head_ids:
            h = gi * hpg + hh
            lhs = jnp.concatenate([qin.pop((gi, hh, c)), probs.pop((gi, hh, c))], axis=1)
            rhs = jnp.concatenate([start_state.pop((gi, hh, c)), v_of(h, c)], axis=0)
            o = jnp.dot(lhs, rhs, preferred_element_type=F32)
            rws, cols = slice(c * CHUNK, (c + 1) * CHUNK), slice(h * dv, (h + 1) * dv)
            y = o * lax.rsqrt(jnp.mean(o * o, axis=-1, keepdims=True) + EPS) * nw_ref[:, cols]
            y = y * _silu(gate_ref[rws, cols])
            y_ref[rws, y_col0 + h * dv:y_col0 + (h + 1) * dv] = y.astype(BF16)

    def store_state():
        for gi, hh in head_ids:
            st_ref[gi * hpg + hh] = state[gi, hh]

    return cumsum_block, scores_and_state, outputs, store_state


_MIXERS = (dict(heads=GLA_HEADS, dk=GLA_DK, dv=GLA_DV), dict(heads=HGRN_HEADS, dk=HGRN_DF, dv=HGRN_DV))


def _mixer_kernel(x_ref, wn_ref, wt_ref, wgk_ref, bgk_ref, lbl_ref, nwa_ref, nwb_ref, *rest,
                  n_cast):
    cast_in, rest = rest[:n_cast], rest[n_cast:]
    y_ref, rest = rest[0], rest[1:]
    cast_out, rest = rest[:n_cast], rest[n_cast:]
    (wa_ref, wbraw_ref, wb_ref,
     qa_ref, ka_ref, ga_ref, va_ref, gga_ref, ba_ref, sta_ref,
     qb_ref, kb_ref, gb_ref, vb_ref, ggb_ref, bb_ref, stb_ref) = rest

    @pl.when((pl.program_id(0) == 0) & (pl.program_id(1) == 0))
    def _():
        _transpose_projection_weight(wt_ref, wa_ref, wbraw_ref, wb_ref)

    @pl.when(pl.program_id(1) == 0)
    def _():
        sta_ref[...] = jnp.zeros_like(sta_ref)
        stb_ref[...] = jnp.zeros_like(stb_ref)

    for src, dst in zip(cast_in, cast_out):
        dst[...] = src[...].astype(BF16)

    decays_and_operands, output_gates = _projection_program(
        x_ref, wn_ref, wa_ref, wbraw_ref, wb_ref, wgk_ref, bgk_ref, lbl_ref,
        dict(q=qa_ref, k=ka_ref, g=ga_ref, v=va_ref, gate=gga_ref),
        dict(q=qb_ref, k=kb_ref, g=gb_ref, v=vb_ref, gate=ggb_ref))
    programs = [
        _mixer_program(qa_ref, ka_ref, ga_ref, va_ref, gga_ref, nwa_ref, y_ref, 0,
                       ba_ref, sta_ref, **_MIXERS[0]),
        _mixer_program(qb_ref, kb_ref, gb_ref, vb_ref, ggb_ref, nwb_ref, y_ref, GLA_V,
                       bb_ref, stb_ref, **_MIXERS[1]),
    ]
    nchunk = x_ref.shape[0] // CHUNK
    chunks_per_block = CUMSUM_ROWS // CHUNK

    decays_and_operands()
    for cumsum_block, _, _, _ in programs:
        cumsum_block(0)
    output_gates()
    for c in range(nchunk):
        for cumsum_block, scores_and_state, outputs, _ in programs:
            scores_and_state(c)
            if c >= 1:
                outputs(c - 1)
            if (c + 2) % chunks_per_block == 0 and (c + 2) < nchunk:
                cumsum_block((c + 2) * CHUNK)
    for _, _, outputs, store_state in programs:
        outputs(nchunk - 1)
        store_state()


def _mixer_call(x2, wn, w_t, wgk_p, bgk, lbl, nwa, nwb, cast_weights, *, batch):
    m = x2.shape[0]
    tm = MIXER_ROWS
    nt = m // batch // tm
    steps = batch * nt
    row = lambda width: pl.BlockSpec((tm, width), lambda b, i: (b * nt + i, 0))
    full = lambda a: pl.BlockSpec(a.shape, lambda b, i: (0,) * a.ndim)
    resident = lambda a: pl.BlockSpec(a.shape, lambda b, i: (0,) * a.ndim,
                                      pipeline_mode=pl.Buffered(1))
    slab = lambda a: pl.BlockSpec((a.shape[0] // steps, a.shape[1]), lambda b, i: (b * nt + i, 0))
    out_shapes = (jax.ShapeDtypeStruct((m, GLA_V + HGRN_V), BF16),)
    cast_shapes = tuple(jax.ShapeDtypeStruct(w.shape, BF16) for w in cast_weights)
    scratch = [pltpu.VMEM((D_MODEL, GLA_COLS), BF16), pltpu.VMEM((D_MODEL, LANES), BF16),
               pltpu.VMEM((D_MODEL, HGRN_COLS), BF16)]
    for mix in _MIXERS:
        w, gw = mix["heads"] * mix["dk"], max(mix["dk"], LANES)
        wv = mix["heads"] * mix["dv"]
        scratch += [pltpu.VMEM((tm, w), F32), pltpu.VMEM((tm, w), F32), pltpu.VMEM((tm, w), F32),
                    pltpu.VMEM((tm, wv), BF16), pltpu.VMEM((tm, wv), F32),
                    pltpu.VMEM((tm, w), F32), pltpu.VMEM((mix["heads"], mix["dv"], gw), F32)]
    outs = pl.pallas_call(
        functools.partial(_mixer_kernel, n_cast=len(cast_weights)),
        grid=(batch, nt),
        in_specs=[row(D_MODEL), full(wn), resident(w_t), full(wgk_p), full(bgk), full(lbl),
                  full(nwa), full(nwb)] + [slab(w) for w in cast_weights],
        out_specs=tuple(row(s.shape[1]) for s in out_shapes) + tuple(slab(w) for w in cast_weights),
        out_shape=out_shapes + cast_shapes,
        scratch_shapes=scratch,
        compiler_params=pltpu.CompilerParams(
            dimension_semantics=("arbitrary", "arbitrary"), vmem_limit_bytes=VMEM_LIMIT_BYTES),
        name="proj_gla_hgrn_mixers",
    )(x2, wn, w_t, wgk_p, bgk, lbl, nwa, nwb, *cast_weights)
    return outs[0], outs[1:]


def _tail_kernel(x_ref, y_ref, w_out_ref, post_mix_ref, pre_mlp_ref, post_mlp_ref,
                 w_up_ref, w_down_ref, out_ref):
    sub = x_ref.shape[0] // TAIL_SUBTILES
    tiles = [slice(s * sub, (s + 1) * sub) for s in range(TAIL_SUBTILES)]
    hs, us = {}, {}

    def out_proj(s):
        r = tiles[s]
        mix = jnp.dot(y_ref[r, :], w_out_ref[...], preferred_element_type=F32)
        hs[s] = x_ref[r, :] + _rmsnorm(mix, post_mix_ref[...])

    def up(s):
        us[s] = jnp.dot(_rmsnorm(hs[s], pre_mlp_ref[...]).astype(BF16), w_up_ref[...],
                        preferred_element_type=F32)

    def down(s):
        relu = jnp.maximum(us.pop(s), 0.0)
        m = jnp.dot((relu * relu).astype(BF16), w_down_ref[...], preferred_element_type=F32)
        out_ref[tiles[s], :] = hs.pop(s) + _rmsnorm(m, post_mlp_ref[...])

    for stage in (out_proj, up, down):
        for s in range(TAIL_SUBTILES):
            stage(s)


def _tail_call(x2, y, w_out, post_mix, pre_mlp, post_mlp, w_up, w_down):
    m = x2.shape[0]
    tm = TAIL_ROWS
    row = lambda width: pl.BlockSpec((tm, width), lambda i: (i, 0))
    full = lambda a: pl.BlockSpec(a.shape, lambda i: (0,) * a.ndim)
    resident = lambda a: pl.BlockSpec(a.shape, lambda i: (0,) * a.ndim,
                                      pipeline_mode=pl.Buffered(1))
    return pl.pallas_call(
        _tail_kernel,
        grid=(m // tm,),
        in_specs=[row(D_MODEL), row(GLA_V + HGRN_V), resident(w_out), full(post_mix), full(pre_mlp),
                  full(post_mlp), resident(w_up), resident(w_down)],
        out_specs=row(D_MODEL),
        out_shape=jax.ShapeDtypeStruct((m, D_MODEL), F32),
        compiler_params=pltpu.CompilerParams(
            dimension_semantics=("parallel",), vmem_limit_bytes=VMEM_LIMIT_BYTES),
        name="out_proj_mlp",
    )(x2, y, w_out, post_mix, pre_mlp, post_mlp, w_up, w_down)


def kernel(x, w_in, w_gk_up, b_gk, gla_norm_w, hgrn_norm_w, hgrn_lower_bounds, w_out,
           pre_mix_norm, post_mix_norm, pre_mlp_norm, post_mlp_norm, w_up, w_down):
    batch, seq, d = x.shape
    x2 = x.reshape(batch * seq, d)
    l = 0
    w_t = jnp.transpose(w_in[l]).astype(BF16)
    wgk_p = jnp.pad(w_gk_up[l], ((0, LANES - GLA_GATE_RANK), (0, 0))).astype(BF16)
    row2 = lambda a: a.reshape(1, -1)

    y, (w_out_b, w_up_b, w_down_b) = _mixer_call(
        x2, row2(pre_mix_norm[l]), w_t, wgk_p, row2(b_gk[l]), hgrn_lower_bounds,
        row2(gla_norm_w[l]), row2(hgrn_norm_w[l]), (w_out[l], w_up[l], w_down[l]), batch=batch)
    out = _tail_call(x2, y, w_out_b, row2(post_mix_norm[l]), row2(pre_mlp_norm[l]),
                     row2(post_mlp_norm[l]), w_up_b, w_down_b)
    return out.reshape(batch, seq, d)
```

```python
import functools

import jax
import jax.numpy as jnp
from jax import lax
from jax.experimental import pallas as pl
from jax.experimental.pallas import tpu as pltpu

F32 = jnp.float32
BF16 = jnp.bfloat16

D_MODEL = 1024
GLA_HEADS, GLA_DK, GLA_DV = 4, 64, 128
GLA_QK = GLA_HEADS * GLA_DK
GLA_V = GLA_HEADS * GLA_DV
GLA_GATE_RANK = 16
GLA_GATE_NORM = 16.0
HGRN_HEADS, HGRN_DF, HGRN_DV = 4, 128, 128
HGRN_F = HGRN_HEADS * HGRN_DF
HGRN_V = HGRN_HEADS * HGRN_DV
D_FF = 4 * D_MODEL
EPS = 1e-6
LOG2_E = 1.4426950408889634

LANES = 128
VMEM_LIMIT_BYTES = 56 * 1024 * 1024

_C_GQ = 0
_C_GK = _C_GQ + GLA_QK
_C_GV = _C_GK + GLA_QK
_C_GG = _C_GV + GLA_V
GLA_COLS = _C_GG + GLA_V
_C_HQ = 0
_C_HF = _C_HQ + HGRN_F
_C_HI = _C_HF + HGRN_F
_C_HG = _C_HI + HGRN_V
HGRN_COLS = _C_HG + HGRN_V

MIXER_ROWS = 512
TRANSPOSE_ROWS = 512
CUMSUM_ROWS = 256
CHUNK = 64
SUB = 16
NLAG = CHUNK // SUB
LAG_OFFSETS = [sum(CHUNK - SUB * m for m in range(l)) for l in range(NLAG + 1)]
LAG_ROWS = LAG_OFFSETS[NLAG]
TAIL_ROWS = 1024
TAIL_SUBTILES = 2


def _rmsnorm(x, w):
    return x * lax.rsqrt(jnp.mean(x * x, axis=-1, keepdims=True) + EPS) * w


def _sigmoid(x):
    return 1.0 / (1.0 + jnp.exp2(x * (-LOG2_E)))


def _silu(x):
    return x * _sigmoid(x)


def _log_sigmoid(x):
    return -(jnp.maximum(-x, 0.0) + jnp.log(1.0 + jnp.exp2(jnp.abs(x) * (-LOG2_E))))


def _transpose_projection_weight(wt_ref, wa_ref, wbraw_ref, wb_ref):
    for r in range(0, GLA_COLS, TRANSPOSE_ROWS):
        wa_ref[:, r:r + TRANSPOSE_ROWS] = wt_ref[r:r + TRANSPOSE_ROWS, :].T
    wbraw_ref[...] = wt_ref[GLA_COLS:GLA_COLS + LANES, :].T
    hgrn0 = GLA_COLS + GLA_GATE_RANK
    for r in range(0, HGRN_COLS, TRANSPOSE_ROWS):
        wb_ref[:, r:r + TRANSPOSE_ROWS] = wt_ref[hgrn0 + r:hgrn0 + r + TRANSPOSE_ROWS, :].T


def _projection_program(x_ref, wn_ref, wa_ref, wbraw_ref, wb_ref, wgk_ref, bgk_ref, lbl_ref,
                        gla, hgrn):
    x = x_ref[...]
    xg = (x * wn_ref[...]).astype(BF16)
    rstd = lax.rsqrt(jnp.mean(x * x, axis=-1, keepdims=True) + EPS)

    def proj(w_ref, lo, width):
        return rstd * jnp.dot(xg, w_ref[:, lo:lo + width], preferred_element_type=F32)

    def decays_and_operands():
        lbl = lbl_ref[...]
        e = jnp.exp(lbl - jnp.max(lbl, axis=0, keepdims=True))
        lb = e[0:1, :] / jnp.sum(e, axis=0, keepdims=True)
        f = lb + (1.0 - lb) * _sigmoid(proj(wb_ref, _C_HF, HGRN_F))
        hgrn["k"][...] = 1.0 - f
        hgrn["g"][...] = jnp.log(f) * LOG2_E
        glr = proj(wbraw_ref, 0, LANES).astype(BF16)
        gla["q"][...] = proj(wa_ref, _C_GQ, GLA_QK) * (GLA_DK ** -0.5)
        gla["k"][...] = proj(wa_ref, _C_GK, GLA_QK)
        gla["v"][...] = proj(wa_ref, _C_GV, GLA_V).astype(BF16)
        gk = jnp.dot(glr, wgk_ref[...], preferred_element_type=F32) + bgk_ref[...]
        gla["g"][...] = _log_sigmoid(gk) * (LOG2_E / GLA_GATE_NORM)
        hgrn["q"][...] = _silu(proj(wb_ref, _C_HQ, HGRN_F))
        hgrn["v"][...] = proj(wb_ref, _C_HI, HGRN_V).astype(BF16)

    def output_gates():
        gla["gate"][...] = gla["gain"][...] * _silu(proj(wa_ref, _C_GG, GLA_V))
        hgrn["gate"][...] = hgrn["gain"][...] * _silu(proj(wb_ref, _C_HG, HGRN_V))

    return decays_and_operands, output_gates


def _mixer_program(q_ref, k_ref, g_ref, v_ref, gate_ref, y_ref, y_col0, b_ref, st_ref, *,
                   heads, dk, dv):
    gw = max(dk, LANES)
    hpg = gw // dk
    ngroups = heads // hpg

    ri = lax.broadcasted_iota(jnp.int32, (CUMSUM_ROWS, CUMSUM_ROWS), 0)
    ci = lax.broadcasted_iota(jnp.int32, (CUMSUM_ROWS, CUMSUM_ROWS), 1)
    tri = jnp.where((ri // CHUNK == ci // CHUNK) & (ci <= ri), 1.0, 0.0).astype(BF16)

    def cumsum_block(r0):
        blk = slice(r0, r0 + CUMSUM_ROWS)
        g = g_ref[blk, :]
        g_hi = g.astype(BF16)
        g_lo = (g - g_hi.astype(F32)).astype(BF16)
        b_ref[blk, :] = (jnp.dot(tri, g_hi, preferred_element_type=F32)
                         + jnp.dot(tri, g_lo, preferred_element_type=F32))

    si = lax.broadcasted_iota(jnp.int32, (SUB, CHUNK), 0)
    sj = lax.broadcasted_iota(jnp.int32, (SUB, CHUNK), 1)
    col_block = [sj // SUB == r for r in range(NLAG)]
    diag_block = [col_block[r] & (sj - SUB * r <= si) for r in range(NLAG)]
    lane = lax.broadcasted_iota(jnp.int32, (1, gw), 1)

    head_ids = [(gi, hh) for gi in range(ngroups) for hh in range(hpg)]
    v_of = lambda h, c: v_ref[c * CHUNK:(c + 1) * CHUNK, h * dv:(h + 1) * dv]
    state = {(gi, hh): st_ref[gi * hpg + hh] for gi, hh in head_ids}
    probs, qin, start_state, prepared = {}, {}, {}, {}
    blk = lambda a, r: a[SUB * r:SUB * (r + 1), :]

    def prep(c):
        r0 = c * CHUNK
        rws = slice(r0, r0 + CHUNK)
        for gi in range(ngroups):
            lanes = slice(gi * gw, (gi + 1) * gw)
            b = b_ref[rws, lanes]
            q = q_ref[rws, lanes]
            k = k_ref[rws, lanes]
            ends = {r: b_ref[r0 + SUB * r + SUB - 1:r0 + SUB * r + SUB, lanes] for r in range(NLAG)}
            ends[-1] = jnp.zeros((1, gw), F32)
            rows_of = lambda f: jnp.concatenate(
                [jnp.broadcast_to(f(r), (SUB, gw)) for r in range(NLAG)], axis=0)

            q0 = q * jnp.exp2(b - rows_of(lambda r: ends[r]))
            q1 = q * jnp.exp2(b - rows_of(lambda r: ends[r - 1]))
            k_hat = k * jnp.exp2(rows_of(lambda r: ends[r]) - b)
            lag_rows = [q0.astype(BF16), q1[SUB:, :].astype(BF16)]
            for l in range(2, NLAG):
                lag_rows.append(jnp.concatenate(
                    [blk(q1, r) * jnp.exp2(ends[r - 1] - ends[r - l]) for r in range(l, NLAG)],
                    axis=0).astype(BF16))
            q_lag = jnp.concatenate(lag_rows, axis=0)
            q_in = jnp.concatenate(
                [blk(q1, r) * jnp.exp2(ends[r - 1]) for r in range(NLAG)], axis=0)
            k_dec = jnp.concatenate(
                [blk(k_hat, r) * jnp.exp2(ends[NLAG - 1] - ends[r]) for r in range(NLAG)],
                axis=0).astype(BF16)
            dec = jnp.exp2(ends[NLAG - 1])
            prepared[gi, c] = (q_lag, k_dec, dec)
            for hh in range(hpg):
                if hpg == 1:
                    prepared[gi, hh, c], qin[gi, hh, c] = k_hat.astype(BF16), q_in.astype(BF16)
                else:
                    in_head = (lane >= hh * dk) & (lane < (hh + 1) * dk)
                    prepared[gi, hh, c] = jnp.where(in_head, k_hat, 0.0).astype(BF16)
                    qin[gi, hh, c] = jnp.where(in_head, q_in, 0.0).astype(BF16)

    def scores_and_state(c):
        for gi in range(ngroups):
            q_lag, k_dec, dec = prepared.pop((gi, c))
            for hh in range(hpg):
                h = gi * hpg + hh
                k_h = prepared.pop((gi, hh, c))
                s_all = lax.dot_general(q_lag, k_h, (((1,), (1,)), ((), ())),
                                        preferred_element_type=F32)
                p_rows = []
                for r in range(NLAG):
                    p_r = jnp.where(diag_block[r], blk(s_all, r), 0.0)
                    for l in range(1, r + 1):
                        src = LAG_OFFSETS[l] + SUB * (r - l)
                        p_r = jnp.where(col_block[r - l], s_all[src:src + SUB, :], p_r)
                    p_rows.append(p_r)
                probs[gi, hh, c] = jnp.concatenate(p_rows, axis=0).astype(BF16)
                upd = lax.dot_general(v_of(h, c), k_dec, (((0,), (0,)), ((), ())),
                                      preferred_element_type=F32)
                start_state[gi, hh, c] = state[gi, hh].astype(BF16).T
                state[gi, hh] = state[gi, hh] * dec + upd

    def outputs(c):
        for gi, hh in head_ids:
            h = gi * hpg + hh
            lhs = jnp.concatenate([qin.pop((gi, hh, c)), probs.pop((gi, hh, c))], axis=1)
            rhs = jnp.concatenate([start_state.pop((gi, hh, c)), v_of(h, c)], axis=0)
            o = jnp.dot(lhs, rhs, preferred_element_type=F32)
            rws, cols = slice(c * CHUNK, (c + 1) * CHUNK), slice(h * dv, (h + 1) * dv)
            y = o * lax.rsqrt(jnp.mean(o * o, axis=-1, keepdims=True) + EPS) * gate_ref[rws, cols]
            y_ref[rws, y_col0 + h * dv:y_col0 + (h + 1) * dv] = y.astype(BF16)

    def store_state():
        for gi, hh in head_ids:
            st_ref[gi * hpg + hh] = state[gi, hh]

    return cumsum_block, prep, scores_and_state, outputs, store_state


_MIXERS = (dict(heads=GLA_HEADS, dk=GLA_DK, dv=GLA_DV), dict(heads=HGRN_HEADS, dk=HGRN_DF, dv=HGRN_DV))


def _mixer_kernel(x_ref, wn_ref, wt_ref, wgk_ref, bgk_ref, lbl_ref, nwa_ref, nwb_ref, *rest,
                  n_cast):
    cast_in, rest = rest[:n_cast], rest[n_cast:]
    y_ref, rest = rest[0], rest[1:]
    cast_out, rest = rest[:n_cast], rest[n_cast:]
    (wa_ref, wbraw_ref, wb_ref,
     qa_ref, ka_ref, ga_ref, va_ref, gga_ref, ba_ref, sta_ref,
     qb_ref, kb_ref, gb_ref, vb_ref, ggb_ref, bb_ref, stb_ref) = rest

    @pl.when((pl.program_id(0) == 0) & (pl.program_id(1) == 0))
    def _():
        _transpose_projection_weight(wt_ref, wa_ref, wbraw_ref, wb_ref)

    @pl.when(pl.program_id(1) == 0)
    def _():
        sta_ref[...] = jnp.zeros_like(sta_ref)
        stb_ref[...] = jnp.zeros_like(stb_ref)

    for src, dst in zip(cast_in, cast_out):
        dst[...] = src[...].astype(BF16)

    decays_and_operands, output_gates = _projection_program(
        x_ref, wn_ref, wa_ref, wbraw_ref, wb_ref, wgk_ref, bgk_ref, lbl_ref,
        dict(q=qa_ref, k=ka_ref, g=ga_ref, v=va_ref, gate=gga_ref, gain=nwa_ref),
        dict(q=qb_ref, k=kb_ref, g=gb_ref, v=vb_ref, gate=ggb_ref, gain=nwb_ref))
    programs = [
        _mixer_program(qa_ref, ka_ref, ga_ref, va_ref, gga_ref, y_ref, 0,
                       ba_ref, sta_ref, **_MIXERS[0]),
        _mixer_program(qb_ref, kb_ref, gb_ref, vb_ref, ggb_ref, y_ref, GLA_V,
                       bb_ref, stb_ref, **_MIXERS[1]),
    ]
    rows = x_ref.shape[0]
    nchunk = rows // CHUNK

    decays_and_operands()
    for cumsum_block, _, _, _, _ in programs:
        for r0 in range(0, rows, CUMSUM_ROWS):
            cumsum_block(r0)
    output_gates()
    for c in range(nchunk):
        for _, prep, _, _, _ in programs:
            prep(c)
    for c in range(nchunk):
        for _, _, scores_and_state, outputs, _ in programs:
            scores_and_state(c)
            if c >= 1:
                outputs(c - 1)
    for _, _, _, outputs, store_state in programs:
        outputs(nchunk - 1)
        store_state()


def _mixer_call(x2, wn, w_t, wgk_p, bgk, lbl, nwa, nwb, cast_weights, *, batch):
    m = x2.shape[0]
    tm = MIXER_ROWS
    nt = m // batch // tm
    steps = batch * nt
    row = lambda width: pl.BlockSpec((tm, width), lambda b, i: (b * nt + i, 0))
    full = lambda a: pl.BlockSpec(a.shape, lambda b, i: (0,) * a.ndim)
    resident = lambda a: pl.BlockSpec(a.shape, lambda b, i: (0,) * a.ndim,
                                      pipeline_mode=pl.Buffered(1))
    slab = lambda a: pl.BlockSpec((a.shape[0] // steps, a.shape[1]), lambda b, i: (b * nt + i, 0))
    out_shapes = (jax.ShapeDtypeStruct((m, GLA_V + HGRN_V), BF16),)
    cast_shapes = tuple(jax.ShapeDtypeStruct(w.shape, BF16) for w in cast_weights)
    scratch = [pltpu.VMEM((D_MODEL, GLA_COLS), BF16), pltpu.VMEM((D_MODEL, LANES), BF16),
               pltpu.VMEM((D_MODEL, HGRN_COLS), BF16)]
    for mix in _MIXERS:
        w, gw = mix["heads"] * mix["dk"], max(mix["dk"], LANES)
        wv = mix["heads"] * mix["dv"]
        scratch += [pltpu.VMEM((tm, w), F32), pltpu.VMEM((tm, w), F32), pltpu.VMEM((tm, w), F32),
                    pltpu.VMEM((tm, wv), BF16), pltpu.VMEM((tm, wv), F32),
                    pltpu.VMEM((tm, w), F32), pltpu.VMEM((mix["heads"], mix["dv"], gw), F32)]
    outs = pl.pallas_call(
        functools.partial(_mixer_kernel, n_cast=len(cast_weights)),
        grid=(batch, nt),
        in_specs=[row(D_MODEL), full(wn), resident(w_t), full(wgk_p), full(bgk), full(lbl),
                  full(nwa), full(nwb)] + [slab(w) for w in cast_weights],
        out_specs=tuple(row(s.shape[1]) for s in out_shapes) + tuple(slab(w) for w in cast_weights),
        out_shape=out_shapes + cast_shapes,
        scratch_shapes=scratch,
        compiler_params=pltpu.CompilerParams(
            dimension_semantics=("arbitrary", "arbitrary"), vmem_limit_bytes=VMEM_LIMIT_BYTES),
        name="proj_gla_hgrn_mixers",
    )(x2, wn, w_t, wgk_p, bgk, lbl, nwa, nwb, *cast_weights)
    return outs[0], outs[1:]


def _tail_kernel(x_ref, y_ref, w_out_ref, post_mix_ref, pre_mlp_ref, post_mlp_ref,
                 w_up_ref, w_down_ref, out_ref):
    sub = x_ref.shape[0] // TAIL_SUBTILES
    tiles = [slice(s * sub, (s + 1) * sub) for s in range(TAIL_SUBTILES)]
    hs, us = {}, {}

    def out_proj(s):
        r = tiles[s]
        mix = jnp.dot(y_ref[r, :], w_out_ref[...], preferred_element_type=F32)
        hs[s] = x_ref[r, :] + _rmsnorm(mix, post_mix_ref[...])

    def up(s):
        us[s] = jnp.dot(_rmsnorm(hs[s], pre_mlp_ref[...]).astype(BF16), w_up_ref[...],
                        preferred_element_type=F32)

    def down(s):
        relu = jnp.maximum(us.pop(s), 0.0)
        m = jnp.dot((relu * relu).astype(BF16), w_down_ref[...], preferred_element_type=F32)
        out_ref[tiles[s], :] = hs.pop(s) + _rmsnorm(m, post_mlp_ref[...])

    for stage in (out_proj, up, down):
        for s in range(TAIL_SUBTILES):
            stage(s)


def _tail_call(x2, y, w_out, post_mix, pre_mlp, post_mlp, w_up, w_down):
    m = x2.shape[0]
    tm = TAIL_ROWS
    row = lambda width: pl.BlockSpec((tm, width), lambda i: (i, 0))
    full = lambda a: pl.BlockSpec(a.shape, lambda i: (0,) * a.ndim)
    resident = lambda a: pl.BlockSpec(a.shape, lambda i: (0,) * a.ndim,
                                      pipeline_mode=pl.Buffered(1))
    return pl.pallas_call(
        _tail_kernel,
        grid=(m // tm,),
        in_specs=[row(D_MODEL), row(GLA_V + HGRN_V), resident(w_out), full(post_mix), full(pre_mlp),
                  full(post_mlp), resident(w_up), resident(w_down)],
        out_specs=row(D_MODEL),
        out_shape=jax.ShapeDtypeStruct((m, D_MODEL), F32),
        compiler_params=pltpu.CompilerParams(
            dimension_semantics=("parallel",), vmem_limit_bytes=VMEM_LIMIT_BYTES),
        name="out_proj_mlp",
    )(x2, y, w_out, post_mix, pre_mlp, post_mlp, w_up, w_down)


def kernel(x, w_in, w_gk_up, b_gk, gla_norm_w, hgrn_norm_w, hgrn_lower_bounds, w_out,
           pre_mix_norm, post_mix_norm, pre_mlp_norm, post_mlp_norm, w_up, w_down):
    batch, seq, d = x.shape
    x2 = x.reshape(batch * seq, d)
    l = 0
    w_t = jnp.transpose(w_in[l]).astype(BF16)
    wgk_p = jnp.pad(w_gk_up[l], ((0, LANES - GLA_GATE_RANK), (0, 0))).astype(BF16)
    row2 = lambda a: a.reshape(1, -1)

    y, (w_out_b, w_up_b, w_down_b) = _mixer_call(
        x2, row2(pre_mix_norm[l]), w_t, wgk_p, row2(b_gk[l]), hgrn_lower_bounds,
        row2(gla_norm_w[l]), row2(hgrn_norm_w[l]), (w_out[l], w_up[l], w_down[l]), batch=batch)
    out = _tail_call(x2, y, w_out_b, row2(post_mix_norm[l]), row2(pre_mlp_norm[l]),
                     row2(post_mlp_norm[l]), w_up_b, w_down_b)
    return out.reshape(batch, seq, d)
```

```python
import functools

import jax
import jax.numpy as jnp
from jax import lax
from jax.experimental import pallas as pl
from jax.experimental.pallas import tpu as pltpu

F32 = jnp.float32
BF16 = jnp.bfloat16

D_MODEL = 1024
GLA_HEADS, GLA_DK, GLA_DV = 4, 64, 128
GLA_QK = GLA_HEADS * GLA_DK
GLA_V = GLA_HEADS * GLA_DV
GLA_GATE_RANK = 16
GLA_GATE_NORM = 16.0
HGRN_HEADS, HGRN_DF, HGRN_DV = 4, 128, 128
HGRN_F = HGRN_HEADS * HGRN_DF
HGRN_V = HGRN_HEADS * HGRN_DV
D_FF = 4 * D_MODEL
EPS = 1e-6
LOG2_E = 1.4426950408889634

LANES = 128
VMEM_LIMIT_BYTES = 56 * 1024 * 1024

_C_GQ = 0
_C_GK = _C_GQ + GLA_QK
_C_GV = _C_GK + GLA_QK
_C_GG = _C_GV + GLA_V
GLA_COLS = _C_GG + GLA_V
_C_HQ = 0
_C_HF = _C_HQ + HGRN_F
_C_HI = _C_HF + HGRN_F
_C_HG = _C_HI + HGRN_V
HGRN_COLS = _C_HG + HGRN_V

MIXER_ROWS = 512
TRANSPOSE_ROWS = 512
CUMSUM_ROWS = 256
CHUNK = 64
SUB = 16
NLAG = CHUNK // SUB
LAG_OFFSETS = [sum(CHUNK - SUB * m for m in range(l)) for l in range(NLAG + 1)]
LAG_ROWS = LAG_OFFSETS[NLAG]
TAIL_ROWS = 1024
TAIL_SUBTILES = 2


def _rmsnorm(x, w):
    return x * lax.rsqrt(jnp.mean(x * x, axis=-1, keepdims=True) + EPS) * w


def _sigmoid(x):
    return 1.0 / (1.0 + jnp.exp2(x * (-LOG2_E)))


def _silu(x):
    return x * _sigmoid(x)


def _log_sigmoid(x):
    return -(jnp.maximum(-x, 0.0) + jnp.log(1.0 + jnp.exp2(jnp.abs(x) * (-LOG2_E))))


def _transpose_projection_weight(wt_ref, wa_ref, wbraw_ref, wb_ref):
    for r in range(0, GLA_COLS, TRANSPOSE_ROWS):
        wa_ref[:, r:r + TRANSPOSE_ROWS] = wt_ref[r:r + TRANSPOSE_ROWS, :].T
    wbraw_ref[...] = wt_ref[GLA_COLS:GLA_COLS + LANES, :].T
    hgrn0 = GLA_COLS + GLA_GATE_RANK
    for r in range(0, HGRN_COLS, TRANSPOSE_ROWS):
        wb_ref[:, r:r + TRANSPOSE_ROWS] = wt_ref[hgrn0 + r:hgrn0 + r + TRANSPOSE_ROWS, :].T


def _projection_program(x_ref, wn_ref, wa_ref, wbraw_ref, wb_ref, wgk_ref, bgk_ref, lbl_ref,
                        gla, hgrn):
    x = x_ref[...]
    xg = (x * wn_ref[...]).astype(BF16)
    rstd = lax.rsqrt(jnp.mean(x * x, axis=-1, keepdims=True) + EPS)

    def proj(w_ref, lo, width):
        return rstd * jnp.dot(xg, w_ref[:, lo:lo + width], preferred_element_type=F32)

    def hgrn_decay():
        lbl = lbl_ref[...]
        e = jnp.exp(lbl - jnp.max(lbl, axis=0, keepdims=True))
        lb = e[0:1, :] / jnp.sum(e, axis=0, keepdims=True)
        f = lb + (1.0 - lb) * _sigmoid(proj(wb_ref, _C_HF, HGRN_F))
        hgrn["k"][...] = 1.0 - f
        hgrn["g"][...] = jnp.log(f) * LOG2_E

    def hgrn_q():
        hgrn["q"][...] = _silu(proj(wb_ref, _C_HQ, HGRN_F))

    def hgrn_v():
        hgrn["v"][...] = proj(wb_ref, _C_HI, HGRN_V).astype(BF16)

    def hgrn_gate():
        hgrn["gate"][...] = hgrn["gain"][...] * _silu(proj(wb_ref, _C_HG, HGRN_V))

    low_rank = {}

    def gla_low_rank():
        low_rank["glr"] = proj(wbraw_ref, 0, LANES).astype(BF16)

    def gla_decay():
        gk = jnp.dot(low_rank.pop("glr"), wgk_ref[...], preferred_element_type=F32) + bgk_ref[...]
        gla["g"][...] = _log_sigmoid(gk) * (LOG2_E / GLA_GATE_NORM)

    def gla_q():
        gla["q"][...] = proj(wa_ref, _C_GQ, GLA_QK) * (GLA_DK ** -0.5)

    def gla_k():
        gla["k"][...] = proj(wa_ref, _C_GK, GLA_QK)

    def gla_v():
        gla["v"][...] = proj(wa_ref, _C_GV, GLA_V).astype(BF16)

    def gla_gate():
        gla["gate"][...] = gla["gain"][...] * _silu(proj(wa_ref, _C_GG, GLA_V))

    return ([gla_low_rank, gla_q, gla_k, gla_decay, gla_v, gla_gate],
            [hgrn_decay, hgrn_q, hgrn_v, hgrn_gate])


def _mixer_program(q_ref, k_ref, g_ref, v_ref, gate_ref, y_ref, y_col0, b_ref, st_ref, *,
                   heads, dk, dv):
    gw = max(dk, LANES)
    hpg = gw // dk
    ngroups = heads // hpg

    ri = lax.broadcasted_iota(jnp.int32, (CUMSUM_ROWS, CUMSUM_ROWS), 0)
    ci = lax.broadcasted_iota(jnp.int32, (CUMSUM_ROWS, CUMSUM_ROWS), 1)
    tri = jnp.where((ri // CHUNK == ci // CHUNK) & (ci <= ri), 1.0, 0.0).astype(BF16)

    def cumsum_block(r0):
        blk = slice(r0, r0 + CUMSUM_ROWS)
        g = g_ref[blk, :]
        g_hi = g.astype(BF16)
        g_lo = (g - g_hi.astype(F32)).astype(BF16)
        b_ref[blk, :] = (jnp.dot(tri, g_hi, preferred_element_type=F32)
                         + jnp.dot(tri, g_lo, preferred_element_type=F32))

    si = lax.broadcasted_iota(jnp.int32, (SUB, CHUNK), 0)
    sj = lax.broadcasted_iota(jnp.int32, (SUB, CHUNK), 1)
    col_block = [sj // SUB == r for r in range(NLAG)]
    diag_block = [col_block[r] & (sj - SUB * r <= si) for r in range(NLAG)]
    lane = lax.broadcasted_iota(jnp.int32, (1, gw), 1)

    head_ids = [(gi, hh) for gi in range(ngroups) for hh in range(hpg)]
    v_of = lambda h, c: v_ref[c * CHUNK:(c + 1) * CHUNK, h * dv:(h + 1) * dv]
    state = {(gi, hh): st_ref[gi * hpg + hh] for gi, hh in head_ids}
    probs, qin, start_state, prepared = {}, {}, {}, {}
    blk = lambda a, r: a[SUB * r:SUB * (r + 1), :]

    def prep(c):
        r0 = c * CHUNK
        rws = slice(r0, r0 + CHUNK)
        for gi in range(ngroups):
            lanes = slice(gi * gw, (gi + 1) * gw)
            b = b_ref[rws, lanes]
            q = q_ref[rws, lanes]
            k = k_ref[rws, lanes]
            ends = {r: b_ref[r0 + SUB * r + SUB - 1:r0 + SUB * r + SUB, lanes] for r in range(NLAG)}
            ends[-1] = jnp.zeros((1, gw), F32)
            rows_of = lambda f: jnp.concatenate(
                [jnp.broadcast_to(f(r), (SUB, gw)) for r in range(NLAG)], axis=0)

            q0 = q * jnp.exp2(b - rows_of(lambda r: ends[r]))
            q1 = q * jnp.exp2(b - rows_of(lambda r: ends[r - 1]))
            k_hat = k * jnp.exp2(rows_of(lambda r: ends[r]) - b)
            lag_rows = [q0.astype(BF16), q1[SUB:, :].astype(BF16)]
            for l in range(2, NLAG):
                lag_rows.append(jnp.concatenate(
                    [blk(q1, r) * jnp.exp2(ends[r - 1] - ends[r - l]) for r in range(l, NLAG)],
                    axis=0).astype(BF16))
            q_lag = jnp.concatenate(lag_rows, axis=0)
            q_in = jnp.concatenate(
                [blk(q1, r) * jnp.exp2(ends[r - 1]) for r in range(NLAG)], axis=0)
            k_dec = jnp.concatenate(
                [blk(k_hat, r) * jnp.exp2(ends[NLAG - 1] - ends[r]) for r in range(NLAG)],
                axis=0).astype(BF16)
            dec = jnp.exp2(ends[NLAG - 1])
            prepared[gi, c] = (q_lag, k_dec, dec)
            for hh in range(hpg):
                if hpg == 1:
                    prepared[gi, hh, c], qin[gi, hh, c] = k_hat.astype(BF16), q_in.astype(BF16)
                else:
                    in_head = (lane >= hh * dk) & (lane < (hh + 1) * dk)
                    prepared[gi, hh, c] = jnp.where(in_head, k_hat, 0.0).astype(BF16)
                    qin[gi, hh, c] = jnp.where(in_head, q_in, 0.0).astype(BF16)

    def scores_and_state(c):
        for gi in range(ngroups):
            q_lag, k_dec, dec = prepared.pop((gi, c))
            for hh in range(hpg):
                h = gi * hpg + hh
                k_h = prepared.pop((gi, hh, c))
                s_all = lax.dot_general(q_lag, k_h, (((1,), (1,)), ((), ())),
                                        preferred_element_type=F32)
                p_rows = []
                for r in range(NLAG):
                    p_r = jnp.where(diag_block[r], blk(s_all, r), 0.0)
                    for l in range(1, r + 1):
                        src = LAG_OFFSETS[l] + SUB * (r - l)
                        p_r = jnp.where(col_block[r - l], s_all[src:src + SUB, :], p_r)
                    p_rows.append(p_r)
                probs[gi, hh, c] = jnp.concatenate(p_rows, axis=0).astype(BF16)
                upd = lax.dot_general(v_of(h, c), k_dec, (((0,), (0,)), ((), ())),
                                      preferred_element_type=F32)
                start_state[gi, hh, c] = state[gi, hh].astype(BF16).T
                state[gi, hh] = state[gi, hh] * dec + upd

    def outputs(c):
        for gi, hh in head_ids:
            h = gi * hpg + hh
            lhs = jnp.concatenate([qin.pop((gi, hh, c)), probs.pop((gi, hh, c))], axis=1)
            rhs = jnp.concatenate([start_state.pop((gi, hh, c)), v_of(h, c)], axis=0)
            o = jnp.dot(lhs, rhs, preferred_element_type=F32)
            rws, cols = slice(c * CHUNK, (c + 1) * CHUNK), slice(h * dv, (h + 1) * dv)
            y = o * lax.rsqrt(jnp.mean(o * o, axis=-1, keepdims=True) + EPS) * gate_ref[rws, cols]
            y_ref[rws, y_col0 + h * dv:y_col0 + (h + 1) * dv] = y.astype(BF16)

    def store_state():
        for gi, hh in head_ids:
            st_ref[gi * hpg + hh] = state[gi, hh]

    return cumsum_block, prep, scores_and_state, outputs, store_state


_MIXERS = (dict(heads=GLA_HEADS, dk=GLA_DK, dv=GLA_DV), dict(heads=HGRN_HEADS, dk=HGRN_DF, dv=HGRN_DV))


def _mixer_kernel(x_ref, wn_ref, wt_ref, wgk_ref, bgk_ref, lbl_ref, nwa_ref, nwb_ref, *rest,
                  n_cast, steps_per_sequence):
    cast_in, rest = rest[:n_cast], rest[n_cast:]
    y_ref, rest = rest[0], rest[1:]
    cast_out, rest = rest[:n_cast], rest[n_cast:]
    (wa_ref, wbraw_ref, wb_ref,
     qa_ref, ka_ref, ga_ref, va_ref, gga_ref, ba_ref, sta_ref,
     qb_ref, kb_ref, gb_ref, vb_ref, ggb_ref, bb_ref, stb_ref) = rest

    step = pl.program_id(0)
    mixer_scratch = (qa_ref, ka_ref, ga_ref, va_ref, gga_ref, ba_ref,
                     qb_ref, kb_ref, gb_ref, vb_ref, ggb_ref, bb_ref)

    @pl.when(step == 0)
    def _():
        _transpose_projection_weight(wt_ref, wa_ref, wbraw_ref, wb_ref)
        for ref in mixer_scratch:
            ref[...] = jnp.zeros_like(ref)

    @pl.when((step == 0) | ((step - 1) % steps_per_sequence == 0))
    def _():
        sta_ref[...] = jnp.zeros_like(sta_ref)
        stb_ref[...] = jnp.zeros_like(stb_ref)

    for src, dst in zip(cast_in, cast_out):
        dst[...] = src[...].astype(BF16)

    gla_stages, hgrn_stages = _projection_program(
        x_ref, wn_ref, wa_ref, wbraw_ref, wb_ref, wgk_ref, bgk_ref, lbl_ref,
        dict(q=qa_ref, k=ka_ref, g=ga_ref, v=va_ref, gate=gga_ref, gain=nwa_ref),
        dict(q=qb_ref, k=kb_ref, g=gb_ref, v=vb_ref, gate=ggb_ref, gain=nwb_ref))
    programs = [
        _mixer_program(qa_ref, ka_ref, ga_ref, va_ref, gga_ref, y_ref, 0,
                       ba_ref, sta_ref, **_MIXERS[0]),
        _mixer_program(qb_ref, kb_ref, gb_ref, vb_ref, ggb_ref, y_ref, GLA_V,
                       bb_ref, stb_ref, **_MIXERS[1]),
    ]
    rows = x_ref.shape[0]
    nchunk = rows // CHUNK

    for _, prep, _, _, _ in programs:
        for c in range(nchunk):
            prep(c)
    for c in range(nchunk):
        for _, _, scores_and_state, outputs, _ in programs:
            scores_and_state(c)
            if c >= 1:
                outputs(c - 1)
    for _, _, _, outputs, store_state in programs:
        outputs(nchunk - 1)
        store_state()
    for stage in hgrn_stages[:1] + gla_stages[:4] + hgrn_stages[1:3] + gla_stages[4:] + hgrn_stages[3:]:
        stage()
    for cumsum_block, _, _, _, _ in programs:
        for r0 in range(0, rows, CUMSUM_ROWS):
            cumsum_block(r0)


def _mixer_call(x2, wn, w_t, wgk_p, bgk, lbl, nwa, nwb, cast_weights, *, batch):
    m = x2.shape[0]
    tm = MIXER_ROWS
    nt = m // batch // tm
    tiles = batch * nt
    this_tile = lambda s: jnp.minimum(s, tiles - 1)
    row_in = pl.BlockSpec((tm, D_MODEL), lambda s: (this_tile(s), 0))
    row_out = lambda width: pl.BlockSpec((tm, width), lambda s: (jnp.maximum(s - 1, 0), 0))
    full = lambda a: pl.BlockSpec(a.shape, lambda s: (0,) * a.ndim)
    resident = lambda a: pl.BlockSpec(a.shape, lambda s: (0,) * a.ndim,
                                      pipeline_mode=pl.Buffered(1))
    slab = lambda a: pl.BlockSpec((a.shape[0] // tiles, a.shape[1]), lambda s: (this_tile(s), 0))
    out_shapes = (jax.ShapeDtypeStruct((m, GLA_V + HGRN_V), BF16),)
    cast_shapes = tuple(jax.ShapeDtypeStruct(w.shape, BF16) for w in cast_weights)
    scratch = [pltpu.VMEM((D_MODEL, GLA_COLS), BF16), pltpu.VMEM((D_MODEL, LANES), BF16),
               pltpu.VMEM((D_MODEL, HGRN_COLS), BF16)]
    for mix in _MIXERS:
        w, gw = mix["heads"] * mix["dk"], max(mix["dk"], LANES)
        wv = mix["heads"] * mix["dv"]
        scratch += [pltpu.VMEM((tm, w), F32), pltpu.VMEM((tm, w), F32), pltpu.VMEM((tm, w), F32),
                    pltpu.VMEM((tm, wv), BF16), pltpu.VMEM((tm, wv), F32),
                    pltpu.VMEM((tm, w), F32), pltpu.VMEM((mix["heads"], mix["dv"], gw), F32)]
    outs = pl.pallas_call(
        functools.partial(_mixer_kernel, n_cast=len(cast_weights), steps_per_sequence=nt),
        grid=(tiles + 1,),
        in_specs=[row_in, full(wn), resident(w_t), full(wgk_p), full(bgk), full(lbl),
                  full(nwa), full(nwb)] + [slab(w) for w in cast_weights],
        out_specs=tuple(row_out(s.shape[1]) for s in out_shapes)
                  + tuple(slab(w) for w in cast_weights),
        out_shape=out_shapes + cast_shapes,
        scratch_shapes=scratch,
        compiler_params=pltpu.CompilerParams(
            dimension_semantics=("arbitrary",), vmem_limit_bytes=VMEM_LIMIT_BYTES),
        name="proj_gla_hgrn_mixers",
    )(x2, wn, w_t, wgk_p, bgk, lbl, nwa, nwb, *cast_weights)
    return outs[0], outs[1:]


def _tail_kernel(x_ref, y_ref, w_out_ref, post_mix_ref, pre_mlp_ref, post_mlp_ref,
                 w_up_ref, w_down_ref, out_ref):
    sub = x_ref.shape[0] // TAIL_SUBTILES
    tiles = [slice(s * sub, (s + 1) * sub) for s in range(TAIL_SUBTILES)]
    hs, us = {}, {}

    def out_proj(s):
        r = tiles[s]
        mix = jnp.dot(y_ref[r, :], w_out_ref[...], preferred_element_type=F32)
        hs[s] = x_ref[r, :] + _rmsnorm(mix, post_mix_ref[...])

    def up(s):
        us[s] = jnp.dot(_rmsnorm(hs[s], pre_mlp_ref[...]).astype(BF16), w_up_ref[...],
                        preferred_element_type=F32)

    def down(s):
        relu = jnp.maximum(us.pop(s), 0.0)
        m = jnp.dot((relu * relu).astype(BF16), w_down_ref[...], preferred_element_type=F32)
        out_ref[tiles[s], :] = hs.pop(s) + _rmsnorm(m, post_mlp_ref[...])

    for stage in (out_proj, up, down):
        for s in range(TAIL_SUBTILES):
            stage(s)


def _tail_call(x2, y, w_out, post_mix, pre_mlp, post_mlp, w_up, w_down):
    m = x2.shape[0]
    tm = TAIL_ROWS
    row = lambda width: pl.BlockSpec((tm, width), lambda i: (i, 0))
    full = lambda a: pl.BlockSpec(a.shape, lambda i: (0,) * a.ndim)
    resident = lambda a: pl.BlockSpec(a.shape, lambda i: (0,) * a.ndim,
                                      pipeline_mode=pl.Buffered(1))
    return pl.pallas_call(
        _tail_kernel,
        grid=(m // tm,),
        in_specs=[row(D_MODEL), row(GLA_V + HGRN_V), resident(w_out), full(post_mix), full(pre_mlp),
                  full(post_mlp), resident(w_up), resident(w_down)],
        out_specs=row(D_MODEL),
        out_shape=jax.ShapeDtypeStruct((m, D_MODEL), F32),
        compiler_params=pltpu.CompilerParams(
            dimension_semantics=("parallel",), vmem_limit_bytes=VMEM_LIMIT_BYTES),
        name="out_proj_mlp",
    )(x2, y, w_out, post_mix, pre_mlp, post_mlp, w_up, w_down)


def kernel(x, w_in, w_gk_up, b_gk, gla_norm_w, hgrn_norm_w, hgrn_lower_bounds, w_out,
           pre_mix_norm, post_mix_norm, pre_mlp_norm, post_mlp_norm, w_up, w_down):
    batch, seq, d = x.shape
    x2 = x.reshape(batch * seq, d)
    l = 0
    w_t = jnp.transpose(w_in[l]).astype(BF16)
    wgk_p = jnp.pad(w_gk_up[l], ((0, LANES - GLA_GATE_RANK), (0, 0))).astype(BF16)
    row2 = lambda a: a.reshape(1, -1)

    y, (w_out_b, w_up_b, w_down_b) = _mixer_call(
        x2, row2(pre_mix_norm[l]), w_t, wgk_p, row2(b_gk[l]), hgrn_lower_bounds,
        row2(gla_norm_w[l]), row2(hgrn_norm_w[l]), (w_out[l], w_up[l], w_down[l]), batch=batch)
    out = _tail_call(x2, y, w_out_b, row2(post_mix_norm[l]), row2(pre_mlp_norm[l]),
                     row2(post_mlp_norm[l]), w_up_b, w_down_b)
    return out.reshape(batch, seq, d)
```

```python
import functools

import jax
import jax.numpy as jnp
from jax import lax
from jax.experimental import pallas as pl
from jax.experimental.pallas import tpu as pltpu

F32 = jnp.float32
BF16 = jnp.bfloat16

D_MODEL = 1024
GLA_HEADS, GLA_DK, GLA_DV = 4, 64, 128
GLA_QK = GLA_HEADS * GLA_DK
GLA_V = GLA_HEADS * GLA_DV
GLA_GATE_RANK = 16
GLA_GATE_NORM = 16.0
HGRN_HEADS, HGRN_DF, HGRN_DV = 4, 128, 128
HGRN_F = HGRN_HEADS * HGRN_DF
HGRN_V = HGRN_HEADS * HGRN_DV
D_FF = 4 * D_MODEL
EPS = 1e-6
LOG2_E = 1.4426950408889634

LANES = 128
VMEM_LIMIT_BYTES = 56 * 1024 * 1024

_C_GQ = 0
_C_GK = _C_GQ + GLA_QK
_C_GV = _C_GK + GLA_QK
_C_GG = _C_GV + GLA_V
GLA_COLS = _C_GG + GLA_V
_C_HQ = 0
_C_HF = _C_HQ + HGRN_F
_C_HI = _C_HF + HGRN_F
_C_HG = _C_HI + HGRN_V
HGRN_COLS = _C_HG + HGRN_V

MIXER_ROWS = 512
TRANSPOSE_ROWS = 512
CUMSUM_ROWS = 256
CHUNK = 64
SUB = 16
NLAG = CHUNK // SUB
LAG_OFFSETS = [sum(CHUNK - SUB * m for m in range(l)) for l in range(NLAG + 1)]
LAG_ROWS = LAG_OFFSETS[NLAG]
TAIL_ROWS = 1024
TAIL_SUBTILES = 2


def _rmsnorm(x, w):
    return x * lax.rsqrt(jnp.mean(x * x, axis=-1, keepdims=True) + EPS) * w


def _sigmoid(x):
    return 1.0 / (1.0 + jnp.exp2(x * (-LOG2_E)))


def _silu(x):
    return x * _sigmoid(x)


def _log_sigmoid(x):
    return -(jnp.maximum(-x, 0.0) + jnp.log(1.0 + jnp.exp2(jnp.abs(x) * (-LOG2_E))))


def _transpose_projection_weight(wt_ref, wa_ref, wbraw_ref, wb_ref):
    turn = lambda lo, n: wt_ref[lo:lo + n, :].astype(BF16).T
    for r in range(0, GLA_COLS, TRANSPOSE_ROWS):
        wa_ref[:, r:r + TRANSPOSE_ROWS] = turn(r, TRANSPOSE_ROWS)
    wbraw_ref[...] = turn(GLA_COLS, LANES)
    hgrn0 = GLA_COLS + GLA_GATE_RANK
    for r in range(0, HGRN_COLS, TRANSPOSE_ROWS):
        wb_ref[:, r:r + TRANSPOSE_ROWS] = turn(hgrn0 + r, TRANSPOSE_ROWS)


def _projection_program(x_ref, wn_ref, wa_ref, wbraw_ref, wb_ref, wgk_ref, bgk_ref, lbl_ref,
                        gla, hgrn):
    x = x_ref[...]
    xg = (x * wn_ref[...]).astype(BF16)
    rstd = lax.rsqrt(jnp.mean(x * x, axis=-1, keepdims=True) + EPS)

    def proj(w_ref, lo, width):
        return rstd * jnp.dot(xg, w_ref[:, lo:lo + width], preferred_element_type=F32)

    def decays_and_operands():
        lbl = lbl_ref[...]
        e = jnp.exp(lbl - jnp.max(lbl, axis=0, keepdims=True))
        lb = e[0:1, :] / jnp.sum(e, axis=0, keepdims=True)
        f = lb + (1.0 - lb) * _sigmoid(proj(wb_ref, _C_HF, HGRN_F))
        hgrn["k"][...] = 1.0 - f
        hgrn["g"][...] = jnp.log(f) * LOG2_E
        glr = proj(wbraw_ref, 0, LANES).astype(BF16)
        gla["q"][...] = proj(wa_ref, _C_GQ, GLA_QK) * (GLA_DK ** -0.5)
        gla["k"][...] = proj(wa_ref, _C_GK, GLA_QK)
        gla["v"][...] = proj(wa_ref, _C_GV, GLA_V).astype(BF16)
        gk = jnp.dot(glr, wgk_ref[...], preferred_element_type=F32) + bgk_ref[...]
        gla["g"][...] = _log_sigmoid(gk) * (LOG2_E / GLA_GATE_NORM)
        hgrn["q"][...] = _silu(proj(wb_ref, _C_HQ, HGRN_F))
        hgrn["v"][...] = proj(wb_ref, _C_HI, HGRN_V).astype(BF16)

    def output_gates():
        gla["gate"][...] = gla["gain"][...] * _silu(proj(wa_ref, _C_GG, GLA_V))
        hgrn["gate"][...] = hgrn["gain"][...] * _silu(proj(wb_ref, _C_HG, HGRN_V))

    return decays_and_operands, output_gates


def _mixer_program(q_ref, k_ref, g_ref, v_ref, gate_ref, y_ref, y_col0, b_ref, st_ref, *,
                   heads, dk, dv):
    gw = max(dk, LANES)
    hpg = gw // dk
    ngroups = heads // hpg

    ri = lax.broadcasted_iota(jnp.int32, (CUMSUM_ROWS, CUMSUM_ROWS), 0)
    ci = lax.broadcasted_iota(jnp.int32, (CUMSUM_ROWS, CUMSUM_ROWS), 1)
    tri = jnp.where((ri // CHUNK == ci // CHUNK) & (ci <= ri), 1.0, 0.0).astype(BF16)

    def cumsum_block(r0):
        blk = slice(r0, r0 + CUMSUM_ROWS)
        g = g_ref[blk, :]
        g_hi = g.astype(BF16)
        g_lo = (g - g_hi.astype(F32)).astype(BF16)
        b_ref[blk, :] = (jnp.dot(tri, g_hi, preferred_element_type=F32)
                         + jnp.dot(tri, g_lo, preferred_element_type=F32))

    si = lax.broadcasted_iota(jnp.int32, (SUB, CHUNK), 0)
    sj = lax.broadcasted_iota(jnp.int32, (SUB, CHUNK), 1)
    col_block = [sj // SUB == r for r in range(NLAG)]
    diag_block = [col_block[r] & (sj - SUB * r <= si) for r in range(NLAG)]
    lane = lax.broadcasted_iota(jnp.int32, (1, gw), 1)

    head_ids = [(gi, hh) for gi in range(ngroups) for hh in range(hpg)]
    v_of = lambda h, c: v_ref[c * CHUNK:(c + 1) * CHUNK, h * dv:(h + 1) * dv]
    state = {(gi, hh): st_ref[gi * hpg + hh] for gi, hh in head_ids}
    probs, qin, start_state, prepared = {}, {}, {}, {}
    blk = lambda a, r: a[SUB * r:SUB * (r + 1), :]

    def prep(c):
        r0 = c * CHUNK
        rws = slice(r0, r0 + CHUNK)
        for gi in range(ngroups):
            lanes = slice(gi * gw, (gi + 1) * gw)
            b = b_ref[rws, lanes]
            q = q_ref[rws, lanes]
            k = k_ref[rws, lanes]
            ends = {r: b_ref[r0 + SUB * r + SUB - 1:r0 + SUB * r + SUB, lanes] for r in range(NLAG)}
            ends[-1] = jnp.zeros((1, gw), F32)
            rows_of = lambda f: jnp.concatenate(
                [jnp.broadcast_to(f(r), (SUB, gw)) for r in range(NLAG)], axis=0)

            q0 = q * jnp.exp2(b - rows_of(lambda r: ends[r]))
            q1 = q * jnp.exp2(b - rows_of(lambda r: ends[r - 1]))
            k_hat = k * jnp.exp2(rows_of(lambda r: ends[r]) - b)
            lag_rows = [q0.astype(BF16), q1[SUB:, :].astype(BF16)]
            for l in range(2, NLAG):
                lag_rows.append(jnp.concatenate(
                    [blk(q1, r) * jnp.exp2(ends[r - 1] - ends[r - l]) for r in range(l, NLAG)],
                    axis=0).astype(BF16))
            q_lag = jnp.concatenate(lag_rows, axis=0)
            q_in = jnp.concatenate(
                [blk(q1, r) * jnp.exp2(ends[r - 1]) for r in range(NLAG)], axis=0)
            k_dec = jnp.concatenate(
                [blk(k_hat, r) * jnp.exp2(ends[NLAG - 1] - ends[r]) for r in range(NLAG)],
                axis=0).astype(BF16)
            dec = jnp.exp2(ends[NLAG - 1])
            prepared[gi, c] = (q_lag, k_dec, dec)
            for hh in range(hpg):
                if hpg == 1:
                    prepared[gi, hh, c], qin[gi, hh, c] = k_hat.astype(BF16), q_in.astype(BF16)
                else:
                    in_head = (lane >= hh * dk) & (lane < (hh + 1) * dk)
                    prepared[gi, hh, c] = jnp.where(in_head, k_hat, 0.0).astype(BF16)
                    qin[gi, hh, c] = jnp.where(in_head, q_in, 0.0).astype(BF16)

    def scores_and_state(c):
        for gi in range(ngroups):
            q_lag, k_dec, dec = prepared.pop((gi, c))
            for hh in range(hpg):
                h = gi * hpg + hh
                k_h = prepared.pop((gi, hh, c))
                s_all = lax.dot_general(q_lag, k_h, (((1,), (1,)), ((), ())),
                                        preferred_element_type=F32)
                p_rows = []
                for r in range(NLAG):
                    p_r = jnp.where(diag_block[r], blk(s_all, r), 0.0)
                    for l in range(1, r + 1):
                        src = LAG_OFFSETS[l] + SUB * (r - l)
                        p_r = jnp.where(col_block[r - l], s_all[src:src + SUB, :], p_r)
                    p_rows.append(p_r)
                probs[gi, hh, c] = jnp.concatenate(p_rows, axis=0).astype(BF16)
                upd = lax.dot_general(v_of(h, c), k_dec, (((0,), (0,)), ((), ())),
                                      preferred_element_type=F32)
                start_state[gi, hh, c] = state[gi, hh].astype(BF16).T
                state[gi, hh] = state[gi, hh] * dec + upd

    def outputs(c):
        for gi, hh in head_ids:
            h = gi * hpg + hh
            lhs = jnp.concatenate([qin.pop((gi, hh, c)), probs.pop((gi, hh, c))], axis=1)
            rhs = jnp.concatenate([start_state.pop((gi, hh, c)), v_of(h, c)], axis=0)
            o = jnp.dot(lhs, rhs, preferred_element_type=F32)
            rws, cols = slice(c * CHUNK, (c + 1) * CHUNK), slice(h * dv, (h + 1) * dv)
            y = o * lax.rsqrt(jnp.mean(o * o, axis=-1, keepdims=True) + EPS) * gate_ref[rws, cols]
            y_ref[rws, y_col0 + h * dv:y_col0 + (h + 1) * dv] = y.astype(BF16)

    def store_state():
        for gi, hh in head_ids:
            st_ref[gi * hpg + hh] = state[gi, hh]

    return cumsum_block, prep, scores_and_state, outputs, store_state


_MIXERS = (dict(heads=GLA_HEADS, dk=GLA_DK, dv=GLA_DV), dict(heads=HGRN_HEADS, dk=HGRN_DF, dv=HGRN_DV))


def _mixer_kernel(x_ref, wn_ref, wt_ref, wgk_ref, bgk_ref, lbl_ref, nwa_ref, nwb_ref, *rest,
                  n_cast):
    cast_in, rest = rest[:n_cast], rest[n_cast:]
    y_ref, rest = rest[0], rest[1:]
    cast_out, rest = rest[:n_cast], rest[n_cast:]
    (wa_ref, wbraw_ref, wb_ref,
     qa_ref, ka_ref, ga_ref, va_ref, gga_ref, ba_ref, sta_ref,
     qb_ref, kb_ref, gb_ref, vb_ref, ggb_ref, bb_ref, stb_ref) = rest

    @pl.when((pl.program_id(0) == 0) & (pl.program_id(1) == 0))
    def _():
        _transpose_projection_weight(wt_ref, wa_ref, wbraw_ref, wb_ref)

    @pl.when(pl.program_id(1) == 0)
    def _():
        sta_ref[...] = jnp.zeros_like(sta_ref)
        stb_ref[...] = jnp.zeros_like(stb_ref)

    for src, dst in zip(cast_in, cast_out):
        dst[...] = src[...].astype(BF16)

    decays_and_operands, output_gates = _projection_program(
        x_ref, wn_ref, wa_ref, wbraw_ref, wb_ref, wgk_ref, bgk_ref, lbl_ref,
        dict(q=qa_ref, k=ka_ref, g=ga_ref, v=va_ref, gate=gga_ref, gain=nwa_ref),
        dict(q=qb_ref, k=kb_ref, g=gb_ref, v=vb_ref, gate=ggb_ref, gain=nwb_ref))
    programs = [
        _mixer_program(qa_ref, ka_ref, ga_ref, va_ref, gga_ref, y_ref, 0,
                       ba_ref, sta_ref, **_MIXERS[0]),
        _mixer_program(qb_ref, kb_ref, gb_ref, vb_ref, ggb_ref, y_ref, GLA_V,
                       bb_ref, stb_ref, **_MIXERS[1]),
    ]
    rows = x_ref.shape[0]
    nchunk = rows // CHUNK

    decays_and_operands()
    for cumsum_block, _, _, _, _ in programs:
        for r0 in range(0, rows, CUMSUM_ROWS):
            cumsum_block(r0)
    output_gates()
    for c in range(nchunk):
        for _, prep, _, _, _ in programs:
            prep(c)
    for c in range(nchunk):
        for _, _, scores_and_state, outputs, _ in programs:
            scores_and_state(c)
            if c >= 1:
                outputs(c - 1)
    for _, _, _, outputs, store_state in programs:
        outputs(nchunk - 1)
        store_state()


def _mixer_call(x2, wn, w_t, wgk_p, bgk, lbl, nwa, nwb, cast_weights, *, batch):
    m = x2.shape[0]
    tm = MIXER_ROWS
    nt = m // batch // tm
    steps = batch * nt
    row = lambda width: pl.BlockSpec((tm, width), lambda b, i: (b * nt + i, 0))
    full = lambda a: pl.BlockSpec(a.shape, lambda b, i: (0,) * a.ndim)
    resident = lambda a: pl.BlockSpec(a.shape, lambda b, i: (0,) * a.ndim,
                                      pipeline_mode=pl.Buffered(1))
    slab = lambda a: pl.BlockSpec((a.shape[0] // steps, a.shape[1]), lambda b, i: (b * nt + i, 0))
    out_shapes = (jax.ShapeDtypeStruct((m, GLA_V + HGRN_V), BF16),)
    cast_shapes = tuple(jax.ShapeDtypeStruct(w.shape, BF16) for w in cast_weights)
    scratch = [pltpu.VMEM((D_MODEL, GLA_COLS), BF16), pltpu.VMEM((D_MODEL, LANES), BF16),
               pltpu.VMEM((D_MODEL, HGRN_COLS), BF16)]
    for mix in _MIXERS:
        w, gw = mix["heads"] * mix["dk"], max(mix["dk"], LANES)
        wv = mix["heads"] * mix["dv"]
        scratch += [pltpu.VMEM((tm, w), F32), pltpu.VMEM((tm, w), F32), pltpu.VMEM((tm, w), F32),
                    pltpu.VMEM((tm, wv), BF16), pltpu.VMEM((tm, wv), F32),
                    pltpu.VMEM((tm, w), F32), pltpu.VMEM((mix["heads"], mix["dv"], gw), F32)]
    outs = pl.pallas_call(
        functools.partial(_mixer_kernel, n_cast=len(cast_weights)),
        grid=(batch, nt),
        in_specs=[row(D_MODEL), full(wn), resident(w_t), full(wgk_p), full(bgk), full(lbl),
                  full(nwa), full(nwb)] + [slab(w) for w in cast_weights],
        out_specs=tuple(row(s.shape[1]) for s in out_shapes) + tuple(slab(w) for w in cast_weights),
        out_shape=out_shapes + cast_shapes,
        scratch_shapes=scratch,
        compiler_params=pltpu.CompilerParams(
            dimension_semantics=("arbitrary", "arbitrary"), vmem_limit_bytes=VMEM_LIMIT_BYTES),
        name="proj_gla_hgrn_mixers",
    )(x2, wn, w_t, wgk_p, bgk, lbl, nwa, nwb, *cast_weights)
    return outs[0], outs[1:]


def _tail_kernel(x_ref, y_ref, w_out_ref, post_mix_ref, pre_mlp_ref, post_mlp_ref,
                 w_up_ref, w_down_ref, out_ref):
    sub = x_ref.shape[0] // TAIL_SUBTILES
    tiles = [slice(s * sub, (s + 1) * sub) for s in range(TAIL_SUBTILES)]
    hs, us = {}, {}

    def out_proj(s):
        r = tiles[s]
        mix = jnp.dot(y_ref[r, :], w_out_ref[...], preferred_element_type=F32)
        hs[s] = x_ref[r, :] + _rmsnorm(mix, post_mix_ref[...])

    def up(s):
        us[s] = jnp.dot(_rmsnorm(hs[s], pre_mlp_ref[...]).astype(BF16), w_up_ref[...],
                        preferred_element_type=F32)

    def down(s):
        relu = jnp.maximum(us.pop(s), 0.0)
        m = jnp.dot((relu * relu).astype(BF16), w_down_ref[...], preferred_element_type=F32)
        out_ref[tiles[s], :] = hs.pop(s) + _rmsnorm(m, post_mlp_ref[...])

    for stage in (out_proj, up, down):
        for s in range(TAIL_SUBTILES):
            stage(s)


def _tail_call(x2, y, w_out, post_mix, pre_mlp, post_mlp, w_up, w_down):
    m = x2.shape[0]
    tm = TAIL_ROWS
    row = lambda width: pl.BlockSpec((tm, width), lambda i: (i, 0))
    full = lambda a: pl.BlockSpec(a.shape, lambda i: (0,) * a.ndim)
    resident = lambda a: pl.BlockSpec(a.shape, lambda i: (0,) * a.ndim,
                                      pipeline_mode=pl.Buffered(1))
    return pl.pallas_call(
        _tail_kernel,
        grid=(m // tm,),
        in_specs=[row(D_MODEL), row(GLA_V + HGRN_V), resident(w_out), full(post_mix), full(pre_mlp),
                  full(post_mlp), resident(w_up), resident(w_down)],
        out_specs=row(D_MODEL),
        out_shape=jax.ShapeDtypeStruct((m, D_MODEL), F32),
        compiler_params=pltpu.CompilerParams(
            dimension_semantics=("parallel",), vmem_limit_bytes=VMEM_LIMIT_BYTES),
        name="out_proj_mlp",
    )(x2, y, w_out, post_mix, pre_mlp, post_mlp, w_up, w_down)


def kernel(x, w_in, w_gk_up, b_gk, gla_norm_w, hgrn_norm_w, hgrn_lower_bounds, w_out,
           pre_mix_norm, post_mix_norm, pre_mlp_norm, post_mlp_norm, w_up, w_down):
    batch, seq, d = x.shape
    x2 = x.reshape(batch * seq, d)
    l = 0
    w_t = jnp.transpose(w_in[l])
    wgk_p = jnp.pad(w_gk_up[l], ((0, LANES - GLA_GATE_RANK), (0, 0))).astype(BF16)
    row2 = lambda a: a.reshape(1, -1)

    y, (w_out_b, w_up_b, w_down_b) = _mixer_call(
        x2, row2(pre_mix_norm[l]), w_t, wgk_p, row2(b_gk[l]), hgrn_lower_bounds,
        row2(gla_norm_w[l]), row2(hgrn_norm_w[l]), (w_out[l], w_up[l], w_down[l]), batch=batch)
    out = _tail_call(x2, y, w_out_b, row2(post_mix_norm[l]), row2(pre_mlp_norm[l]),
                     row2(post_mlp_norm[l]), w_up_b, w_down_b)
    return out.reshape(batch, seq, d)
```

```python
import functools

import jax
import jax.numpy as jnp
from jax import lax
from jax.experimental import pallas as pl
from jax.experimental.pallas import tpu as pltpu

F32 = jnp.float32
BF16 = jnp.bfloat16

D_MODEL = 1024
GLA_HEADS, GLA_DK, GLA_DV = 4, 64, 128
GLA_QK = GLA_HEADS * GLA_DK
GLA_V = GLA_HEADS * GLA_DV
GLA_GATE_RANK = 16
GLA_GATE_NORM = 16.0
HGRN_HEADS, HGRN_DF, HGRN_DV = 4, 128, 128
HGRN_F = HGRN_HEADS * HGRN_DF
HGRN_V = HGRN_HEADS * HGRN_DV
D_FF = 4 * D_MODEL
EPS = 1e-6
LOG2_E = 1.4426950408889634

LANES = 128
VMEM_LIMIT_BYTES = 56 * 1024 * 1024

_C_GQ = 0
_C_GK = _C_GQ + GLA_QK
_C_GV = _C_GK + GLA_QK
_C_GG = _C_GV + GLA_V
GLA_COLS = _C_GG + GLA_V
_C_HQ = 0
_C_HF = _C_HQ + HGRN_F
_C_HI = _C_HF + HGRN_F
_C_HG = _C_HI + HGRN_V
HGRN_COLS = _C_HG + HGRN_V

MIXER_ROWS = 512
TRANSPOSE_ROWS = 512
CUMSUM_ROWS = 256
CHUNK = 64
SUB = 16
NLAG = CHUNK // SUB
LAG_OFFSETS = [sum(CHUNK - SUB * m for m in range(l)) for l in range(NLAG + 1)]
LAG_ROWS = LAG_OFFSETS[NLAG]
TAIL_ROWS = 1024
TAIL_SUBTILES = 2


def _rmsnorm(x, w):
    return x * lax.rsqrt(jnp.mean(x * x, axis=-1, keepdims=True) + EPS) * w


def _sigmoid(x):
    return 1.0 / (1.0 + jnp.exp2(x * (-LOG2_E)))


def _silu(x):
    return x * _sigmoid(x)


def _log_sigmoid(x):
    return -(jnp.maximum(-x, 0.0) + jnp.log(1.0 + jnp.exp2(jnp.abs(x) * (-LOG2_E))))


def _transpose_projection_weight(wt_ref, wa_ref, wbraw_ref, wb_ref):
    turn = lambda lo, n: wt_ref[lo:lo + n, :].astype(BF16).T
    for r in range(0, GLA_COLS, TRANSPOSE_ROWS):
        wa_ref[:, r:r + TRANSPOSE_ROWS] = turn(r, TRANSPOSE_ROWS)
    wbraw_ref[...] = turn(GLA_COLS, LANES)
    hgrn0 = GLA_COLS + GLA_GATE_RANK
    for r in range(0, HGRN_COLS, TRANSPOSE_ROWS):
        wb_ref[:, r:r + TRANSPOSE_ROWS] = turn(hgrn0 + r, TRANSPOSE_ROWS)


def _projection_program(x_ref, wn_ref, wa_ref, wbraw_ref, wb_ref, wgk_ref, bgk_ref, lbl_ref,
                        gla, hgrn):
    x = x_ref[...]
    xg = (x * wn_ref[...]).astype(BF16)
    rstd = lax.rsqrt(jnp.mean(x * x, axis=-1, keepdims=True) + EPS)

    def proj(w_ref, lo, width):
        return rstd * jnp.dot(xg, w_ref[:, lo:lo + width], preferred_element_type=F32)

    def decays_and_operands():
        lbl = lbl_ref[...]
        e = jnp.exp(lbl - jnp.max(lbl, axis=0, keepdims=True))
        lb = e[0:1, :] / jnp.sum(e, axis=0, keepdims=True)
        f = lb + (1.0 - lb) * _sigmoid(proj(wb_ref, _C_HF, HGRN_F))
        hgrn["k"][...] = 1.0 - f
        hgrn["g"][...] = jnp.log(f) * LOG2_E
        glr = proj(wbraw_ref, 0, LANES)[:, :GLA_GATE_RANK].astype(BF16)
        gla["q"][...] = proj(wa_ref, _C_GQ, GLA_QK) * (GLA_DK ** -0.5)
        gla["k"][...] = proj(wa_ref, _C_GK, GLA_QK)
        gla["v"][...] = proj(wa_ref, _C_GV, GLA_V).astype(BF16)
        gk = jnp.dot(glr, wgk_ref[...].astype(BF16), preferred_element_type=F32) + bgk_ref[...]
        gla["g"][...] = _log_sigmoid(gk) * (LOG2_E / GLA_GATE_NORM)
        hgrn["q"][...] = _silu(proj(wb_ref, _C_HQ, HGRN_F))
        hgrn["v"][...] = proj(wb_ref, _C_HI, HGRN_V).astype(BF16)

    def output_gates():
        gla["gate"][...] = gla["gain"][...] * _silu(proj(wa_ref, _C_GG, GLA_V))
        hgrn["gate"][...] = hgrn["gain"][...] * _silu(proj(wb_ref, _C_HG, HGRN_V))

    return decays_and_operands, output_gates


def _mixer_program(q_ref, k_ref, g_ref, v_ref, gate_ref, y_ref, y_col0, b_ref, st_ref, *,
                   heads, dk, dv):
    gw = max(dk, LANES)
    hpg = gw // dk
    ngroups = heads // hpg

    ri = lax.broadcasted_iota(jnp.int32, (CUMSUM_ROWS, CUMSUM_ROWS), 0)
    ci = lax.broadcasted_iota(jnp.int32, (CUMSUM_ROWS, CUMSUM_ROWS), 1)
    tri = jnp.where((ri // CHUNK == ci // CHUNK) & (ci <= ri), 1.0, 0.0).astype(BF16)

    def cumsum_block(r0):
        blk = slice(r0, r0 + CUMSUM_ROWS)
        g = g_ref[blk, :]
        g_hi = g.astype(BF16)
        g_lo = (g - g_hi.astype(F32)).astype(BF16)
        b_ref[blk, :] = (jnp.dot(tri, g_hi, preferred_element_type=F32)
                         + jnp.dot(tri, g_lo, preferred_element_type=F32))

    si = lax.broadcasted_iota(jnp.int32, (SUB, CHUNK), 0)
    sj = lax.broadcasted_iota(jnp.int32, (SUB, CHUNK), 1)
    col_block = [sj // SUB == r for r in range(NLAG)]
    diag_block = [col_block[r] & (sj - SUB * r <= si) for r in range(NLAG)]
    lane = lax.broadcasted_iota(jnp.int32, (1, gw), 1)

    head_ids = [(gi, hh) for gi in range(ngroups) for hh in range(hpg)]
    v_of = lambda h, c: v_ref[c * CHUNK:(c + 1) * CHUNK, h * dv:(h + 1) * dv]
    state = {(gi, hh): st_ref[gi * hpg + hh] for gi, hh in head_ids}
    probs, qin, start_state, prepared = {}, {}, {}, {}
    blk = lambda a, r: a[SUB * r:SUB * (r + 1), :]

    def prep(c):
        r0 = c * CHUNK
        rws = slice(r0, r0 + CHUNK)
        for gi in range(ngroups):
            lanes = slice(gi * gw, (gi + 1) * gw)
            b = b_ref[rws, lanes]
            q = q_ref[rws, lanes]
            k = k_ref[rws, lanes]
            ends = {r: b_ref[r0 + SUB * r + SUB - 1:r0 + SUB * r + SUB, lanes] for r in range(NLAG)}
            ends[-1] = jnp.zeros((1, gw), F32)
            rows_of = lambda f: jnp.concatenate(
                [jnp.broadcast_to(f(r), (SUB, gw)) for r in range(NLAG)], axis=0)

            q0 = q * jnp.exp2(b - rows_of(lambda r: ends[r]))
            q1 = q * jnp.exp2(b - rows_of(lambda r: ends[r - 1]))
            k_hat = k * jnp.exp2(rows_of(lambda r: ends[r]) - b)
            lag_rows = [q0.astype(BF16), q1[SUB:, :].astype(BF16)]
            for l in range(2, NLAG):
                lag_rows.append(jnp.concatenate(
                    [blk(q1, r) * jnp.exp2(ends[r - 1] - ends[r - l]) for r in range(l, NLAG)],
                    axis=0).astype(BF16))
            q_lag = jnp.concatenate(lag_rows, axis=0)
            q_in = jnp.concatenate(
                [blk(q1, r) * jnp.exp2(ends[r - 1]) for r in range(NLAG)], axis=0)
            k_dec = jnp.concatenate(
                [blk(k_hat, r) * jnp.exp2(ends[NLAG - 1] - ends[r]) for r in range(NLAG)],
                axis=0).astype(BF16)
            dec = jnp.exp2(ends[NLAG - 1])
            prepared[gi, c] = (q_lag, k_dec, dec)
            for hh in range(hpg):
                if hpg == 1:
                    prepared[gi, hh, c], qin[gi, hh, c] = k_hat.astype(BF16), q_in.astype(BF16)
                else:
                    in_head = (lane >= hh * dk) & (lane < (hh + 1) * dk)
                    prepared[gi, hh, c] = jnp.where(in_head, k_hat, 0.0).astype(BF16)
                    qin[gi, hh, c] = jnp.where(in_head, q_in, 0.0).astype(BF16)

    def scores_and_state(c):
        for gi in range(ngroups):
            q_lag, k_dec, dec = prepared.pop((gi, c))
            for hh in range(hpg):
                h = gi * hpg + hh
                k_h = prepared.pop((gi, hh, c))
                s_all = lax.dot_general(q_lag, k_h, (((1,), (1,)), ((), ())),
                                        preferred_element_type=F32)
                p_rows = []
                for r in range(NLAG):
                    p_r = jnp.where(diag_block[r], blk(s_all, r), 0.0)
                    for l in range(1, r + 1):
                        src = LAG_OFFSETS[l] + SUB * (r - l)
                        p_r = jnp.where(col_block[r - l], s_all[src:src + SUB, :], p_r)
                    p_rows.append(p_r)
                probs[gi, hh, c] = jnp.concatenate(p_rows, axis=0).astype(BF16)
                upd = lax.dot_general(v_of(h, c), k_dec, (((0,), (0,)), ((), ())),
                                      preferred_element_type=F32)
                start_state[gi, hh, c] = state[gi, hh].astype(BF16).T
                state[gi, hh] = state[gi, hh] * dec + upd

    def outputs(c):
        for gi, hh in head_ids:
            h = gi * hpg + hh
            lhs = jnp.concatenate([qin.pop((gi, hh, c)), probs.pop((gi, hh, c))], axis=1)
            rhs = jnp.concatenate([start_state.pop((gi, hh, c)), v_of(h, c)], axis=0)
            o = jnp.dot(lhs, rhs, preferred_element_type=F32)
            rws, cols = slice(c * CHUNK, (c + 1) * CHUNK), slice(h * dv, (h + 1) * dv)
            y = o * lax.rsqrt(jnp.mean(o * o, axis=-1, keepdims=True) + EPS) * gate_ref[rws, cols]
            y_ref[rws, y_col0 + h * dv:y_col0 + (h + 1) * dv] = y.astype(BF16)

    def store_state():
        for gi, hh in head_ids:
            st_ref[gi * hpg + hh] = state[gi, hh]

    return cumsum_block, prep, scores_and_state, outputs, store_state


_MIXERS = (dict(heads=GLA_HEADS, dk=GLA_DK, dv=GLA_DV), dict(heads=HGRN_HEADS, dk=HGRN_DF, dv=HGRN_DV))


def _mixer_kernel(x_ref, wn_ref, wt_ref, wgk_ref, bgk_ref, lbl_ref, nwa_ref, nwb_ref, *rest,
                  n_cast):
    cast_in, rest = rest[:n_cast], rest[n_cast:]
    y_ref, rest = rest[0], rest[1:]
    cast_out, rest = rest[:n_cast], rest[n_cast:]
    (wa_ref, wbraw_ref, wb_ref,
     qa_ref, ka_ref, ga_ref, va_ref, gga_ref, ba_ref, sta_ref,
     qb_ref, kb_ref, gb_ref, vb_ref, ggb_ref, bb_ref, stb_ref) = rest

    @pl.when((pl.program_id(0) == 0) & (pl.program_id(1) == 0))
    def _():
        _transpose_projection_weight(wt_ref, wa_ref, wbraw_ref, wb_ref)

    @pl.when(pl.program_id(1) == 0)
    def _():
        sta_ref[...] = jnp.zeros_like(sta_ref)
        stb_ref[...] = jnp.zeros_like(stb_ref)

    for src, dst in zip(cast_in, cast_out):
        dst[...] = src[...].astype(BF16)

    decays_and_operands, output_gates = _projection_program(
        x_ref, wn_ref, wa_ref, wbraw_ref, wb_ref, wgk_ref, bgk_ref, lbl_ref,
        dict(q=qa_ref, k=ka_ref, g=ga_ref, v=va_ref, gate=gga_ref, gain=nwa_ref),
        dict(q=qb_ref, k=kb_ref, g=gb_ref, v=vb_ref, gate=ggb_ref, gain=nwb_ref))
    programs = [
        _mixer_program(qa_ref, ka_ref, ga_ref, va_ref, gga_ref, y_ref, 0,
                       ba_ref, sta_ref, **_MIXERS[0]),
        _mixer_program(qb_ref, kb_ref, gb_ref, vb_ref, ggb_ref, y_ref, GLA_V,
                       bb_ref, stb_ref, **_MIXERS[1]),
    ]
    rows = x_ref.shape[0]
    nchunk = rows // CHUNK

    decays_and_operands()
    for cumsum_block, _, _, _, _ in programs:
        for r0 in range(0, rows, CUMSUM_ROWS):
            cumsum_block(r0)
    output_gates()
    for c in range(nchunk):
        for _, prep, _, _, _ in programs:
            prep(c)
    for c in range(nchunk):
        for _, _, scores_and_state, outputs, _ in programs:
            scores_and_state(c)
            if c >= 1:
                outputs(c - 1)
    for _, _, _, outputs, store_state in programs:
        outputs(nchunk - 1)
        store_state()


def _mixer_call(x2, wn, w_t, w_gk, bgk, lbl, nwa, nwb, cast_weights, *, batch):
    m = x2.shape[0]
    tm = MIXER_ROWS
    nt = m // batch // tm
    steps = batch * nt
    row = lambda width: pl.BlockSpec((tm, width), lambda b, i: (b * nt + i, 0))
    full = lambda a: pl.BlockSpec(a.shape, lambda b, i: (0,) * a.ndim)
    resident = lambda a: pl.BlockSpec(a.shape, lambda b, i: (0,) * a.ndim,
                                      pipeline_mode=pl.Buffered(1))
    slab = lambda a: pl.BlockSpec((a.shape[0] // steps, a.shape[1]), lambda b, i: (b * nt + i, 0))
    out_shapes = (jax.ShapeDtypeStruct((m, GLA_V + HGRN_V), BF16),)
    cast_shapes = tuple(jax.ShapeDtypeStruct(w.shape, BF16) for w in cast_weights)
    scratch = [pltpu.VMEM((D_MODEL, GLA_COLS), BF16), pltpu.VMEM((D_MODEL, LANES), BF16),
               pltpu.VMEM((D_MODEL, HGRN_COLS), BF16)]
    for mix in _MIXERS:
        w, gw = mix["heads"] * mix["dk"], max(mix["dk"], LANES)
        wv = mix["heads"] * mix["dv"]
        scratch += [pltpu.VMEM((tm, w), F32), pltpu.VMEM((tm, w), F32), pltpu.VMEM((tm, w), F32),
                    pltpu.VMEM((tm, wv), BF16), pltpu.VMEM((tm, wv), F32),
                    pltpu.VMEM((tm, w), F32), pltpu.VMEM((mix["heads"], mix["dv"], gw), F32)]
    outs = pl.pallas_call(
        functools.partial(_mixer_kernel, n_cast=len(cast_weights)),
        grid=(batch, nt),
        in_specs=[row(D_MODEL), full(wn), resident(w_t), full(w_gk), full(bgk), full(lbl),
                  full(nwa), full(nwb)] + [slab(w) for w in cast_weights],
        out_specs=tuple(row(s.shape[1]) for s in out_shapes) + tuple(slab(w) for w in cast_weights),
        out_shape=out_shapes + cast_shapes,
        scratch_shapes=scratch,
        compiler_params=pltpu.CompilerParams(
            dimension_semantics=("arbitrary", "arbitrary"), vmem_limit_bytes=VMEM_LIMIT_BYTES),
        name="proj_gla_hgrn_mixers",
    )(x2, wn, w_t, w_gk, bgk, lbl, nwa, nwb, *cast_weights)
    return outs[0], outs[1:]


def _tail_kernel(x_ref, y_ref, w_out_ref, post_mix_ref, pre_mlp_ref, post_mlp_ref,
                 w_up_ref, w_down_ref, out_ref):
    sub = x_ref.shape[0] // TAIL_SUBTILES
    tiles = [slice(s * sub, (s + 1) * sub) for s in range(TAIL_SUBTILES)]
    hs, us = {}, {}

    def out_proj(s):
        r = tiles[s]
        mix = jnp.dot(y_ref[r, :], w_out_ref[...], preferred_element_type=F32)
        hs[s] = x_ref[r, :] + _rmsnorm(mix, post_mix_ref[...])

    def up(s):
        us[s] = jnp.dot(_rmsnorm(hs[s], pre_mlp_ref[...]).astype(BF16), w_up_ref[...],
                        preferred_element_type=F32)

    def down(s):
        relu = jnp.maximum(us.pop(s), 0.0)
        m = jnp.dot((relu * relu).astype(BF16), w_down_ref[...], preferred_element_type=F32)
        out_ref[tiles[s], :] = hs.pop(s) + _rmsnorm(m, post_mlp_ref[...])

    for stage in (out_proj, up, down):
        for s in range(TAIL_SUBTILES):
            stage(s)


def _tail_call(x2, y, w_out, post_mix, pre_mlp, post_mlp, w_up, w_down):
    m = x2.shape[0]
    tm = TAIL_ROWS
    row = lambda width: pl.BlockSpec((tm, width), lambda i: (i, 0))
    full = lambda a: pl.BlockSpec(a.shape, lambda i: (0,) * a.ndim)
    resident = lambda a: pl.BlockSpec(a.shape, lambda i: (0,) * a.ndim,
                                      pipeline_mode=pl.Buffered(1))
    return pl.pallas_call(
        _tail_kernel,
        grid=(m // tm,),
        in_specs=[row(D_MODEL), row(GLA_V + HGRN_V), resident(w_out), full(post_mix), full(pre_mlp),
                  full(post_mlp), resident(w_up), resident(w_down)],
        out_specs=row(D_MODEL),
        out_shape=jax.ShapeDtypeStruct((m, D_MODEL), F32),
        compiler_params=pltpu.CompilerParams(
            dimension_semantics=("parallel",), vmem_limit_bytes=VMEM_LIMIT_BYTES),
        name="out_proj_mlp",
    )(x2, y, w_out, post_mix, pre_mlp, post_mlp, w_up, w_down)


def kernel(x, w_in, w_gk_up, b_gk, gla_norm_w, hgrn_norm_w, hgrn_lower_bounds, w_out,
           pre_mix_norm, post_mix_norm, pre_mlp_norm, post_mlp_norm, w_up, w_down):
    batch, seq, d = x.shape
    x2 = x.reshape(batch * seq, d)
    l = 0
    w_t = jnp.transpose(w_in[l])
    row2 = lambda a: a.reshape(1, -1)

    y, (w_out_b, w_up_b, w_down_b) = _mixer_call(
        x2, row2(pre_mix_norm[l]), w_t, w_gk_up[l], row2(b_gk[l]), hgrn_lower_bounds,
        row2(gla_norm_w[l]), row2(hgrn_norm_w[l]), (w_out[l], w_up[l], w_down[l]), batch=batch)
    out = _tail_call(x2, y, w_out_b, row2(post_mix_norm[l]), row2(pre_mlp_norm[l]),
                     row2(post_mlp_norm[l]), w_up_b, w_down_b)
    return out.reshape(batch, seq, d)
```

```python
import functools

import jax
import jax.numpy as jnp
from jax import lax
from jax.experimental import pallas as pl
from jax.experimental.pallas import tpu as pltpu

F32 = jnp.float32
BF16 = jnp.bfloat16

D_MODEL = 1024
GLA_HEADS, GLA_DK, GLA_DV = 4, 64, 128
GLA_QK = GLA_HEADS * GLA_DK
GLA_V = GLA_HEADS * GLA_DV
GLA_GATE_RANK = 16
GLA_GATE_NORM = 16.0
HGRN_HEADS, HGRN_DF, HGRN_DV = 4, 128, 128
HGRN_F = HGRN_HEADS * HGRN_DF
HGRN_V = HGRN_HEADS * HGRN_DV
D_FF = 4 * D_MODEL
EPS = 1e-6
LOG2_E = 1.4426950408889634

LANES = 128
VMEM_LIMIT_BYTES = 56 * 1024 * 1024

_C_GQ = 0
_C_GK = _C_GQ + GLA_QK
_C_GV = _C_GK + GLA_QK
_C_GG = _C_GV + GLA_V
GLA_COLS = _C_GG + GLA_V
_C_HQ = 0
_C_HF = _C_HQ + HGRN_F
_C_HI = _C_HF + HGRN_F
_C_HG = _C_HI + HGRN_V
HGRN_COLS = _C_HG + HGRN_V

MIXER_ROWS = 512
TRANSPOSE_ROWS = 512
CUMSUM_ROWS = 256
CHUNK = 64
SUB = 16
NLAG = CHUNK // SUB
LAG_OFFSETS = [sum(CHUNK - SUB * m for m in range(l)) for l in range(NLAG + 1)]
LAG_ROWS = LAG_OFFSETS[NLAG]
TAIL_ROWS = 1024
TAIL_SUBTILES = 2


def _rmsnorm(x, w):
    return x * lax.rsqrt(jnp.mean(x * x, axis=-1, keepdims=True) + EPS) * w


def _sigmoid(x):
    return 1.0 / (1.0 + jnp.exp2(x * (-LOG2_E)))


def _silu(x):
    return x * _sigmoid(x)


def _log_sigmoid(x):
    return -(jnp.maximum(-x, 0.0) + jnp.log(1.0 + jnp.exp2(jnp.abs(x) * (-LOG2_E))))


def _transpose_projection_weight(wt_ref, wa_ref, wbraw_ref, wb_ref):
    turn = lambda lo, n: wt_ref[lo:lo + n, :].astype(BF16).T
    for r in range(0, GLA_COLS, TRANSPOSE_ROWS):
        wa_ref[:, r:r + TRANSPOSE_ROWS] = turn(r, TRANSPOSE_ROWS)
    wbraw_ref[...] = turn(GLA_COLS, LANES)
    hgrn0 = GLA_COLS + GLA_GATE_RANK
    for r in range(0, HGRN_COLS, TRANSPOSE_ROWS):
        wb_ref[:, r:r + TRANSPOSE_ROWS] = turn(hgrn0 + r, TRANSPOSE_ROWS)


def _projection_program(x_ref, wn_ref, wa_ref, wbraw_ref, wb_ref, wgk_ref, bgk_ref, lbl_ref,
                        gla, hgrn):
    x = x_ref[...]
    xg = (x * wn_ref[...]).astype(BF16)
    rstd = lax.rsqrt(jnp.mean(x * x, axis=-1, keepdims=True) + EPS)

    def proj(w_ref, lo, width):
        return rstd * jnp.dot(xg, w_ref[:, lo:lo + width], preferred_element_type=F32)

    def decays_and_operands():
        lbl = lbl_ref[...]
        e = jnp.exp(lbl - jnp.max(lbl, axis=0, keepdims=True))
        lb = e[0:1, :] / jnp.sum(e, axis=0, keepdims=True)
        f = lb + (1.0 - lb) * _sigmoid(proj(wb_ref, _C_HF, HGRN_F))
        hgrn["k"][...] = 1.0 - f
        hgrn["g"][...] = jnp.log(f) * LOG2_E
        glr = proj(wbraw_ref, 0, LANES)[:, :GLA_GATE_RANK].astype(BF16)
        gla["q"][...] = proj(wa_ref, _C_GQ, GLA_QK) * (GLA_DK ** -0.5)
        gla["k"][...] = proj(wa_ref, _C_GK, GLA_QK)
        gla["v"][...] = proj(wa_ref, _C_GV, GLA_V).astype(BF16)
        gk = jnp.dot(glr, wgk_ref[...].astype(BF16), preferred_element_type=F32) + bgk_ref[...]
        gla["g"][...] = _log_sigmoid(gk) * (LOG2_E / GLA_GATE_NORM)
        hgrn["q"][...] = _silu(proj(wb_ref, _C_HQ, HGRN_F))
        hgrn["v"][...] = proj(wb_ref, _C_HI, HGRN_V).astype(BF16)

    def output_gates():
        gla["gate"][...] = gla["gain"][...] * _silu(proj(wa_ref, _C_GG, GLA_V))
        hgrn["gate"][...] = hgrn["gain"][...] * _silu(proj(wb_ref, _C_HG, HGRN_V))

    return decays_and_operands, output_gates


def _mixer_program(q_ref, k_ref, g_ref, v_ref, gate_ref, y_ref, y_col0, b_ref, st_ref, *,
                   heads, dk, dv):
    gw = max(dk, LANES)
    hpg = gw // dk
    ngroups = heads // hpg

    ri = lax.broadcasted_iota(jnp.int32, (CUMSUM_ROWS, CUMSUM_ROWS), 0)
    ci = lax.broadcasted_iota(jnp.int32, (CUMSUM_ROWS, CUMSUM_ROWS), 1)
    tri = jnp.where((ri // CHUNK == ci // CHUNK) & (ci <= ri), 1.0, 0.0).astype(BF16)

    def cumsum_block(r0):
        blk = slice(r0, r0 + CUMSUM_ROWS)
        g = g_ref[blk, :]
        g_hi = g.astype(BF16)
        g_lo = (g - g_hi.astype(F32)).astype(BF16)
        b_ref[blk, :] = (jnp.dot(tri, g_hi, preferred_element_type=F32)
                         + jnp.dot(tri, g_lo, preferred_element_type=F32))

    si = lax.broadcasted_iota(jnp.int32, (SUB, CHUNK), 0)
    sj = lax.broadcasted_iota(jnp.int32, (SUB, CHUNK), 1)
    col_block = [sj // SUB == r for r in range(NLAG)]
    diag_block = [col_block[r] & (sj - SUB * r <= si) for r in range(NLAG)]
    lane = lax.broadcasted_iota(jnp.int32, (1, gw), 1)

    head_ids = [(gi, hh) for gi in range(ngroups) for hh in range(hpg)]
    v_of = lambda h, c: v_ref[c * CHUNK:(c + 1) * CHUNK, h * dv:(h + 1) * dv]
    state = {(gi, hh): st_ref[gi * hpg + hh] for gi, hh in head_ids}
    probs, qin, start_state, prepared = {}, {}, {}, {}
    blk = lambda a, r: a[SUB * r:SUB * (r + 1), :]

    def prep(c):
        r0 = c * CHUNK
        rws = slice(r0, r0 + CHUNK)
        for gi in range(ngroups):
            lanes = slice(gi * gw, (gi + 1) * gw)
            b = b_ref[rws, lanes]
            q = q_ref[rws, lanes]
            k = k_ref[rws, lanes]
            ends = {r: b_ref[r0 + SUB * r + SUB - 1:r0 + SUB * r + SUB, lanes] for r in range(NLAG)}
            ends[-1] = jnp.zeros((1, gw), F32)
            rows_of = lambda f: jnp.concatenate(
                [jnp.broadcast_to(f(r), (SUB, gw)) for r in range(NLAG)], axis=0)

            q0 = q * jnp.exp2(b - rows_of(lambda r: ends[r]))
            q1 = q * jnp.exp2(b - rows_of(lambda r: ends[r - 1]))
            k_hat = k * jnp.exp2(rows_of(lambda r: ends[r]) - b)
            lag_rows = [q0.astype(BF16), q1[SUB:, :].astype(BF16)]
            for l in range(2, NLAG):
                lag_rows.append(jnp.concatenate(
                    [blk(q1, r) * jnp.exp2(ends[r - 1] - ends[r - l]) for r in range(l, NLAG)],
                    axis=0).astype(BF16))
            q_lag = jnp.concatenate(lag_rows, axis=0)
            q_in = jnp.concatenate(
                [blk(q1, r) * jnp.exp2(ends[r - 1]) for r in range(NLAG)], axis=0)
            k_dec = jnp.concatenate(
                [blk(k_hat, r) * jnp.exp2(ends[NLAG - 1] - ends[r]) for r in range(NLAG)],
                axis=0).astype(BF16)
            dec = jnp.exp2(ends[NLAG - 1])
            prepared[gi, c] = (q_lag, k_dec, dec)
            for hh in range(hpg):
                if hpg == 1:
                    prepared[gi, hh, c], qin[gi, hh, c] = k_hat.astype(BF16), q_in.astype(BF16)
                else:
                    in_head = (lane >= hh * dk) & (lane < (hh + 1) * dk)
                    prepared[gi, hh, c] = jnp.where(in_head, k_hat, 0.0).astype(BF16)
                    qin[gi, hh, c] = jnp.where(in_head, q_in, 0.0).astype(BF16)

    def scores_and_state(c):
        for gi in range(ngroups):
            q_lag, k_dec, dec = prepared.pop((gi, c))
            for hh in range(hpg):
                h = gi * hpg + hh
                k_h = prepared.pop((gi, hh, c))
                s_all = lax.dot_general(q_lag, k_h, (((1,), (1,)), ((), ())),
                                        preferred_element_type=F32)
                p_rows = []
                for r in range(NLAG):
                    p_r = jnp.where(diag_block[r], blk(s_all, r), 0.0)
                    for l in range(1, r + 1):
                        src = LAG_OFFSETS[l] + SUB * (r - l)
                        p_r = jnp.where(col_block[r - l], s_all[src:src + SUB, :], p_r)
                    p_rows.append(p_r)
                probs[gi, hh, c] = jnp.concatenate(p_rows, axis=0).astype(BF16)
                upd = lax.dot_general(v_of(h, c), k_dec, (((0,), (0,)), ((), ())),
                                      preferred_element_type=F32)
                start_state[gi, hh, c] = state[gi, hh].astype(BF16).T
                state[gi, hh] = state[gi, hh] * dec + upd

    def outputs(c):
        for gi, hh in head_ids:
            h = gi * hpg + hh
            lhs = jnp.concatenate([qin.pop((gi, hh, c)), probs.pop((gi, hh, c))], axis=1)
            rhs = jnp.concatenate([start_state.pop((gi, hh, c)), v_of(h, c)], axis=0)
            o = jnp.dot(lhs, rhs, preferred_element_type=F32)
            rws, cols = slice(c * CHUNK, (c + 1) * CHUNK), slice(h * dv, (h + 1) * dv)
            y = o * lax.rsqrt(jnp.mean(o * o, axis=-1, keepdims=True) + EPS) * gate_ref[rws, cols]
            y_ref[rws, y_col0 + h * dv:y_col0 + (h + 1) * dv] = y.astype(BF16)

    def store_state():
        for gi, hh in head_ids:
            st_ref[gi * hpg + hh] = state[gi, hh]

    return cumsum_block, prep, scores_and_state, outputs, store_state


_MIXERS = (dict(heads=GLA_HEADS, dk=GLA_DK, dv=GLA_DV), dict(heads=HGRN_HEADS, dk=HGRN_DF, dv=HGRN_DV))


def _mixer_kernel(x_ref, wn_ref, wt_ref, wgk_ref, bgk_ref, lbl_ref, nwa_ref, nwb_ref, *rest,
                  n_cast):
    cast_in, rest = rest[:n_cast], rest[n_cast:]
    y_ref, rest = rest[0], rest[1:]
    cast_out, rest = rest[:n_cast], rest[n_cast:]
    (wa_ref, wbraw_ref, wb_ref,
     qa_ref, ka_ref, ga_ref, va_ref, gga_ref, ba_ref, sta_ref,
     qb_ref, kb_ref, gb_ref, vb_ref, ggb_ref, bb_ref, stb_ref) = rest

    @pl.when((pl.program_id(0) == 0) & (pl.program_id(1) == 0))
    def _():
        _transpose_projection_weight(wt_ref, wa_ref, wbraw_ref, wb_ref)

    @pl.when(pl.program_id(1) == 0)
    def _():
        sta_ref[...] = jnp.zeros_like(sta_ref)
        stb_ref[...] = jnp.zeros_like(stb_ref)

    for src, dst in zip(cast_in, cast_out):
        dst[...] = src[...].astype(BF16)

    decays_and_operands, output_gates = _projection_program(
        x_ref, wn_ref, wa_ref, wbraw_ref, wb_ref, wgk_ref, bgk_ref, lbl_ref,
        dict(q=qa_ref, k=ka_ref, g=ga_ref, v=va_ref, gate=gga_ref, gain=nwa_ref),
        dict(q=qb_ref, k=kb_ref, g=gb_ref, v=vb_ref, gate=ggb_ref, gain=nwb_ref))
    programs = [
        _mixer_program(qa_ref, ka_ref, ga_ref, va_ref, gga_ref, y_ref, 0,
                       ba_ref, sta_ref, **_MIXERS[0]),
        _mixer_program(qb_ref, kb_ref, gb_ref, vb_ref, ggb_ref, y_ref, GLA_V,
                       bb_ref, stb_ref, **_MIXERS[1]),
    ]
    rows = x_ref.shape[0]
    nchunk = rows // CHUNK

    decays_and_operands()
    for cumsum_block, _, _, _, _ in programs:
        for r0 in range(0, rows, CUMSUM_ROWS):
            cumsum_block(r0)
    output_gates()
    for c in range(nchunk):
        for _, prep, _, _, _ in programs:
            prep(c)
    for _, _, scores_and_state, outputs, store_state in programs:
        for c in range(nchunk):
            scores_and_state(c)
            if c >= 1:
                outputs(c - 1)
        outputs(nchunk - 1)
        store_state()


def _mixer_call(x2, wn, w_t, w_gk, bgk, lbl, nwa, nwb, cast_weights, *, batch):
    m = x2.shape[0]
    tm = MIXER_ROWS
    nt = m // batch // tm
    steps = batch * nt
    row = lambda width: pl.BlockSpec((tm, width), lambda b, i: (b * nt + i, 0))
    full = lambda a: pl.BlockSpec(a.shape, lambda b, i: (0,) * a.ndim)
    resident = lambda a: pl.BlockSpec(a.shape, lambda b, i: (0,) * a.ndim,
                                      pipeline_mode=pl.Buffered(1))
    slab = lambda a: pl.BlockSpec((a.shape[0] // steps, a.shape[1]), lambda b, i: (b * nt + i, 0))
    out_shapes = (jax.ShapeDtypeStruct((m, GLA_V + HGRN_V), BF16),)
    cast_shapes = tuple(jax.ShapeDtypeStruct(w.shape, BF16) for w in cast_weights)
    scratch = [pltpu.VMEM((D_MODEL, GLA_COLS), BF16), pltpu.VMEM((D_MODEL, LANES), BF16),
               pltpu.VMEM((D_MODEL, HGRN_COLS), BF16)]
    for mix in _MIXERS:
        w, gw = mix["heads"] * mix["dk"], max(mix["dk"], LANES)
        wv = mix["heads"] * mix["dv"]
        scratch += [pltpu.VMEM((tm, w), F32), pltpu.VMEM((tm, w), F32), pltpu.VMEM((tm, w), F32),
                    pltpu.VMEM((tm, wv), BF16), pltpu.VMEM((tm, wv), F32),
                    pltpu.VMEM((tm, w), F32), pltpu.VMEM((mix["heads"], mix["dv"], gw), F32)]
    outs = pl.pallas_call(
        functools.partial(_mixer_kernel, n_cast=len(cast_weights)),
        grid=(batch, nt),
        in_specs=[row(D_MODEL), full(wn), resident(w_t), full(w_gk), full(bgk), full(lbl),
                  full(nwa), full(nwb)] + [slab(w) for w in cast_weights],
        out_specs=tuple(row(s.shape[1]) for s in out_shapes) + tuple(slab(w) for w in cast_weights),
        out_shape=out_shapes + cast_shapes,
        scratch_shapes=scratch,
        compiler_params=pltpu.CompilerParams(
            dimension_semantics=("arbitrary", "arbitrary"), vmem_limit_bytes=VMEM_LIMIT_BYTES),
        name="proj_gla_hgrn_mixers",
    )(x2, wn, w_t, w_gk, bgk, lbl, nwa, nwb, *cast_weights)
    return outs[0], outs[1:]


def _tail_kernel(x_ref, y_ref, w_out_ref, post_mix_ref, pre_mlp_ref, post_mlp_ref,
                 w_up_ref, w_down_ref, out_ref):
    sub = x_ref.shape[0] // TAIL_SUBTILES
    tiles = [slice(s * sub, (s + 1) * sub) for s in range(TAIL_SUBTILES)]
    hs, us = {}, {}

    def out_proj(s):
        r = tiles[s]
        mix = jnp.dot(y_ref[r, :], w_out_ref[...], preferred_element_type=F32)
        hs[s] = x_ref[r, :] + _rmsnorm(mix, post_mix_ref[...])

    def up(s):
        us[s] = jnp.dot(_rmsnorm(hs[s], pre_mlp_ref[...]).astype(BF16), w_up_ref[...],
                        preferred_element_type=F32)

    def down(s):
        relu = jnp.maximum(us.pop(s), 0.0)
        m = jnp.dot((relu * relu).astype(BF16), w_down_ref[...], preferred_element_type=F32)
        out_ref[tiles[s], :] = hs.pop(s) + _rmsnorm(m, post_mlp_ref[...])

    for stage in (out_proj, up, down):
        for s in range(TAIL_SUBTILES):
            stage(s)


def _tail_call(x2, y, w_out, post_mix, pre_mlp, post_mlp, w_up, w_down):
    m = x2.shape[0]
    tm = TAIL_ROWS
    row = lambda width: pl.BlockSpec((tm, width), lambda i: (i, 0))
    full = lambda a: pl.BlockSpec(a.shape, lambda i: (0,) * a.ndim)
    resident = lambda a: pl.BlockSpec(a.shape, lambda i: (0,) * a.ndim,
                                      pipeline_mode=pl.Buffered(1))
    return pl.pallas_call(
        _tail_kernel,
        grid=(m // tm,),
        in_specs=[row(D_MODEL), row(GLA_V + HGRN_V), resident(w_out), full(post_mix), full(pre_mlp),
                  full(post_mlp), resident(w_up), resident(w_down)],
        out_specs=row(D_MODEL),
        out_shape=jax.ShapeDtypeStruct((m, D_MODEL), F32),
        compiler_params=pltpu.CompilerParams(
            dimension_semantics=("parallel",), vmem_limit_bytes=VMEM_LIMIT_BYTES),
        name="out_proj_mlp",
    )(x2, y, w_out, post_mix, pre_mlp, post_mlp, w_up, w_down)


def kernel(x, w_in, w_gk_up, b_gk, gla_norm_w, hgrn_norm_w, hgrn_lower_bounds, w_out,
           pre_mix_norm, post_mix_norm, pre_mlp_norm, post_mlp_norm, w_up, w_down):
    batch, seq, d = x.shape
    x2 = x.reshape(batch * seq, d)
    l = 0
    w_t = jnp.transpose(w_in[l])
    row2 = lambda a: a.reshape(1, -1)

    y, (w_out_b, w_up_b, w_down_b) = _mixer_call(
        x2, row2(pre_mix_norm[l]), w_t, w_gk_up[l], row2(b_gk[l]), hgrn_lower_bounds,
        row2(gla_norm_w[l]), row2(hgrn_norm_w[l]), (w_out[l], w_up[l], w_down[l]), batch=batch)
    out = _tail_call(x2, y, w_out_b, row2(post_mix_norm[l]), row2(pre_mlp_norm[l]),
                     row2(post_mlp_norm[l]), w_up_b, w_down_b)
    return out.reshape(batch, seq, d)
```

```python
import functools

import jax
import jax.numpy as jnp
from jax import lax
from jax.experimental import pallas as pl
from jax.experimental.pallas import tpu as pltpu

F32 = jnp.float32
BF16 = jnp.bfloat16

D_MODEL = 1024
GLA_HEADS, GLA_DK, GLA_DV = 4, 64, 128
GLA_QK = GLA_HEADS * GLA_DK
GLA_V = GLA_HEADS * GLA_DV
GLA_GATE_RANK = 16
GLA_GATE_NORM = 16.0
HGRN_HEADS, HGRN_DF, HGRN_DV = 4, 128, 128
HGRN_F = HGRN_HEADS * HGRN_DF
HGRN_V = HGRN_HEADS * HGRN_DV
D_FF = 4 * D_MODEL
EPS = 1e-6
LOG2_E = 1.4426950408889634

LANES = 128
VMEM_LIMIT_BYTES = 56 * 1024 * 1024

_C_GQ = 0
_C_GK = _C_GQ + GLA_QK
_C_GV = _C_GK + GLA_QK
_C_GG = _C_GV + GLA_V
GLA_COLS = _C_GG + GLA_V
_C_HQ = 0
_C_HF = _C_HQ + HGRN_F
_C_HI = _C_HF + HGRN_F
_C_HG = _C_HI + HGRN_V
HGRN_COLS = _C_HG + HGRN_V

MIXER_ROWS = 512
TRANSPOSE_ROWS = 512
CUMSUM_ROWS = 256
CHUNK = 64
SUB = 16
NLAG = CHUNK // SUB
LAG_OFFSETS = [sum(CHUNK - SUB * m for m in range(l)) for l in range(NLAG + 1)]
LAG_ROWS = LAG_OFFSETS[NLAG]
TAIL_ROWS = 1024
TAIL_SUBTILES = 2


def _rmsnorm(x, w):
    return x * lax.rsqrt(jnp.mean(x * x, axis=-1, keepdims=True) + EPS) * w


def _sigmoid(x):
    return 1.0 / (1.0 + jnp.exp2(x * (-LOG2_E)))


def _silu(x):
    return x * _sigmoid(x)


def _log_sigmoid(x):
    return -(jnp.maximum(-x, 0.0) + jnp.log(1.0 + jnp.exp2(jnp.abs(x) * (-LOG2_E))))


def _transpose_projection_weight(wt_ref, wa_ref, wbraw_ref, wb_ref):
    turn = lambda lo, n: wt_ref[lo:lo + n, :].astype(BF16).T
    for r in range(0, GLA_COLS, TRANSPOSE_ROWS):
        wa_ref[:, r:r + TRANSPOSE_ROWS] = turn(r, TRANSPOSE_ROWS)
    wbraw_ref[...] = turn(GLA_COLS, LANES)
    hgrn0 = GLA_COLS + GLA_GATE_RANK
    for r in range(0, HGRN_COLS, TRANSPOSE_ROWS):
        wb_ref[:, r:r + TRANSPOSE_ROWS] = turn(hgrn0 + r, TRANSPOSE_ROWS)


def _projection_program(x_ref, wn_ref, wa_ref, wbraw_ref, wb_ref, wgk_ref, bgk_ref, lbl_ref,
                        gla, hgrn):
    x = x_ref[...]
    xg = (x * wn_ref[...]).astype(BF16)
    rstd = lax.rsqrt(jnp.mean(x * x, axis=-1, keepdims=True) + EPS)

    def proj(w_ref, lo, width):
        return rstd * jnp.dot(xg, w_ref[:, lo:lo + width], preferred_element_type=F32)

    def decays_and_operands():
        lbl = lbl_ref[...]
        e = jnp.exp(lbl - jnp.max(lbl, axis=0, keepdims=True))
        lb = e[0:1, :] / jnp.sum(e, axis=0, keepdims=True)
        f = lb + (1.0 - lb) * _sigmoid(proj(wb_ref, _C_HF, HGRN_F))
        hgrn["k"][...] = 1.0 - f
        hgrn["g"][...] = jnp.log(f) * LOG2_E
        glr = proj(wbraw_ref, 0, LANES)[:, :GLA_GATE_RANK].astype(BF16)
        gla["q"][...] = proj(wa_ref, _C_GQ, GLA_QK) * (GLA_DK ** -0.5)
        gla["k"][...] = proj(wa_ref, _C_GK, GLA_QK)
        gla["v"][...] = proj(wa_ref, _C_GV, GLA_V).astype(BF16)
        gk = jnp.dot(glr, wgk_ref[...].astype(BF16), preferred_element_type=F32) + bgk_ref[...]
        gla["g"][...] = _log_sigmoid(gk) * (LOG2_E / GLA_GATE_NORM)
        hgrn["q"][...] = _silu(proj(wb_ref, _C_HQ, HGRN_F))
        hgrn["v"][...] = proj(wb_ref, _C_HI, HGRN_V).astype(BF16)

    def output_gates():
        gla["gate"][...] = gla["gain"][...] * _silu(proj(wa_ref, _C_GG, GLA_V))
        hgrn["gate"][...] = hgrn["gain"][...] * _silu(proj(wb_ref, _C_HG, HGRN_V))

    return decays_and_operands, output_gates


def _mixer_program(q_ref, k_ref, g_ref, v_ref, gate_ref, y_ref, y_col0, b_ref, st_ref, *,
                   heads, dk, dv):
    gw = max(dk, LANES)
    hpg = gw // dk
    ngroups = heads // hpg

    ri = lax.broadcasted_iota(jnp.int32, (CUMSUM_ROWS, CUMSUM_ROWS), 0)
    ci = lax.broadcasted_iota(jnp.int32, (CUMSUM_ROWS, CUMSUM_ROWS), 1)
    tri = jnp.where((ri // CHUNK == ci // CHUNK) & (ci <= ri), 1.0, 0.0).astype(BF16)

    def cumsum_block(r0):
        blk = slice(r0, r0 + CUMSUM_ROWS)
        g = g_ref[blk, :]
        g_hi = g.astype(BF16)
        g_lo = (g - g_hi.astype(F32)).astype(BF16)
        b_ref[blk, :] = (jnp.dot(tri, g_hi, preferred_element_type=F32)
                         + jnp.dot(tri, g_lo, preferred_element_type=F32))

    si = lax.broadcasted_iota(jnp.int32, (SUB, hpg * CHUNK), 0)
    sj = lax.broadcasted_iota(jnp.int32, (SUB, hpg * CHUNK), 1) % CHUNK
    col_block = [sj // SUB == r for r in range(NLAG)]
    diag_block = [col_block[r] & (sj - SUB * r <= si) for r in range(NLAG)]
    lane = lax.broadcasted_iota(jnp.int32, (1, gw), 1)

    head_ids = [(gi, hh) for gi in range(ngroups) for hh in range(hpg)]
    v_of = lambda h, c: v_ref[c * CHUNK:(c + 1) * CHUNK, h * dv:(h + 1) * dv]
    state = {(gi, hh): st_ref[gi * hpg + hh] for gi, hh in head_ids}
    probs, qin, start_state, prepared = {}, {}, {}, {}
    blk = lambda a, r: a[SUB * r:SUB * (r + 1), :]

    def prep(c):
        r0 = c * CHUNK
        rws = slice(r0, r0 + CHUNK)
        for gi in range(ngroups):
            lanes = slice(gi * gw, (gi + 1) * gw)
            b = b_ref[rws, lanes]
            q = q_ref[rws, lanes]
            k = k_ref[rws, lanes]
            ends = {r: b_ref[r0 + SUB * r + SUB - 1:r0 + SUB * r + SUB, lanes] for r in range(NLAG)}
            ends[-1] = jnp.zeros((1, gw), F32)
            rows_of = lambda f: jnp.concatenate(
                [jnp.broadcast_to(f(r), (SUB, gw)) for r in range(NLAG)], axis=0)

            q0 = q * jnp.exp2(b - rows_of(lambda r: ends[r]))
            q1 = q * jnp.exp2(b - rows_of(lambda r: ends[r - 1]))
            k_hat = k * jnp.exp2(rows_of(lambda r: ends[r]) - b)
            lag_rows = [q0.astype(BF16), q1[SUB:, :].astype(BF16)]
            for l in range(2, NLAG):
                lag_rows.append(jnp.concatenate(
                    [blk(q1, r) * jnp.exp2(ends[r - 1] - ends[r - l]) for r in range(l, NLAG)],
                    axis=0).astype(BF16))
            q_lag = jnp.concatenate(lag_rows, axis=0)
            q_in = jnp.concatenate(
                [blk(q1, r) * jnp.exp2(ends[r - 1]) for r in range(NLAG)], axis=0)
            k_dec = jnp.concatenate(
                [blk(k_hat, r) * jnp.exp2(ends[NLAG - 1] - ends[r]) for r in range(NLAG)],
                axis=0).astype(BF16)
            dec = jnp.exp2(ends[NLAG - 1])
            if hpg == 1:
                keys, qin[gi, 0, c] = k_hat.astype(BF16), q_in.astype(BF16)
            else:
                in_head = [(lane >= hh * dk) & (lane < (hh + 1) * dk) for hh in range(hpg)]
                keys = jnp.concatenate(
                    [jnp.where(m, k_hat, 0.0).astype(BF16) for m in in_head], axis=0)
                for hh in range(hpg):
                    qin[gi, hh, c] = jnp.where(in_head[hh], q_in, 0.0).astype(BF16)
            prepared[gi, c] = (q_lag, keys, k_dec, dec)

    def scores_and_state(c):
        for gi in range(ngroups):
            q_lag, keys, k_dec, dec = prepared.pop((gi, c))
            s_all = lax.dot_general(q_lag, keys, (((1,), (1,)), ((), ())),
                                    preferred_element_type=F32)
            p_rows = []
            for r in range(NLAG):
                p_r = jnp.where(diag_block[r], blk(s_all, r), 0.0)
                for l in range(1, r + 1):
                    src = LAG_OFFSETS[l] + SUB * (r - l)
                    p_r = jnp.where(col_block[r - l], s_all[src:src + SUB, :], p_r)
                p_rows.append(p_r)
            p_all = jnp.concatenate(p_rows, axis=0).astype(BF16)
            for hh in range(hpg):
                h = gi * hpg + hh
                probs[gi, hh, c] = p_all[:, hh * CHUNK:(hh + 1) * CHUNK]
                upd = lax.dot_general(v_of(h, c), k_dec, (((0,), (0,)), ((), ())),
                                      preferred_element_type=F32)
                start_state[gi, hh, c] = state[gi, hh].astype(BF16).T
                state[gi, hh] = state[gi, hh] * dec + upd

    def outputs(c):
        for gi, hh in head_ids:
            h = gi * hpg + hh
            lhs = jnp.concatenate([qin.pop((gi, hh, c)), probs.pop((gi, hh, c))], axis=1)
            rhs = jnp.concatenate([start_state.pop((gi, hh, c)), v_of(h, c)], axis=0)
            o = jnp.dot(lhs, rhs, preferred_element_type=F32)
            rws, cols = slice(c * CHUNK, (c + 1) * CHUNK), slice(h * dv, (h + 1) * dv)
            y = o * lax.rsqrt(jnp.mean(o * o, axis=-1, keepdims=True) + EPS) * gate_ref[rws, cols]
            y_ref[rws, y_col0 + h * dv:y_col0 + (h + 1) * dv] = y.astype(BF16)

    def store_state():
        for gi, hh in head_ids:
            st_ref[gi * hpg + hh] = state[gi, hh]

    return cumsum_block, prep, scores_and_state, outputs, store_state


_MIXERS = (dict(heads=GLA_HEADS, dk=GLA_DK, dv=GLA_DV), dict(heads=HGRN_HEADS, dk=HGRN_DF, dv=HGRN_DV))


def _mixer_kernel(x_ref, wn_ref, wt_ref, wgk_ref, bgk_ref, lbl_ref, nwa_ref, nwb_ref, *rest,
                  n_cast):
    cast_in, rest = rest[:n_cast], rest[n_cast:]
    y_ref, rest = rest[0], rest[1:]
    cast_out, rest = rest[:n_cast], rest[n_cast:]
    (wa_ref, wbraw_ref, wb_ref,
     qa_ref, ka_ref, ga_ref, va_ref, gga_ref, ba_ref, sta_ref,
     qb_ref, kb_ref, gb_ref, vb_ref, ggb_ref, bb_ref, stb_ref) = rest

    @pl.when((pl.program_id(0) == 0) & (pl.program_id(1) == 0))
    def _():
        _transpose_projection_weight(wt_ref, wa_ref, wbraw_ref, wb_ref)

    @pl.when(pl.program_id(1) == 0)
    def _():
        sta_ref[...] = jnp.zeros_like(sta_ref)
        stb_ref[...] = jnp.zeros_like(stb_ref)

    for src, dst in zip(cast_in, cast_out):
        dst[...] = src[...].astype(BF16)

    decays_and_operands, output_gates = _projection_program(
        x_ref, wn_ref, wa_ref, wbraw_ref, wb_ref, wgk_ref, bgk_ref, lbl_ref,
        dict(q=qa_ref, k=ka_ref, g=ga_ref, v=va_ref, gate=gga_ref, gain=nwa_ref),
        dict(q=qb_ref, k=kb_ref, g=gb_ref, v=vb_ref, gate=ggb_ref, gain=nwb_ref))
    programs = [
        _mixer_program(qa_ref, ka_ref, ga_ref, va_ref, gga_ref, y_ref, 0,
                       ba_ref, sta_ref, **_MIXERS[0]),
        _mixer_program(qb_ref, kb_ref, gb_ref, vb_ref, ggb_ref, y_ref, GLA_V,
                       bb_ref, stb_ref, **_MIXERS[1]),
    ]
    rows = x_ref.shape[0]
    nchunk = rows // CHUNK

    decays_and_operands()
    for cumsum_block, _, _, _, _ in programs:
        for r0 in range(0, rows, CUMSUM_ROWS):
            cumsum_block(r0)
    output_gates()
    for c in range(nchunk):
        for _, prep, _, _, _ in programs:
            prep(c)
    for _, _, scores_and_state, outputs, store_state in programs:
        for c in range(nchunk):
            scores_and_state(c)
            if c >= 1:
                outputs(c - 1)
        outputs(nchunk - 1)
        store_state()


def _mixer_call(x2, wn, w_t, w_gk, bgk, lbl, nwa, nwb, cast_weights, *, batch):
    m = x2.shape[0]
    tm = MIXER_ROWS
    nt = m // batch // tm
    steps = batch * nt
    row = lambda width: pl.BlockSpec((tm, width), lambda b, i: (b * nt + i, 0))
    full = lambda a: pl.BlockSpec(a.shape, lambda b, i: (0,) * a.ndim)
    resident = lambda a: pl.BlockSpec(a.shape, lambda b, i: (0,) * a.ndim,
                                      pipeline_mode=pl.Buffered(1))
    slab = lambda a: pl.BlockSpec((a.shape[0] // steps, a.shape[1]), lambda b, i: (b * nt + i, 0))
    out_shapes = (jax.ShapeDtypeStruct((m, GLA_V + HGRN_V), BF16),)
    cast_shapes = tuple(jax.ShapeDtypeStruct(w.shape, BF16) for w in cast_weights)
    scratch = [pltpu.VMEM((D_MODEL, GLA_COLS), BF16), pltpu.VMEM((D_MODEL, LANES), BF16),
               pltpu.VMEM((D_MODEL, HGRN_COLS), BF16)]
    for mix in _MIXERS:
        w, gw = mix["heads"] * mix["dk"], max(mix["dk"], LANES)
        wv = mix["heads"] * mix["dv"]
        scratch += [pltpu.VMEM((tm, w), F32), pltpu.VMEM((tm, w), F32), pltpu.VMEM((tm, w), F32),
                    pltpu.VMEM((tm, wv), BF16), pltpu.VMEM((tm, wv), F32),
                    pltpu.VMEM((tm, w), F32), pltpu.VMEM((mix["heads"], mix["dv"], gw), F32)]
    outs = pl.pallas_call(
        functools.partial(_mixer_kernel, n_cast=len(cast_weights)),
        grid=(batch, nt),
        in_specs=[row(D_MODEL), full(wn), resident(w_t), full(w_gk), full(bgk), full(lbl),
                  full(nwa), full(nwb)] + [slab(w) for w in cast_weights],
        out_specs=tuple(row(s.shape[1]) for s in out_shapes) + tuple(slab(w) for w in cast_weights),
        out_shape=out_shapes + cast_shapes,
        scratch_shapes=scratch,
        compiler_params=pltpu.CompilerParams(
            dimension_semantics=("arbitrary", "arbitrary"), vmem_limit_bytes=VMEM_LIMIT_BYTES),
        name="proj_gla_hgrn_mixers",
    )(x2, wn, w_t, w_gk, bgk, lbl, nwa, nwb, *cast_weights)
    return outs[0], outs[1:]


def _tail_kernel(x_ref, y_ref, w_out_ref, post_mix_ref, pre_mlp_ref, post_mlp_ref,
                 w_up_ref, w_down_ref, out_ref):
    sub = x_ref.shape[0] // TAIL_SUBTILES
    tiles = [slice(s * sub, (s + 1) * sub) for s in range(TAIL_SUBTILES)]
    hs, us = {}, {}

    def out_proj(s):
        r = tiles[s]
        mix = jnp.dot(y_ref[r, :], w_out_ref[...], preferred_element_type=F32)
        hs[s] = x_ref[r, :] + _rmsnorm(mix, post_mix_ref[...])

    def up(s):
        us[s] = jnp.dot(_rmsnorm(hs[s], pre_mlp_ref[...]).astype(BF16), w_up_ref[...],
                        preferred_element_type=F32)

    def down(s):
        relu = jnp.maximum(us.pop(s), 0.0)
        m = jnp.dot((relu * relu).astype(BF16), w_down_ref[...], preferred_element_type=F32)
        out_ref[tiles[s], :] = hs.pop(s) + _rmsnorm(m, post_mlp_ref[...])

    for stage in (out_proj, up, down):
        for s in range(TAIL_SUBTILES):
            stage(s)


def _tail_call(x2, y, w_out, post_mix, pre_mlp, post_mlp, w_up, w_down):
    m = x2.shape[0]
    tm = TAIL_ROWS
    row = lambda width: pl.BlockSpec((tm, width), lambda i: (i, 0))
    full = lambda a: pl.BlockSpec(a.shape, lambda i: (0,) * a.ndim)
    resident = lambda a: pl.BlockSpec(a.shape, lambda i: (0,) * a.ndim,
                                      pipeline_mode=pl.Buffered(1))
    return pl.pallas_call(
        _tail_kernel,
        grid=(m // tm,),
        in_specs=[row(D_MODEL), row(GLA_V + HGRN_V), resident(w_out), full(post_mix), full(pre_mlp),
                  full(post_mlp), resident(w_up), resident(w_down)],
        out_specs=row(D_MODEL),
        out_shape=jax.ShapeDtypeStruct((m, D_MODEL), F32),
        compiler_params=pltpu.CompilerParams(
            dimension_semantics=("parallel",), vmem_limit_bytes=VMEM_LIMIT_BYTES),
        name="out_proj_mlp",
    )(x2, y, w_out, post_mix, pre_mlp, post_mlp, w_up, w_down)


def kernel(x, w_in, w_gk_up, b_gk, gla_norm_w, hgrn_norm_w, hgrn_lower_bounds, w_out,
           pre_mix_norm, post_mix_norm, pre_mlp_norm, post_mlp_norm, w_up, w_down):
    batch, seq, d = x.shape
    x2 = x.reshape(batch * seq, d)
    l = 0
    w_t = jnp.transpose(w_in[l])
    row2 = lambda a: a.reshape(1, -1)

    y, (w_out_b, w_up_b, w_down_b) = _mixer_call(
        x2, row2(pre_mix_norm[l]), w_t, w_gk_up[l], row2(b_gk[l]), hgrn_lower_bounds,
        row2(gla_norm_w[l]), row2(hgrn_norm_w[l]), (w_out[l], w_up[l], w_down[l]), batch=batch)
    out = _tail_call(x2, y, w_out_b, row2(post_mix_norm[l]), row2(pre_mlp_norm[l]),
                     row2(post_mlp_norm[l]), w_up_b, w_down_b)
    return out.reshape(batch, seq, d)
```

```python
import functools

import jax
import jax.numpy as jnp
from jax import lax
from jax.experimental import pallas as pl
from jax.experimental.pallas import tpu as pltpu

F32 = jnp.float32
BF16 = jnp.bfloat16

D_MODEL = 1024
GLA_HEADS, GLA_DK, GLA_DV = 4, 64, 128
GLA_QK = GLA_HEADS * GLA_DK
GLA_V = GLA_HEADS * GLA_DV
GLA_GATE_RANK = 16
GLA_GATE_NORM = 16.0
HGRN_HEADS, HGRN_DF, HGRN_DV = 4, 128, 128
HGRN_F = HGRN_HEADS * HGRN_DF
HGRN_V = HGRN_HEADS * HGRN_DV
D_FF = 4 * D_MODEL
EPS = 1e-6
LOG2_E = 1.4426950408889634

LANES = 128
VMEM_LIMIT_BYTES = 56 * 1024 * 1024

_C_GQ = 0
_C_GK = _C_GQ + GLA_QK
_C_GV = _C_GK + GLA_QK
_C_GG = _C_GV + GLA_V
GLA_COLS = _C_GG + GLA_V
_C_HQ = 0
_C_HF = _C_HQ + HGRN_F
_C_HI = _C_HF + HGRN_F
_C_HG = _C_HI + HGRN_V
HGRN_COLS = _C_HG + HGRN_V

MIXER_ROWS = 512
TRANSPOSE_ROWS = 512
CUMSUM_ROWS = 256
CHUNK = 64
SUB = 16
NLAG = CHUNK // SUB
LAG_OFFSETS = [sum(CHUNK - SUB * m for m in range(l)) for l in range(NLAG + 1)]
LAG_ROWS = LAG_OFFSETS[NLAG]
TAIL_ROWS = 1024
TAIL_SUBTILES = 2


def _rmsnorm(x, w):
    return x * lax.rsqrt(jnp.mean(x * x, axis=-1, keepdims=True) + EPS) * w


def _sigmoid(x):
    return 1.0 / (1.0 + jnp.exp2(x * (-LOG2_E)))


def _silu(x):
    return x * _sigmoid(x)


def _log_sigmoid(x):
    return -(jnp.maximum(-x, 0.0) + jnp.log(1.0 + jnp.exp2(jnp.abs(x) * (-LOG2_E))))


def _transpose_projection_weight(wt_ref, wa_ref, wbraw_ref, wb_ref):
    turn = lambda lo, n: wt_ref[lo:lo + n, :].astype(BF16).T
    for r in range(0, GLA_COLS, TRANSPOSE_ROWS):
        wa_ref[:, r:r + TRANSPOSE_ROWS] = turn(r, TRANSPOSE_ROWS)
    wbraw_ref[...] = turn(GLA_COLS, LANES)
    hgrn0 = GLA_COLS + GLA_GATE_RANK
    for r in range(0, HGRN_COLS, TRANSPOSE_ROWS):
        wb_ref[:, r:r + TRANSPOSE_ROWS] = turn(hgrn0 + r, TRANSPOSE_ROWS)


def _projection_program(x_ref, wn_ref, wa_ref, wbraw_ref, wb_ref, wgk_ref, bgk_ref, lbl_ref,
                        gla, hgrn):
    hb = _rmsnorm(x_ref[...], wn_ref[...]).astype(BF16)

    def proj(w_ref, lo, width):
        return jnp.dot(hb, w_ref[:, lo:lo + width], preferred_element_type=F32)

    def decays_and_operands():
        lbl = lbl_ref[...]
        e = jnp.exp(lbl - jnp.max(lbl, axis=0, keepdims=True))
        lb = e[0:1, :] / jnp.sum(e, axis=0, keepdims=True)
        f = lb + (1.0 - lb) * _sigmoid(proj(wb_ref, _C_HF, HGRN_F))
        hgrn["k"][...] = 1.0 - f
        hgrn["g"][...] = jnp.log(f) * LOG2_E
        glr = proj(wbraw_ref, 0, LANES)[:, :GLA_GATE_RANK].astype(BF16)
        gla["q"][...] = proj(wa_ref, _C_GQ, GLA_QK) * (GLA_DK ** -0.5)
        gla["k"][...] = proj(wa_ref, _C_GK, GLA_QK)
        gla["v"][...] = proj(wa_ref, _C_GV, GLA_V).astype(BF16)
        gk = jnp.dot(glr, wgk_ref[...].astype(BF16), preferred_element_type=F32) + bgk_ref[...]
        gla["g"][...] = _log_sigmoid(gk) * (LOG2_E / GLA_GATE_NORM)
        hgrn["q"][...] = _silu(proj(wb_ref, _C_HQ, HGRN_F))
        hgrn["v"][...] = proj(wb_ref, _C_HI, HGRN_V).astype(BF16)

    def output_gates():
        gla["gate"][...] = gla["gain"][...] * _silu(proj(wa_ref, _C_GG, GLA_V))
        hgrn["gate"][...] = hgrn["gain"][...] * _silu(proj(wb_ref, _C_HG, HGRN_V))

    return decays_and_operands, output_gates


def _mixer_program(q_ref, k_ref, g_ref, v_ref, gate_ref, y_ref, y_col0, b_ref, st_ref, *,
                   heads, dk, dv):
    gw = max(dk, LANES)
    hpg = gw // dk
    ngroups = heads // hpg

    ri = lax.broadcasted_iota(jnp.int32, (CUMSUM_ROWS, CUMSUM_ROWS), 0)
    ci = lax.broadcasted_iota(jnp.int32, (CUMSUM_ROWS, CUMSUM_ROWS), 1)
    tri = jnp.where((ri // CHUNK == ci // CHUNK) & (ci <= ri), 1.0, 0.0).astype(BF16)

    def cumsum_block(r0):
        blk = slice(r0, r0 + CUMSUM_ROWS)
        g = g_ref[blk, :]
        g_hi = g.astype(BF16)
        g_lo = (g - g_hi.astype(F32)).astype(BF16)
        b_ref[blk, :] = (jnp.dot(tri, g_hi, preferred_element_type=F32)
                         + jnp.dot(tri, g_lo, preferred_element_type=F32))

    si = lax.broadcasted_iota(jnp.int32, (SUB, CHUNK), 0)
    sj = lax.broadcasted_iota(jnp.int32, (SUB, CHUNK), 1)
    col_block = [sj // SUB == r for r in range(NLAG)]
    diag_block = [col_block[r] & (sj - SUB * r <= si) for r in range(NLAG)]
    lane = lax.broadcasted_iota(jnp.int32, (1, gw), 1)

    head_ids = [(gi, hh) for gi in range(ngroups) for hh in range(hpg)]
    v_of = lambda h, c: v_ref[c * CHUNK:(c + 1) * CHUNK, h * dv:(h + 1) * dv]
    state = {(gi, hh): st_ref[gi * hpg + hh] for gi, hh in head_ids}
    probs, qin, start_state, prepared = {}, {}, {}, {}
    blk = lambda a, r: a[SUB * r:SUB * (r + 1), :]

    def prep(c):
        r0 = c * CHUNK
        rws = slice(r0, r0 + CHUNK)
        for gi in range(ngroups):
            lanes = slice(gi * gw, (gi + 1) * gw)
            b = b_ref[rws, lanes]
            q = q_ref[rws, lanes]
            k = k_ref[rws, lanes]
            ends = {r: b_ref[r0 + SUB * r + SUB - 1:r0 + SUB * r + SUB, lanes] for r in range(NLAG)}
            ends[-1] = jnp.zeros((1, gw), F32)
            rows_of = lambda f: jnp.concatenate(
                [jnp.broadcast_to(f(r), (SUB, gw)) for r in range(NLAG)], axis=0)

            q0 = q * jnp.exp2(b - rows_of(lambda r: ends[r]))
            q1 = q * jnp.exp2(b - rows_of(lambda r: ends[r - 1]))
            k_hat = k * jnp.exp2(rows_of(lambda r: ends[r]) - b)
            lag_rows = [q0.astype(BF16), q1[SUB:, :].astype(BF16)]
            for l in range(2, NLAG):
                lag_rows.append(jnp.concatenate(
                    [blk(q1, r) * jnp.exp2(ends[r - 1] - ends[r - l]) for r in range(l, NLAG)],
                    axis=0).astype(BF16))
            q_lag = jnp.concatenate(lag_rows, axis=0)
            q_in = jnp.concatenate(
                [blk(q1, r) * jnp.exp2(ends[r - 1]) for r in range(NLAG)], axis=0)
            k_dec = jnp.concatenate(
                [blk(k_hat, r) * jnp.exp2(ends[NLAG - 1] - ends[r]) for r in range(NLAG)],
                axis=0).astype(BF16)
            dec = jnp.exp2(ends[NLAG - 1])
            prepared[gi, c] = (q_lag, k_dec, dec)
            for hh in range(hpg):
                if hpg == 1:
                    prepared[gi, hh, c], qin[gi, hh, c] = k_hat.astype(BF16), q_in.astype(BF16)
                else:
                    in_head = (lane >= hh * dk) & (lane < (hh + 1) * dk)
                    prepared[gi, hh, c] = jnp.where(in_head, k_hat, 0.0).astype(BF16)
                    qin[gi, hh, c] = jnp.where(in_head, q_in, 0.0).astype(BF16)

    def scores_and_state(c):
        for gi in range(ngroups):
            q_lag, k_dec, dec = prepared.pop((gi, c))
            for hh in range(hpg):
                h = gi * hpg + hh
                k_h = prepared.pop((gi, hh, c))
                s_all = lax.dot_general(q_lag, k_h, (((1,), (1,)), ((), ())),
                                        preferred_element_type=F32)
                p_rows = []
                for r in range(NLAG):
                    p_r = jnp.where(diag_block[r], blk(s_all, r), 0.0)
                    for l in range(1, r + 1):
                        src = LAG_OFFSETS[l] + SUB * (r - l)
                        p_r = jnp.where(col_block[r - l], s_all[src:src + SUB, :], p_r)
                    p_rows.append(p_r)
                probs[gi, hh, c] = jnp.concatenate(p_rows, axis=0).astype(BF16)
                upd = lax.dot_general(v_of(h, c), k_dec, (((0,), (0,)), ((), ())),
                                      preferred_element_type=F32)
                start_state[gi, hh, c] = state[gi, hh].astype(BF16).T
                state[gi, hh] = state[gi, hh] * dec + upd

    def outputs(c):
        for gi, hh in head_ids:
            h = gi * hpg + hh
            lhs = jnp.concatenate([qin.pop((gi, hh, c)), probs.pop((gi, hh, c))], axis=1)
            rhs = jnp.concatenate([start_state.pop((gi, hh, c)), v_of(h, c)], axis=0)
            o = jnp.dot(lhs, rhs, preferred_element_type=F32)
            rws, cols = slice(c * CHUNK, (c + 1) * CHUNK), slice(h * dv, (h + 1) * dv)
            y = o * lax.rsqrt(jnp.mean(o * o, axis=-1, keepdims=True) + EPS) * gate_ref[rws, cols]
            y_ref[rws, y_col0 + h * dv:y_col0 + (h + 1) * dv] = y.astype(BF16)

    def store_state():
        for gi, hh in head_ids:
            st_ref[gi * hpg + hh] = state[gi, hh]

    return cumsum_block, prep, scores_and_state, outputs, store_state


_MIXERS = (dict(heads=GLA_HEADS, dk=GLA_DK, dv=GLA_DV), dict(heads=HGRN_HEADS, dk=HGRN_DF, dv=HGRN_DV))


def _mixer_kernel(x_ref, wn_ref, wt_ref, wgk_ref, bgk_ref, lbl_ref, nwa_ref, nwb_ref, *rest,
                  n_cast):
    cast_in, rest = rest[:n_cast], rest[n_cast:]
    y_ref, rest = rest[0], rest[1:]
    cast_out, rest = rest[:n_cast], rest[n_cast:]
    (wa_ref, wbraw_ref, wb_ref,
     qa_ref, ka_ref, ga_ref, va_ref, gga_ref, ba_ref, sta_ref,
     qb_ref, kb_ref, gb_ref, vb_ref, ggb_ref, bb_ref, stb_ref) = rest

    @pl.when((pl.program_id(0) == 0) & (pl.program_id(1) == 0))
    def _():
        _transpose_projection_weight(wt_ref, wa_ref, wbraw_ref, wb_ref)

    @pl.when(pl.program_id(1) == 0)
    def _():
        sta_ref[...] = jnp.zeros_like(sta_ref)
        stb_ref[...] = jnp.zeros_like(stb_ref)

    for src, dst in zip(cast_in, cast_out):
        dst[...] = src[...].astype(BF16)

    decays_and_operands, output_gates = _projection_program(
        x_ref, wn_ref, wa_ref, wbraw_ref, wb_ref, wgk_ref, bgk_ref, lbl_ref,
        dict(q=qa_ref, k=ka_ref, g=ga_ref, v=va_ref, gate=gga_ref, gain=nwa_ref),
        dict(q=qb_ref, k=kb_ref, g=gb_ref, v=vb_ref, gate=ggb_ref, gain=nwb_ref))
    programs = [
        _mixer_program(qa_ref, ka_ref, ga_ref, va_ref, gga_ref, y_ref, 0,
                       ba_ref, sta_ref, **_MIXERS[0]),
        _mixer_program(qb_ref, kb_ref, gb_ref, vb_ref, ggb_ref, y_ref, GLA_V,
                       bb_ref, stb_ref, **_MIXERS[1]),
    ]
    rows = x_ref.shape[0]
    nchunk = rows // CHUNK

    decays_and_operands()
    for cumsum_block, _, _, _, _ in programs:
        for r0 in range(0, rows, CUMSUM_ROWS):
            cumsum_block(r0)
    output_gates()
    for c in range(nchunk):
        for _, prep, _, _, _ in programs:
            prep(c)
    for _, _, scores_and_state, outputs, store_state in programs:
        for c in range(nchunk):
            scores_and_state(c)
            if c >= 1:
                outputs(c - 1)
        outputs(nchunk - 1)
        store_state()


def _mixer_call(x2, wn, w_t, w_gk, bgk, lbl, nwa, nwb, cast_weights, *, batch):
    m = x2.shape[0]
    tm = MIXER_ROWS
    nt = m // batch // tm
    steps = batch * nt
    row = lambda width: pl.BlockSpec((tm, width), lambda b, i: (b * nt + i, 0))
    full = lambda a: pl.BlockSpec(a.shape, lambda b, i: (0,) * a.ndim)
    resident = lambda a: pl.BlockSpec(a.shape, lambda b, i: (0,) * a.ndim,
                                      pipeline_mode=pl.Buffered(1))
    slab = lambda a: pl.BlockSpec((a.shape[0] // steps, a.shape[1]), lambda b, i: (b * nt + i, 0))
    out_shapes = (jax.ShapeDtypeStruct((m, GLA_V + HGRN_V), BF16),)
    cast_shapes = tuple(jax.ShapeDtypeStruct(w.shape, BF16) for w in cast_weights)
    scratch = [pltpu.VMEM((D_MODEL, GLA_COLS), BF16), pltpu.VMEM((D_MODEL, LANES), BF16),
               pltpu.VMEM((D_MODEL, HGRN_COLS), BF16)]
    for mix in _MIXERS:
        w, gw = mix["heads"] * mix["dk"], max(mix["dk"], LANES)
        wv = mix["heads"] * mix["dv"]
        scratch += [pltpu.VMEM((tm, w), F32), pltpu.VMEM((tm, w), F32), pltpu.VMEM((tm, w), F32),
                    pltpu.VMEM((tm, wv), BF16), pltpu.VMEM((tm, wv), F32),
                    pltpu.VMEM((tm, w), F32), pltpu.VMEM((mix["heads"], mix["dv"], gw), F32)]
    outs = pl.pallas_call(
        functools.partial(_mixer_kernel, n_cast=len(cast_weights)),
        grid=(batch, nt),
        in_specs=[row(D_MODEL), full(wn), resident(w_t), full(w_gk), full(bgk), full(lbl),
                  full(nwa), full(nwb)] + [slab(w) for w in cast_weights],
        out_specs=tuple(row(s.shape[1]) for s in out_shapes) + tuple(slab(w) for w in cast_weights),
        out_shape=out_shapes + cast_shapes,
        scratch_shapes=scratch,
        compiler_params=pltpu.CompilerParams(
            dimension_semantics=("arbitrary", "arbitrary"), vmem_limit_bytes=VMEM_LIMIT_BYTES),
        name="proj_gla_hgrn_mixers",
    )(x2, wn, w_t, w_gk, bgk, lbl, nwa, nwb, *cast_weights)
    return outs[0], outs[1:]


def _tail_kernel(x_ref, y_ref, w_out_ref, post_mix_ref, pre_mlp_ref, post_mlp_ref,
                 w_up_ref, w_down_ref, out_ref):
    sub = x_ref.shape[0] // TAIL_SUBTILES
    tiles = [slice(s * sub, (s + 1) * sub) for s in range(TAIL_SUBTILES)]
    hs, us = {}, {}

    def out_proj(s):
        r = tiles[s]
        mix = jnp.dot(y_ref[r, :], w_out_ref[...], preferred_element_type=F32)
        hs[s] = x_ref[r, :] + _rmsnorm(mix, post_mix_ref[...])

    def up(s):
        us[s] = jnp.dot(_rmsnorm(hs[s], pre_mlp_ref[...]).astype(BF16), w_up_ref[...],
                        preferred_element_type=F32)

    def down(s):
        relu = jnp.maximum(us.pop(s), 0.0)
        m = jnp.dot((relu * relu).astype(BF16), w_down_ref[...], preferred_element_type=F32)
        out_ref[tiles[s], :] = hs.pop(s) + _rmsnorm(m, post_mlp_ref[...])

    for stage in (out_proj, up, down):
        for s in range(TAIL_SUBTILES):
            stage(s)


def _tail_call(x2, y, w_out, post_mix, pre_mlp, post_mlp, w_up, w_down):
    m = x2.shape[0]
    tm = TAIL_ROWS
    row = lambda width: pl.BlockSpec((tm, width), lambda i: (i, 0))
    full = lambda a: pl.BlockSpec(a.shape, lambda i: (0,) * a.ndim)
    resident = lambda a: pl.BlockSpec(a.shape, lambda i: (0,) * a.ndim,
                                      pipeline_mode=pl.Buffered(1))
    return pl.pallas_call(
        _tail_kernel,
        grid=(m // tm,),
        in_specs=[row(D_MODEL), row(GLA_V + HGRN_V), resident(w_out), full(post_mix), full(pre_mlp),
                  full(post_mlp), resident(w_up), resident(w_down)],
        out_specs=row(D_MODEL),
        out_shape=jax.ShapeDtypeStruct((m, D_MODEL), F32),
        compiler_params=pltpu.CompilerParams(
            dimension_semantics=("parallel",), vmem_limit_bytes=VMEM_LIMIT_BYTES),
        name="out_proj_mlp",
    )(x2, y, w_out, post_mix, pre_mlp, post_mlp, w_up, w_down)


def kernel(x, w_in, w_gk_up, b_gk, gla_norm_w, hgrn_norm_w, hgrn_lower_bounds, w_out,
           pre_mix_norm, post_mix_norm, pre_mlp_norm, post_mlp_norm, w_up, w_down):
    batch, seq, d = x.shape
    x2 = x.reshape(batch * seq, d)
    l = 0
    w_t = jnp.transpose(w_in[l])
    row2 = lambda a: a.reshape(1, -1)

    y, (w_out_b, w_up_b, w_down_b) = _mixer_call(
        x2, row2(pre_mix_norm[l]), w_t, w_gk_up[l], row2(b_gk[l]), hgrn_lower_bounds,
        row2(gla_norm_w[l]), row2(hgrn_norm_w[l]), (w_out[l], w_up[l], w_down[l]), batch=batch)
    out = _tail_call(x2, y, w_out_b, row2(post_mix_norm[l]), row2(pre_mlp_norm[l]),
                     row2(post_mlp_norm[l]), w_up_b, w_down_b)
    return out.reshape(batch, seq, d)
```

```python
import functools

import jax
import jax.numpy as jnp
from jax import lax
from jax.experimental import pallas as pl
from jax.experimental.pallas import tpu as pltpu

F32 = jnp.float32
BF16 = jnp.bfloat16

D_MODEL = 1024
GLA_HEADS, GLA_DK, GLA_DV = 4, 64, 128
GLA_QK = GLA_HEADS * GLA_DK
GLA_V = GLA_HEADS * GLA_DV
GLA_GATE_RANK = 16
GLA_GATE_NORM = 16.0
HGRN_HEADS, HGRN_DF, HGRN_DV = 4, 128, 128
HGRN_F = HGRN_HEADS * HGRN_DF
HGRN_V = HGRN_HEADS * HGRN_DV
D_FF = 4 * D_MODEL
EPS = 1e-6
LOG2_E = 1.4426950408889634

LANES = 128
VMEM_LIMIT_BYTES = 56 * 1024 * 1024

_C_GQ = 0
_C_GK = _C_GQ + GLA_QK
_C_GV = _C_GK + GLA_QK
_C_GG = _C_GV + GLA_V
GLA_COLS = _C_GG + GLA_V
_C_HQ = 0
_C_HF = _C_HQ + HGRN_F
_C_HI = _C_HF + HGRN_F
_C_HG = _C_HI + HGRN_V
HGRN_COLS = _C_HG + HGRN_V

MIXER_ROWS = 512
TRANSPOSE_ROWS = 512
CUMSUM_ROWS = 256
CHUNK = 64
SUB = 16
NLAG = CHUNK // SUB
LAG_OFFSETS = [sum(CHUNK - SUB * m for m in range(l)) for l in range(NLAG + 1)]
LAG_ROWS = LAG_OFFSETS[NLAG]
TAIL_ROWS = 1024
TAIL_SUBTILES = 2


def _rmsnorm(x, w):
    return x * lax.rsqrt(jnp.mean(x * x, axis=-1, keepdims=True) + EPS) * w


def _sigmoid(x):
    return 1.0 / (1.0 + jnp.exp2(x * (-LOG2_E)))


def _silu(x):
    return x * _sigmoid(x)


def _log_sigmoid(x):
    return -(jnp.maximum(-x, 0.0) + jnp.log(1.0 + jnp.exp2(jnp.abs(x) * (-LOG2_E))))


def _transpose_projection_weight(wt_ref, wa_ref, wbraw_ref, wb_ref):
    turn = lambda lo, n: wt_ref[lo:lo + n, :].astype(BF16).T
    for r in range(0, GLA_COLS, TRANSPOSE_ROWS):
        wa_ref[:, r:r + TRANSPOSE_ROWS] = turn(r, TRANSPOSE_ROWS)
    wbraw_ref[...] = turn(GLA_COLS, LANES)
    hgrn0 = GLA_COLS + GLA_GATE_RANK
    for r in range(0, HGRN_COLS, TRANSPOSE_ROWS):
        wb_ref[:, r:r + TRANSPOSE_ROWS] = turn(hgrn0 + r, TRANSPOSE_ROWS)


def _projection_program(x_ref, wn_ref, wa_ref, wbraw_ref, wb_ref, wgk_ref, bgk_ref, lbl_ref,
                        gla, hgrn):
    hb = _rmsnorm(x_ref[...], wn_ref[...]).astype(BF16)

    def proj(w_ref, lo, width):
        return jnp.dot(hb, w_ref[:, lo:lo + width], preferred_element_type=F32)

    def decays_and_operands():
        lbl = lbl_ref[...]
        e = jnp.exp(lbl - jnp.max(lbl, axis=0, keepdims=True))
        lb = e[0:1, :] / jnp.sum(e, axis=0, keepdims=True)
        f = lb + (1.0 - lb) * _sigmoid(proj(wb_ref, _C_HF, HGRN_F))
        hgrn["k"][...] = 1.0 - f
        hgrn["g"][...] = jnp.log(f) * LOG2_E
        glr = proj(wbraw_ref, 0, LANES)[:, :GLA_GATE_RANK].astype(BF16)
        gla["q"][...] = proj(wa_ref, _C_GQ, GLA_QK) * (GLA_DK ** -0.5)
        gla["k"][...] = proj(wa_ref, _C_GK, GLA_QK)
        gla["v"][...] = proj(wa_ref, _C_GV, GLA_V).astype(BF16)
        gk = jnp.dot(glr, wgk_ref[...].astype(BF16), preferred_element_type=F32) + bgk_ref[...]
        gla["g"][...] = _log_sigmoid(gk) * (LOG2_E / GLA_GATE_NORM)
        hgrn["q"][...] = _silu(proj(wb_ref, _C_HQ, HGRN_F))
        hgrn["v"][...] = proj(wb_ref, _C_HI, HGRN_V).astype(BF16)

    def output_gates():
        gla["gate"][...] = gla["gain"][...] * _silu(proj(wa_ref, _C_GG, GLA_V))
        hgrn["gate"][...] = hgrn["gain"][...] * _silu(proj(wb_ref, _C_HG, HGRN_V))

    return decays_and_operands, output_gates


def _mixer_program(q_ref, k_ref, g_ref, v_ref, gate_ref, y_ref, y_col0, b_ref, st_ref, *,
                   heads, dk, dv):
    gw = max(dk, LANES)
    hpg = gw // dk
    ngroups = heads // hpg

    ri = lax.broadcasted_iota(jnp.int32, (CUMSUM_ROWS, CUMSUM_ROWS), 0)
    ci = lax.broadcasted_iota(jnp.int32, (CUMSUM_ROWS, CUMSUM_ROWS), 1)
    tri = jnp.where((ri // CHUNK == ci // CHUNK) & (ci <= ri), 1.0, 0.0).astype(BF16)

    def cumsum_block(r0):
        blk = slice(r0, r0 + CUMSUM_ROWS)
        g = g_ref[blk, :]
        g_hi = g.astype(BF16)
        g_lo = (g - g_hi.astype(F32)).astype(BF16)
        b_ref[blk, :] = (jnp.dot(tri, g_hi, preferred_element_type=F32)
                         + jnp.dot(tri, g_lo, preferred_element_type=F32))

    si = lax.broadcasted_iota(jnp.int32, (SUB, CHUNK), 0)
    sj = lax.broadcasted_iota(jnp.int32, (SUB, CHUNK), 1)
    col_block = [sj // SUB == r for r in range(NLAG)]
    diag_block = [col_block[r] & (sj - SUB * r <= si) for r in range(NLAG)]
    lane = lax.broadcasted_iota(jnp.int32, (1, gw), 1)
    feat = lax.broadcasted_iota(jnp.int32, (gw, CHUNK), 0)
    feat_in_head = [(feat >= hh * dk) & (feat < (hh + 1) * dk) for hh in range(hpg)]

    head_ids = [(gi, hh) for gi in range(ngroups) for hh in range(hpg)]
    v_of = lambda h, c: v_ref[c * CHUNK:(c + 1) * CHUNK, h * dv:(h + 1) * dv]
    state = {(gi, hh): st_ref[gi * hpg + hh] for gi, hh in head_ids}
    probs, qin, start_state, prepared = {}, {}, {}, {}
    blk = lambda a, r: a[SUB * r:SUB * (r + 1), :]

    def prep(c):
        r0 = c * CHUNK
        rws = slice(r0, r0 + CHUNK)
        for gi in range(ngroups):
            lanes = slice(gi * gw, (gi + 1) * gw)
            b = b_ref[rws, lanes]
            q = q_ref[rws, lanes]
            k = k_ref[rws, lanes]
            ends = {r: b_ref[r0 + SUB * r + SUB - 1:r0 + SUB * r + SUB, lanes] for r in range(NLAG)}
            ends[-1] = jnp.zeros((1, gw), F32)
            rows_of = lambda f: jnp.concatenate(
                [jnp.broadcast_to(f(r), (SUB, gw)) for r in range(NLAG)], axis=0)

            q0 = q * jnp.exp2(b - rows_of(lambda r: ends[r]))
            q1 = q * jnp.exp2(b - rows_of(lambda r: ends[r - 1]))
            k_hat = k * jnp.exp2(rows_of(lambda r: ends[r]) - b)
            lag_rows = [q0.astype(BF16), q1[SUB:, :].astype(BF16)]
            for l in range(2, NLAG):
                lag_rows.append(jnp.concatenate(
                    [blk(q1, r) * jnp.exp2(ends[r - 1] - ends[r - l]) for r in range(l, NLAG)],
                    axis=0).astype(BF16))
            q_lag = jnp.concatenate(lag_rows, axis=0)
            q_in = jnp.concatenate(
                [blk(q1, r) * jnp.exp2(ends[r - 1]) for r in range(NLAG)], axis=0)
            k_dec = jnp.concatenate(
                [blk(k_hat, r) * jnp.exp2(ends[NLAG - 1] - ends[r]) for r in range(NLAG)],
                axis=0).astype(BF16)
            dec = jnp.exp2(ends[NLAG - 1])
            prepared[gi, c] = (q_lag, k_dec, dec)
            k_hat_t = k_hat.astype(BF16).T
            for hh in range(hpg):
                if hpg == 1:
                    prepared[gi, hh, c], qin[gi, hh, c] = k_hat_t, q_in.astype(BF16)
                else:
                    in_head = (lane >= hh * dk) & (lane < (hh + 1) * dk)
                    prepared[gi, hh, c] = jnp.where(feat_in_head[hh], k_hat_t, jnp.zeros_like(k_hat_t))
                    qin[gi, hh, c] = jnp.where(in_head, q_in, 0.0).astype(BF16)

    def scores_and_state(c):
        for gi in range(ngroups):
            q_lag, k_dec, dec = prepared.pop((gi, c))
            for hh in range(hpg):
                h = gi * hpg + hh
                k_h = prepared.pop((gi, hh, c))
                s_all = jnp.dot(q_lag, k_h, preferred_element_type=F32)
                p_rows = []
                for r in range(NLAG):
                    p_r = jnp.where(diag_block[r], blk(s_all, r), 0.0)
                    for l in range(1, r + 1):
                        src = LAG_OFFSETS[l] + SUB * (r - l)
                        p_r = jnp.where(col_block[r - l], s_all[src:src + SUB, :], p_r)
                    p_rows.append(p_r)
                probs[gi, hh, c] = jnp.concatenate(p_rows, axis=0).astype(BF16)
                upd = lax.dot_general(v_of(h, c), k_dec, (((0,), (0,)), ((), ())),
                                      preferred_element_type=F32)
                start_state[gi, hh, c] = state[gi, hh].astype(BF16).T
                state[gi, hh] = state[gi, hh] * dec + upd

    def outputs(c):
        for gi, hh in head_ids:
            h = gi * hpg + hh
            lhs = jnp.concatenate([qin.pop((gi, hh, c)), probs.pop((gi, hh, c))], axis=1)
            rhs = jnp.concatenate([start_state.pop((gi, hh, c)), v_of(h, c)], axis=0)
            o = jnp.dot(lhs, rhs, preferred_element_type=F32)
            rws, cols = slice(c * CHUNK, (c + 1) * CHUNK), slice(h * dv, (h + 1) * dv)
            y = o * lax.rsqrt(jnp.mean(o * o, axis=-1, keepdims=True) + EPS) * gate_ref[rws, cols]
            y_ref[rws, y_col0 + h * dv:y_col0 + (h + 1) * dv] = y.astype(BF16)

    def store_state():
        for gi, hh in head_ids:
            st_ref[gi * hpg + hh] = state[gi, hh]

    return cumsum_block, prep, scores_and_state, outputs, store_state


_MIXERS = (dict(heads=GLA_HEADS, dk=GLA_DK, dv=GLA_DV), dict(heads=HGRN_HEADS, dk=HGRN_DF, dv=HGRN_DV))


def _mixer_kernel(x_ref, wn_ref, wt_ref, wgk_ref, bgk_ref, lbl_ref, nwa_ref, nwb_ref, *rest,
                  n_cast):
    cast_in, rest = rest[:n_cast], rest[n_cast:]
    y_ref, rest = rest[0], rest[1:]
    cast_out, rest = rest[:n_cast], rest[n_cast:]
    (wa_ref, wbraw_ref, wb_ref,
     qa_ref, ka_ref, ga_ref, va_ref, gga_ref, ba_ref, sta_ref,
     qb_ref, kb_ref, gb_ref, vb_ref, ggb_ref, bb_ref, stb_ref) = rest

    @pl.when((pl.program_id(0) == 0) & (pl.program_id(1) == 0))
    def _():
        _transpose_projection_weight(wt_ref, wa_ref, wbraw_ref, wb_ref)

    @pl.when(pl.program_id(1) == 0)
    def _():
        sta_ref[...] = jnp.zeros_like(sta_ref)
        stb_ref[...] = jnp.zeros_like(stb_ref)

    for src, dst in zip(cast_in, cast_out):
        dst[...] = src[...].astype(BF16)

    decays_and_operands, output_gates = _projection_program(
        x_ref, wn_ref, wa_ref, wbraw_ref, wb_ref, wgk_ref, bgk_ref, lbl_ref,
        dict(q=qa_ref, k=ka_ref, g=ga_ref, v=va_ref, gate=gga_ref, gain=nwa_ref),
        dict(q=qb_ref, k=kb_ref, g=gb_ref, v=vb_ref, gate=ggb_ref, gain=nwb_ref))
    programs = [
        _mixer_program(qa_ref, ka_ref, ga_ref, va_ref, gga_ref, y_ref, 0,
                       ba_ref, sta_ref, **_MIXERS[0]),
        _mixer_program(qb_ref, kb_ref, gb_ref, vb_ref, ggb_ref, y_ref, GLA_V,
                       bb_ref, stb_ref, **_MIXERS[1]),
    ]
    rows = x_ref.shape[0]
    nchunk = rows // CHUNK

    decays_and_operands()
    for cumsum_block, _, _, _, _ in programs:
        for r0 in range(0, rows, CUMSUM_ROWS):
            cumsum_block(r0)
    output_gates()
    for c in range(nchunk):
        for _, prep, _, _, _ in programs:
            prep(c)
    for _, _, scores_and_state, outputs, store_state in programs:
        for c in range(nchunk):
            scores_and_state(c)
            if c >= 1:
                outputs(c - 1)
        outputs(nchunk - 1)
        store_state()


def _mixer_call(x2, wn, w_t, w_gk, bgk, lbl, nwa, nwb, cast_weights, *, batch):
    m = x2.shape[0]
    tm = MIXER_ROWS
    nt = m // batch // tm
    steps = batch * nt
    row = lambda width: pl.BlockSpec((tm, width), lambda b, i: (b * nt + i, 0))
    full = lambda a: pl.BlockSpec(a.shape, lambda b, i: (0,) * a.ndim)
    resident = lambda a: pl.BlockSpec(a.shape, lambda b, i: (0,) * a.ndim,
                                      pipeline_mode=pl.Buffered(1))
    slab = lambda a: pl.BlockSpec((a.shape[0] // steps, a.shape[1]), lambda b, i: (b * nt + i, 0))
    out_shapes = (jax.ShapeDtypeStruct((m, GLA_V + HGRN_V), BF16),)
    cast_shapes = tuple(jax.ShapeDtypeStruct(w.shape, BF16) for w in cast_weights)
    scratch = [pltpu.VMEM((D_MODEL, GLA_COLS), BF16), pltpu.VMEM((D_MODEL, LANES), BF16),
               pltpu.VMEM((D_MODEL, HGRN_COLS), BF16)]
    for mix in _MIXERS:
        w, gw = mix["heads"] * mix["dk"], max(mix["dk"], LANES)
        wv = mix["heads"] * mix["dv"]
        scratch += [pltpu.VMEM((tm, w), F32), pltpu.VMEM((tm, w), F32), pltpu.VMEM((tm, w), F32),
                    pltpu.VMEM((tm, wv), BF16), pltpu.VMEM((tm, wv), F32),
                    pltpu.VMEM((tm, w), F32), pltpu.VMEM((mix["heads"], mix["dv"], gw), F32)]
    outs = pl.pallas_call(
        functools.partial(_mixer_kernel, n_cast=len(cast_weights)),
        grid=(batch, nt),
        in_specs=[row(D_MODEL), full(wn), resident(w_t), full(w_gk), full(bgk), full(lbl),
                  full(nwa), full(nwb)] + [slab(w) for w in cast_weights],
        out_specs=tuple(row(s.shape[1]) for s in out_shapes) + tuple(slab(w) for w in cast_weights),
        out_shape=out_shapes + cast_shapes,
        scratch_shapes=scratch,
        compiler_params=pltpu.CompilerParams(
            dimension_semantics=("arbitrary", "arbitrary"), vmem_limit_bytes=VMEM_LIMIT_BYTES),
        name="proj_gla_hgrn_mixers",
    )(x2, wn, w_t, w_gk, bgk, lbl, nwa, nwb, *cast_weights)
    return outs[0], outs[1:]


def _tail_kernel(x_ref, y_ref, w_out_ref, post_mix_ref, pre_mlp_ref, post_mlp_ref,
                 w_up_ref, w_down_ref, out_ref):
    sub = x_ref.shape[0] // TAIL_SUBTILES
    tiles = [slice(s * sub, (s + 1) * sub) for s in range(TAIL_SUBTILES)]
    hs, us = {}, {}

    def out_proj(s):
        r = tiles[s]
        mix = jnp.dot(y_ref[r, :], w_out_ref[...], preferred_element_type=F32)
        hs[s] = x_ref[r, :] + _rmsnorm(mix, post_mix_ref[...])

    def up(s):
        us[s] = jnp.dot(_rmsnorm(hs[s], pre_mlp_ref[...]).astype(BF16), w_up_ref[...],
                        preferred_element_type=F32)

    def down(s):
        relu = jnp.maximum(us.pop(s), 0.0)
        m = jnp.dot((relu * relu).astype(BF16), w_down_ref[...], preferred_element_type=F32)
        out_ref[tiles[s], :] = hs.pop(s) + _rmsnorm(m, post_mlp_ref[...])

    for stage in (out_proj, up, down):
        for s in range(TAIL_SUBTILES):
            stage(s)


def _tail_call(x2, y, w_out, post_mix, pre_mlp, post_mlp, w_up, w_down):
    m = x2.shape[0]
    tm = TAIL_ROWS
    row = lambda width: pl.BlockSpec((tm, width), lambda i: (i, 0))
    full = lambda a: pl.BlockSpec(a.shape, lambda i: (0,) * a.ndim)
    resident = lambda a: pl.BlockSpec(a.shape, lambda i: (0,) * a.ndim,
                                      pipeline_mode=pl.Buffered(1))
    return pl.pallas_call(
        _tail_kernel,
        grid=(m // tm,),
        in_specs=[row(D_MODEL), row(GLA_V + HGRN_V), resident(w_out), full(post_mix), full(pre_mlp),
                  full(post_mlp), resident(w_up), resident(w_down)],
        out_specs=row(D_MODEL),
        out_shape=jax.ShapeDtypeStruct((m, D_MODEL), F32),
        compiler_params=pltpu.CompilerParams(
            dimension_semantics=("parallel",), vmem_limit_bytes=VMEM_LIMIT_BYTES),
        name="out_proj_mlp",
    )(x2, y, w_out, post_mix, pre_mlp, post_mlp, w_up, w_down)


def kernel(x, w_in, w_gk_up, b_gk, gla_norm_w, hgrn_norm_w, hgrn_lower_bounds, w_out,
           pre_mix_norm, post_mix_norm, pre_mlp_norm, post_mlp_norm, w_up, w_down):
    batch, seq, d = x.shape
    x2 = x.reshape(batch * seq, d)
    l = 0
    w_t = jnp.transpose(w_in[l])
    row2 = lambda a: a.reshape(1, -1)

    y, (w_out_b, w_up_b, w_down_b) = _mixer_call(
        x2, row2(pre_mix_norm[l]), w_t, w_gk_up[l], row2(b_gk[l]), hgrn_lower_bounds,
        row2(gla_norm_w[l]), row2(hgrn_norm_w[l]), (w_out[l], w_up[l], w_down[l]), batch=batch)
    out = _tail_call(x2, y, w_out_b, row2(post_mix_norm[l]), row2(pre_mlp_norm[l]),
                     row2(post_mlp_norm[l]), w_up_b, w_down_b)
    return out.reshape(batch, seq, d)
```

```python
import functools

import jax
import jax.numpy as jnp
from jax import lax
from jax.experimental import pallas as pl
from jax.experimental.pallas import tpu as pltpu

F32 = jnp.float32
BF16 = jnp.bfloat16

D_MODEL = 1024
GLA_HEADS, GLA_DK, GLA_DV = 4, 64, 128
GLA_QK = GLA_HEADS * GLA_DK
GLA_V = GLA_HEADS * GLA_DV
GLA_GATE_RANK = 16
GLA_GATE_NORM = 16.0
HGRN_HEADS, HGRN_DF, HGRN_DV = 4, 128, 128
HGRN_F = HGRN_HEADS * HGRN_DF
HGRN_V = HGRN_HEADS * HGRN_DV
D_FF = 4 * D_MODEL
EPS = 1e-6
LOG2_E = 1.4426950408889634

LANES = 128
VMEM_LIMIT_BYTES = 56 * 1024 * 1024

_C_GQ = 0
_C_GK = _C_GQ + GLA_QK
_C_GV = _C_GK + GLA_QK
_C_GG = _C_GV + GLA_V
GLA_COLS = _C_GG + GLA_V
_C_HQ = 0
_C_HF = _C_HQ + HGRN_F
_C_HI = _C_HF + HGRN_F
_C_HG = _C_HI + HGRN_V
HGRN_COLS = _C_HG + HGRN_V

MIXER_ROWS = 512
TRANSPOSE_ROWS = 512
CUMSUM_ROWS = 256
CHUNK = 64
SUB = 16
NLAG = CHUNK // SUB
LAG_OFFSETS = [sum(CHUNK - SUB * m for m in range(l)) for l in range(NLAG + 1)]
LAG_ROWS = LAG_OFFSETS[NLAG]
TAIL_ROWS = 1024
TAIL_SUBTILES = 4


def _rmsnorm(x, w):
    return x * lax.rsqrt(jnp.mean(x * x, axis=-1, keepdims=True) + EPS) * w


def _sigmoid(x):
    return 1.0 / (1.0 + jnp.exp2(x * (-LOG2_E)))


def _silu(x):
    return x * _sigmoid(x)


def _log_sigmoid(x):
    return -(jnp.maximum(-x, 0.0) + jnp.log(1.0 + jnp.exp2(jnp.abs(x) * (-LOG2_E))))


def _transpose_projection_weight(wt_ref, wa_ref, wbraw_ref, wb_ref):
    turn = lambda lo, n: wt_ref[lo:lo + n, :].astype(BF16).T
    for r in range(0, GLA_COLS, TRANSPOSE_ROWS):
        wa_ref[:, r:r + TRANSPOSE_ROWS] = turn(r, TRANSPOSE_ROWS)
    wbraw_ref[...] = turn(GLA_COLS, LANES)
    hgrn0 = GLA_COLS + GLA_GATE_RANK
    for r in range(0, HGRN_COLS, TRANSPOSE_ROWS):
        wb_ref[:, r:r + TRANSPOSE_ROWS] = turn(hgrn0 + r, TRANSPOSE_ROWS)


def _projection_program(x_ref, wn_ref, wa_ref, wbraw_ref, wb_ref, wgk_ref, bgk_ref, lbl_ref,
                        gla, hgrn):
    hb = _rmsnorm(x_ref[...], wn_ref[...]).astype(BF16)

    def proj(w_ref, lo, width):
        return jnp.dot(hb, w_ref[:, lo:lo + width], preferred_element_type=F32)

    def decays_and_operands():
        lbl = lbl_ref[...]
        e = jnp.exp(lbl - jnp.max(lbl, axis=0, keepdims=True))
        lb = e[0:1, :] / jnp.sum(e, axis=0, keepdims=True)
        f = lb + (1.0 - lb) * _sigmoid(proj(wb_ref, _C_HF, HGRN_F))
        hgrn["k"][...] = 1.0 - f
        hgrn["g"][...] = jnp.log(f) * LOG2_E
        glr = proj(wbraw_ref, 0, LANES)[:, :GLA_GATE_RANK].astype(BF16)
        gla["q"][...] = proj(wa_ref, _C_GQ, GLA_QK) * (GLA_DK ** -0.5)
        gla["k"][...] = proj(wa_ref, _C_GK, GLA_QK)
        gla["v"][...] = proj(wa_ref, _C_GV, GLA_V).astype(BF16)
        gk = jnp.dot(glr, wgk_ref[...].astype(BF16), preferred_element_type=F32) + bgk_ref[...]
        gla["g"][...] = _log_sigmoid(gk) * (LOG2_E / GLA_GATE_NORM)
        hgrn["q"][...] = _silu(proj(wb_ref, _C_HQ, HGRN_F))
        hgrn["v"][...] = proj(wb_ref, _C_HI, HGRN_V).astype(BF16)

    def output_gates():
        gla["gate"][...] = gla["gain"][...] * _silu(proj(wa_ref, _C_GG, GLA_V))
        hgrn["gate"][...] = hgrn["gain"][...] * _silu(proj(wb_ref, _C_HG, HGRN_V))

    return decays_and_operands, output_gates


def _mixer_program(q_ref, k_ref, g_ref, v_ref, gate_ref, y_ref, y_col0, b_ref, st_ref, *,
                   heads, dk, dv):
    gw = max(dk, LANES)
    hpg = gw // dk
    ngroups = heads // hpg

    ri = lax.broadcasted_iota(jnp.int32, (CUMSUM_ROWS, CUMSUM_ROWS), 0)
    ci = lax.broadcasted_iota(jnp.int32, (CUMSUM_ROWS, CUMSUM_ROWS), 1)
    tri = jnp.where((ri // CHUNK == ci // CHUNK) & (ci <= ri), 1.0, 0.0).astype(BF16)

    def cumsum_block(r0):
        blk = slice(r0, r0 + CUMSUM_ROWS)
        g = g_ref[blk, :]
        g_hi = g.astype(BF16)
        g_lo = (g - g_hi.astype(F32)).astype(BF16)
        b_ref[blk, :] = (jnp.dot(tri, g_hi, preferred_element_type=F32)
                         + jnp.dot(tri, g_lo, preferred_element_type=F32))

    si = lax.broadcasted_iota(jnp.int32, (SUB, CHUNK), 0)
    sj = lax.broadcasted_iota(jnp.int32, (SUB, CHUNK), 1)
    col_block = [sj // SUB == r for r in range(NLAG)]
    diag_block = [col_block[r] & (sj - SUB * r <= si) for r in range(NLAG)]
    lane = lax.broadcasted_iota(jnp.int32, (1, gw), 1)

    head_ids = [(gi, hh) for gi in range(ngroups) for hh in range(hpg)]
    v_of = lambda h, c: v_ref[c * CHUNK:(c + 1) * CHUNK, h * dv:(h + 1) * dv]
    state = {(gi, hh): st_ref[gi * hpg + hh] for gi, hh in head_ids}
    probs, qin, start_state, prepared = {}, {}, {}, {}
    blk = lambda a, r: a[SUB * r:SUB * (r + 1), :]

    def prep(c):
        r0 = c * CHUNK
        rws = slice(r0, r0 + CHUNK)
        for gi in range(ngroups):
            lanes = slice(gi * gw, (gi + 1) * gw)
            b = b_ref[rws, lanes]
            q = q_ref[rws, lanes]
            k = k_ref[rws, lanes]
            ends = {r: b_ref[r0 + SUB * r + SUB - 1:r0 + SUB * r + SUB, lanes] for r in range(NLAG)}
            ends[-1] = jnp.zeros((1, gw), F32)
            rows_of = lambda f: jnp.concatenate(
                [jnp.broadcast_to(f(r), (SUB, gw)) for r in range(NLAG)], axis=0)

            q0 = q * jnp.exp2(b - rows_of(lambda r: ends[r]))
            q1 = q * jnp.exp2(b - rows_of(lambda r: ends[r - 1]))
            k_hat = k * jnp.exp2(rows_of(lambda r: ends[r]) - b)
            lag_rows = [q0.astype(BF16), q1[SUB:, :].astype(BF16)]
            for l in range(2, NLAG):
                lag_rows.append(jnp.concatenate(
                    [blk(q1, r) * jnp.exp2(ends[r - 1] - ends[r - l]) for r in range(l, NLAG)],
                    axis=0).astype(BF16))
            q_lag = jnp.concatenate(lag_rows, axis=0)
            q_in = jnp.concatenate(
                [blk(q1, r) * jnp.exp2(ends[r - 1]) for r in range(NLAG)], axis=0)
            k_dec = jnp.concatenate(
                [blk(k_hat, r) * jnp.exp2(ends[NLAG - 1] - ends[r]) for r in range(NLAG)],
                axis=0).astype(BF16)
            dec = jnp.exp2(ends[NLAG - 1])
            prepared[gi, c] = (q_lag, k_dec, dec)
            for hh in range(hpg):
                if hpg == 1:
                    prepared[gi, hh, c], qin[gi, hh, c] = k_hat.astype(BF16), q_in.astype(BF16)
                else:
                    in_head = (lane >= hh * dk) & (lane < (hh + 1) * dk)
                    prepared[gi, hh, c] = jnp.where(in_head, k_hat, 0.0).astype(BF16)
                    qin[gi, hh, c] = jnp.where(in_head, q_in, 0.0).astype(BF16)

    def scores_and_state(c):
        for gi in range(ngroups):
            q_lag, k_dec, dec = prepared.pop((gi, c))
            for hh in range(hpg):
                h = gi * hpg + hh
                k_h = prepared.pop((gi, hh, c))
                s_all = lax.dot_general(q_lag, k_h, (((1,), (1,)), ((), ())),
                                        preferred_element_type=F32)
                p_rows = []
                for r in range(NLAG):
                    p_r = jnp.where(diag_block[r], blk(s_all, r), 0.0)
                    for l in range(1, r + 1):
                        src = LAG_OFFSETS[l] + SUB * (r - l)
                        p_r = jnp.where(col_block[r - l], s_all[src:src + SUB, :], p_r)
                    p_rows.append(p_r)
                probs[gi, hh, c] = jnp.concatenate(p_rows, axis=0).astype(BF16)
                upd = lax.dot_general(v_of(h, c), k_dec, (((0,), (0,)), ((), ())),
                                      preferred_element_type=F32)
                start_state[gi, hh, c] = state[gi, hh].astype(BF16).T
                state[gi, hh] = state[gi, hh] * dec + upd

    def outputs(c):
        for gi, hh in head_ids:
            h = gi * hpg + hh
            lhs = jnp.concatenate([qin.pop((gi, hh, c)), probs.pop((gi, hh, c))], axis=1)
            rhs = jnp.concatenate([start_state.pop((gi, hh, c)), v_of(h, c)], axis=0)
            o = jnp.dot(lhs, rhs, preferred_element_type=F32)
            rws, cols = slice(c * CHUNK, (c + 1) * CHUNK), slice(h * dv, (h + 1) * dv)
            y = o * lax.rsqrt(jnp.mean(o * o, axis=-1, keepdims=True) + EPS) * gate_ref[rws, cols]
            y_ref[rws, y_col0 + h * dv:y_col0 + (h + 1) * dv] = y.astype(BF16)

    def store_state():
        for gi, hh in head_ids:
            st_ref[gi * hpg + hh] = state[gi, hh]

    return cumsum_block, prep, scores_and_state, outputs, store_state


_MIXERS = (dict(heads=GLA_HEADS, dk=GLA_DK, dv=GLA_DV), dict(heads=HGRN_HEADS, dk=HGRN_DF, dv=HGRN_DV))


def _mixer_kernel(x_ref, wn_ref, wt_ref, wgk_ref, bgk_ref, lbl_ref, nwa_ref, nwb_ref, *rest,
                  n_cast):
    cast_in, rest = rest[:n_cast], rest[n_cast:]
    y_ref, rest = rest[0], rest[1:]
    cast_out, rest = rest[:n_cast], rest[n_cast:]
    (wa_ref, wbraw_ref, wb_ref,
     qa_ref, ka_ref, ga_ref, va_ref, gga_ref, ba_ref, sta_ref,
     qb_ref, kb_ref, gb_ref, vb_ref, ggb_ref, bb_ref, stb_ref) = rest

    @pl.when((pl.program_id(0) == 0) & (pl.program_id(1) == 0))
    def _():
        _transpose_projection_weight(wt_ref, wa_ref, wbraw_ref, wb_ref)

    @pl.when(pl.program_id(1) == 0)
    def _():
        sta_ref[...] = jnp.zeros_like(sta_ref)
        stb_ref[...] = jnp.zeros_like(stb_ref)

    for src, dst in zip(cast_in, cast_out):
        dst[...] = src[...].astype(BF16)

    decays_and_operands, output_gates = _projection_program(
        x_ref, wn_ref, wa_ref, wbraw_ref, wb_ref, wgk_ref, bgk_ref, lbl_ref,
        dict(q=qa_ref, k=ka_ref, g=ga_ref, v=va_ref, gate=gga_ref, gain=nwa_ref),
        dict(q=qb_ref, k=kb_ref, g=gb_ref, v=vb_ref, gate=ggb_ref, gain=nwb_ref))
    programs = [
        _mixer_program(qa_ref, ka_ref, ga_ref, va_ref, gga_ref, y_ref, 0,
                       ba_ref, sta_ref, **_MIXERS[0]),
        _mixer_program(qb_ref, kb_ref, gb_ref, vb_ref, ggb_ref, y_ref, GLA_V,
                       bb_ref, stb_ref, **_MIXERS[1]),
    ]
    rows = x_ref.shape[0]
    nchunk = rows // CHUNK

    decays_and_operands()
    for cumsum_block, _, _, _, _ in programs:
        for r0 in range(0, rows, CUMSUM_ROWS):
            cumsum_block(r0)
    output_gates()
    for c in range(nchunk):
        for _, prep, _, _, _ in programs:
            prep(c)
    for _, _, scores_and_state, outputs, store_state in programs:
        for c in range(nchunk):
            scores_and_state(c)
            if c >= 1:
                outputs(c - 1)
        outputs(nchunk - 1)
        store_state()


def _mixer_call(x2, wn, w_t, w_gk, bgk, lbl, nwa, nwb, cast_weights, *, batch):
    m = x2.shape[0]
    tm = MIXER_ROWS
    nt = m // batch // tm
    steps = batch * nt
    row = lambda width: pl.BlockSpec((tm, width), lambda b, i: (b * nt + i, 0))
    full = lambda a: pl.BlockSpec(a.shape, lambda b, i: (0,) * a.ndim)
    resident = lambda a: pl.BlockSpec(a.shape, lambda b, i: (0,) * a.ndim,
                                      pipeline_mode=pl.Buffered(1))
    slab = lambda a: pl.BlockSpec((a.shape[0] // steps, a.shape[1]), lambda b, i: (b * nt + i, 0))
    out_shapes = (jax.ShapeDtypeStruct((m, GLA_V + HGRN_V), BF16),)
    cast_shapes = tuple(jax.ShapeDtypeStruct(w.shape, BF16) for w in cast_weights)
    scratch = [pltpu.VMEM((D_MODEL, GLA_COLS), BF16), pltpu.VMEM((D_MODEL, LANES), BF16),
               pltpu.VMEM((D_MODEL, HGRN_COLS), BF16)]
    for mix in _MIXERS:
        w, gw = mix["heads"] * mix["dk"], max(mix["dk"], LANES)
        wv = mix["heads"] * mix["dv"]
        scratch += [pltpu.VMEM((tm, w), F32), pltpu.VMEM((tm, w), F32), pltpu.VMEM((tm, w), F32),
                    pltpu.VMEM((tm, wv), BF16), pltpu.VMEM((tm, wv), F32),
                    pltpu.VMEM((tm, w), F32), pltpu.VMEM((mix["heads"], mix["dv"], gw), F32)]
    outs = pl.pallas_call(
        functools.partial(_mixer_kernel, n_cast=len(cast_weights)),
        grid=(batch, nt),
        in_specs=[row(D_MODEL), full(wn), resident(w_t), full(w_gk), full(bgk), full(lbl),
                  full(nwa), full(nwb)] + [slab(w) for w in cast_weights],
        out_specs=tuple(row(s.shape[1]) for s in out_shapes) + tuple(slab(w) for w in cast_weights),
        out_shape=out_shapes + cast_shapes,
        scratch_shapes=scratch,
        compiler_params=pltpu.CompilerParams(
            dimension_semantics=("arbitrary", "arbitrary"), vmem_limit_bytes=VMEM_LIMIT_BYTES),
        name="proj_gla_hgrn_mixers",
    )(x2, wn, w_t, w_gk, bgk, lbl, nwa, nwb, *cast_weights)
    return outs[0], outs[1:]


def _tail_kernel(x_ref, y_ref, w_out_ref, post_mix_ref, pre_mlp_ref, post_mlp_ref,
                 w_up_ref, w_down_ref, out_ref):
    sub = x_ref.shape[0] // TAIL_SUBTILES
    tiles = [slice(s * sub, (s + 1) * sub) for s in range(TAIL_SUBTILES)]
    hs, us = {}, {}

    def out_proj(s):
        r = tiles[s]
        mix = jnp.dot(y_ref[r, :], w_out_ref[...], preferred_element_type=F32)
        hs[s] = x_ref[r, :] + _rmsnorm(mix, post_mix_ref[...])

    def up(s):
        us[s] = jnp.dot(_rmsnorm(hs[s], pre_mlp_ref[...]).astype(BF16), w_up_ref[...],
                        preferred_element_type=F32)

    def down(s):
        relu = jnp.maximum(us.pop(s), 0.0)
        m = jnp.dot((relu * relu).astype(BF16), w_down_ref[...], preferred_element_type=F32)
        out_ref[tiles[s], :] = hs.pop(s) + _rmsnorm(m, post_mlp_ref[...])

    for s in range(TAIL_SUBTILES):
        out_proj(s)
    for s in range(TAIL_SUBTILES):
        up(s)
        down(s)


def _tail_call(x2, y, w_out, post_mix, pre_mlp, post_mlp, w_up, w_down):
    m = x2.shape[0]
    tm = TAIL_ROWS
    row = lambda width: pl.BlockSpec((tm, width), lambda i: (i, 0))
    full = lambda a: pl.BlockSpec(a.shape, lambda i: (0,) * a.ndim)
    resident = lambda a: pl.BlockSpec(a.shape, lambda i: (0,) * a.ndim,
                                      pipeline_mode=pl.Buffered(1))
    return pl.pallas_call(
        _tail_kernel,
        grid=(m // tm,),
        in_specs=[row(D_MODEL), row(GLA_V + HGRN_V), resident(w_out), full(post_mix), full(pre_mlp),
                  full(post_mlp), resident(w_up), resident(w_down)],
        out_specs=row(D_MODEL),
        out_shape=jax.ShapeDtypeStruct((m, D_MODEL), F32),
        compiler_params=pltpu.CompilerParams(
            dimension_semantics=("parallel",), vmem_limit_bytes=VMEM_LIMIT_BYTES),
        name="out_proj_mlp",
    )(x2, y, w_out, post_mix, pre_mlp, post_mlp, w_up, w_down)


def kernel(x, w_in, w_gk_up, b_gk, gla_norm_w, hgrn_norm_w, hgrn_lower_bounds, w_out,
           pre_mix_norm, post_mix_norm, pre_mlp_norm, post_mlp_norm, w_up, w_down):
    batch, seq, d = x.shape
    x2 = x.reshape(batch * seq, d)
    l = 0
    w_t = jnp.transpose(w_in[l])
    row2 = lambda a: a.reshape(1, -1)

    y, (w_out_b, w_up_b, w_down_b) = _mixer_call(
        x2, row2(pre_mix_norm[l]), w_t, w_gk_up[l], row2(b_gk[l]), hgrn_lower_bounds,
        row2(gla_norm_w[l]), row2(hgrn_norm_w[l]), (w_out[l], w_up[l], w_down[l]), batch=batch)
    out = _tail_call(x2, y, w_out_b, row2(post_mix_norm[l]), row2(pre_mlp_norm[l]),
                     row2(post_mlp_norm[l]), w_up_b, w_down_b)
    return out.reshape(batch, seq, d)
```

```python
import functools

import jax
import jax.numpy as jnp
from jax import lax
from jax.experimental import pallas as pl
from jax.experimental.pallas import tpu as pltpu

F32 = jnp.float32
BF16 = jnp.bfloat16

D_MODEL = 1024
GLA_HEADS, GLA_DK, GLA_DV = 4, 64, 128
GLA_QK = GLA_HEADS * GLA_DK
GLA_V = GLA_HEADS * GLA_DV
GLA_GATE_RANK = 16
GLA_GATE_NORM = 16.0
HGRN_HEADS, HGRN_DF, HGRN_DV = 4, 128, 128
HGRN_F = HGRN_HEADS * HGRN_DF
HGRN_V = HGRN_HEADS * HGRN_DV
D_FF = 4 * D_MODEL
EPS = 1e-6
LOG2_E = 1.4426950408889634

LANES = 128
VMEM_LIMIT_BYTES = 56 * 1024 * 1024

_C_GQ = 0
_C_GK = _C_GQ + GLA_QK
_C_GV = _C_GK + GLA_QK
_C_GG = _C_GV + GLA_V
GLA_COLS = _C_GG + GLA_V
_C_HQ = 0
_C_HF = _C_HQ + HGRN_F
_C_HI = _C_HF + HGRN_F
_C_HG = _C_HI + HGRN_V
HGRN_COLS = _C_HG + HGRN_V

MIXER_ROWS = 512
TRANSPOSE_ROWS = 512
CUMSUM_ROWS = 256
CHUNK = 64
SUB = 16
NLAG = CHUNK // SUB
LAG_OFFSETS = [sum(CHUNK - SUB * m for m in range(l)) for l in range(NLAG + 1)]
LAG_ROWS = LAG_OFFSETS[NLAG]
TAIL_ROWS = 1024
TAIL_SUBTILES = 2


def _rmsnorm(x, w):
    return x * lax.rsqrt(jnp.mean(x * x, axis=-1, keepdims=True) + EPS) * w


def _sigmoid(x):
    return 1.0 / (1.0 + jnp.exp2(x * (-LOG2_E)))


def _silu(x):
    return x * _sigmoid(x)


def _log_sigmoid(x):
    return -(jnp.maximum(-x, 0.0) + jnp.log(1.0 + jnp.exp2(jnp.abs(x) * (-LOG2_E))))


def _transpose_projection_weight(wt_ref, wa_ref, wbraw_ref, wb_ref):
    turn = lambda lo, n: wt_ref[lo:lo + n, :].astype(BF16).T
    for r in range(0, GLA_COLS, TRANSPOSE_ROWS):
        wa_ref[:, r:r + TRANSPOSE_ROWS] = turn(r, TRANSPOSE_ROWS)
    wbraw_ref[...] = turn(GLA_COLS, LANES)
    hgrn0 = GLA_COLS + GLA_GATE_RANK
    for r in range(0, HGRN_COLS, TRANSPOSE_ROWS):
        wb_ref[:, r:r + TRANSPOSE_ROWS] = turn(hgrn0 + r, TRANSPOSE_ROWS)


def _projection_program(x_ref, wn_ref, wa_ref, wbraw_ref, wb_ref, wgk_ref, bgk_ref, lbl_ref,
                        gla, hgrn):
    hb = _rmsnorm(x_ref[...], wn_ref[...]).astype(BF16)

    def proj(w_ref, lo, width):
        return jnp.dot(hb, w_ref[:, lo:lo + width], preferred_element_type=F32)

    def decays_and_operands():
        lbl = lbl_ref[...]
        e = jnp.exp(lbl - jnp.max(lbl, axis=0, keepdims=True))
        lb = e[0:1, :] / jnp.sum(e, axis=0, keepdims=True)
        f = lb + (1.0 - lb) * _sigmoid(proj(wb_ref, _C_HF, HGRN_F))
        hgrn["k"][...] = 1.0 - f
        hgrn["g"][...] = jnp.log(f) * LOG2_E
        glr = proj(wbraw_ref, 0, LANES)[:, :GLA_GATE_RANK].astype(BF16)
        gla["q"][...] = proj(wa_ref, _C_GQ, GLA_QK) * (GLA_DK ** -0.5)
        gla["k"][...] = proj(wa_ref, _C_GK, GLA_QK)
        gla["v"][...] = proj(wa_ref, _C_GV, GLA_V).astype(BF16)
        gk = jnp.dot(glr, wgk_ref[...].astype(BF16), preferred_element_type=F32) + bgk_ref[...]
        gla["g"][...] = _log_sigmoid(gk) * (LOG2_E / GLA_GATE_NORM)
        hgrn["q"][...] = _silu(proj(wb_ref, _C_HQ, HGRN_F))
        hgrn["v"][...] = proj(wb_ref, _C_HI, HGRN_V).astype(BF16)

    def output_gates():
        gla["gate"][...] = gla["gain"][...] * _silu(proj(wa_ref, _C_GG, GLA_V))
        hgrn["gate"][...] = hgrn["gain"][...] * _silu(proj(wb_ref, _C_HG, HGRN_V))

    return decays_and_operands, output_gates


def _mixer_program(q_ref, k_ref, g_ref, v_ref, gate_ref, y_ref, y_col0, b_ref, st_ref, *,
                   heads, dk, dv):
    gw = max(dk, LANES)
    hpg = gw // dk
    ngroups = heads // hpg

    ri = lax.broadcasted_iota(jnp.int32, (CUMSUM_ROWS, CUMSUM_ROWS), 0)
    ci = lax.broadcasted_iota(jnp.int32, (CUMSUM_ROWS, CUMSUM_ROWS), 1)
    tri = jnp.where((ri // CHUNK == ci // CHUNK) & (ci <= ri), 1.0, 0.0).astype(BF16)

    def cumsum_block(r0):
        blk = slice(r0, r0 + CUMSUM_ROWS)
        g = g_ref[blk, :]
        g_hi = g.astype(BF16)
        g_lo = (g - g_hi.astype(F32)).astype(BF16)
        b_ref[blk, :] = (jnp.dot(tri, g_hi, preferred_element_type=F32)
                         + jnp.dot(tri, g_lo, preferred_element_type=F32))

    si = lax.broadcasted_iota(jnp.int32, (SUB, CHUNK), 0)
    sj = lax.broadcasted_iota(jnp.int32, (SUB, CHUNK), 1)
    col_block = [sj // SUB == r for r in range(NLAG)]
    diag_block = [col_block[r] & (sj - SUB * r <= si) for r in range(NLAG)]
    lane = lax.broadcasted_iota(jnp.int32, (1, gw), 1)

    head_ids = [(gi, hh) for gi in range(ngroups) for hh in range(hpg)]
    v_of = lambda h, c: v_ref[c * CHUNK:(c + 1) * CHUNK, h * dv:(h + 1) * dv]
    state = {(gi, hh): st_ref[gi * hpg + hh] for gi, hh in head_ids}
    probs, qin, start_state, prepared = {}, {}, {}, {}
    blk = lambda a, r: a[SUB * r:SUB * (r + 1), :]

    def prep(c):
        r0 = c * CHUNK
        rws = slice(r0, r0 + CHUNK)
        for gi in range(ngroups):
            lanes = slice(gi * gw, (gi + 1) * gw)
            b = b_ref[rws, lanes]
            q = q_ref[rws, lanes]
            k = k_ref[rws, lanes]
            ends = {r: b_ref[r0 + SUB * r + SUB - 1:r0 + SUB * r + SUB, lanes] for r in range(NLAG)}
            ends[-1] = jnp.zeros((1, gw), F32)
            rows_of = lambda f: jnp.concatenate(
                [jnp.broadcast_to(f(r), (SUB, gw)) for r in range(NLAG)], axis=0)

            q0 = q * jnp.exp2(b - rows_of(lambda r: ends[r]))
            q1 = q * jnp.exp2(b - rows_of(lambda r: ends[r - 1]))
            k_hat = k * jnp.exp2(rows_of(lambda r: ends[r]) - b)
            lag_rows = [q0.astype(BF16), q1[SUB:, :].astype(BF16)]
            for l in range(2, NLAG):
                lag_rows.append(jnp.concatenate(
                    [blk(q1, r) * jnp.exp2(ends[r - 1] - ends[r - l]) for r in range(l, NLAG)],
                    axis=0).astype(BF16))
            q_lag = jnp.concatenate(lag_rows, axis=0)
            q_in = jnp.concatenate(
                [blk(q1, r) * jnp.exp2(ends[r - 1]) for r in range(NLAG)], axis=0)
            k_dec = jnp.concatenate(
                [blk(k_hat, r) * jnp.exp2(ends[NLAG - 1] - ends[r]) for r in range(NLAG)],
                axis=0).astype(BF16)
            dec = jnp.exp2(ends[NLAG - 1])
            prepared[gi, c] = (q_lag, k_dec, dec)
            for hh in range(hpg):
                if hpg == 1:
                    prepared[gi, hh, c], qin[gi, hh, c] = k_hat.astype(BF16), q_in.astype(BF16)
                else:
                    in_head = (lane >= hh * dk) & (lane < (hh + 1) * dk)
                    prepared[gi, hh, c] = jnp.where(in_head, k_hat, 0.0).astype(BF16)
                    qin[gi, hh, c] = jnp.where(in_head, q_in, 0.0).astype(BF16)

    def scores_and_state(c):
        for gi in range(ngroups):
            q_lag, k_dec, dec = prepared.pop((gi, c))
            for hh in range(hpg):
                h = gi * hpg + hh
                k_h = prepared.pop((gi, hh, c))
                s_all = lax.dot_general(q_lag, k_h, (((1,), (1,)), ((), ())),
                                        preferred_element_type=F32)
                p_rows = []
                for r in range(NLAG):
                    p_r = jnp.where(diag_block[r], blk(s_all, r), 0.0)
                    for l in range(1, r + 1):
                        src = LAG_OFFSETS[l] + SUB * (r - l)
                        p_r = jnp.where(col_block[r - l], s_all[src:src + SUB, :], p_r)
                    p_rows.append(p_r)
                probs[gi, hh, c] = jnp.concatenate(p_rows, axis=0).astype(BF16)
                upd = lax.dot_general(v_of(h, c), k_dec, (((0,), (0,)), ((), ())),
                                      preferred_element_type=F32)
                start_state[gi, hh, c] = state[gi, hh].astype(BF16).T
                state[gi, hh] = state[gi, hh] * dec + upd

    def outputs(c):
        for gi, hh in head_ids:
            h = gi * hpg + hh
            lhs = jnp.concatenate([qin.pop((gi, hh, c)), probs.pop((gi, hh, c))], axis=1)
            rhs = jnp.concatenate([start_state.pop((gi, hh, c)), v_of(h, c)], axis=0)
            o = jnp.dot(lhs, rhs, preferred_element_type=F32)
            rws, cols = slice(c * CHUNK, (c + 1) * CHUNK), slice(h * dv, (h + 1) * dv)
            y = o * lax.rsqrt(jnp.mean(o * o, axis=-1, keepdims=True) + EPS) * gate_ref[rws, cols]
            y_ref[rws, y_col0 + h * dv:y_col0 + (h + 1) * dv] = y.astype(BF16)

    def store_state():
        for gi, hh in head_ids:
            st_ref[gi * hpg + hh] = state[gi, hh]

    return cumsum_block, prep, scores_and_state, outputs, store_state


_MIXERS = (dict(heads=GLA_HEADS, dk=GLA_DK, dv=GLA_DV), dict(heads=HGRN_HEADS, dk=HGRN_DF, dv=HGRN_DV))


def _mixer_kernel(x_ref, wn_ref, wt_ref, wgk_ref, bgk_ref, lbl_ref, nwa_ref, nwb_ref, *rest,
                  n_cast):
    cast_in, rest = rest[:n_cast], rest[n_cast:]
    y_ref, rest = rest[0], rest[1:]
    cast_out, rest = rest[:n_cast], rest[n_cast:]
    (wa_ref, wbraw_ref, wb_ref,
     qa_ref, ka_ref, ga_ref, va_ref, gga_ref, ba_ref, sta_ref,
     qb_ref, kb_ref, gb_ref, vb_ref, ggb_ref, bb_ref, stb_ref) = rest

    @pl.when((pl.program_id(0) == 0) & (pl.program_id(1) == 0))
    def _():
        _transpose_projection_weight(wt_ref, wa_ref, wbraw_ref, wb_ref)

    @pl.when(pl.program_id(1) == 0)
    def _():
        sta_ref[...] = jnp.zeros_like(sta_ref)
        stb_ref[...] = jnp.zeros_like(stb_ref)

    for src, dst in zip(cast_in, cast_out):
        dst[...] = src[...].astype(BF16)

    decays_and_operands, output_gates = _projection_program(
        x_ref, wn_ref, wa_ref, wbraw_ref, wb_ref, wgk_ref, bgk_ref, lbl_ref,
        dict(q=qa_ref, k=ka_ref, g=ga_ref, v=va_ref, gate=gga_ref, gain=nwa_ref),
        dict(q=qb_ref, k=kb_ref, g=gb_ref, v=vb_ref, gate=ggb_ref, gain=nwb_ref))
    programs = [
        _mixer_program(qa_ref, ka_ref, ga_ref, va_ref, gga_ref, y_ref, 0,
                       ba_ref, sta_ref, **_MIXERS[0]),
        _mixer_program(qb_ref, kb_ref, gb_ref, vb_ref, ggb_ref, y_ref, GLA_V,
                       bb_ref, stb_ref, **_MIXERS[1]),
    ]
    rows = x_ref.shape[0]
    nchunk = rows // CHUNK

    decays_and_operands()
    for cumsum_block, _, _, _, _ in programs:
        for r0 in range(0, rows, CUMSUM_ROWS):
            cumsum_block(r0)
    output_gates()
    for c in range(nchunk):
        for _, prep, _, _, _ in programs:
            prep(c)
    for _, _, scores_and_state, outputs, store_state in programs:
        for c in range(nchunk):
            scores_and_state(c)
            if c >= 1:
                outputs(c - 1)
        outputs(nchunk - 1)
        store_state()


def _mixer_call(x2, wn, w_t, w_gk, bgk, lbl, nwa, nwb, cast_weights, *, batch):
    m = x2.shape[0]
    tm = MIXER_ROWS
    nt = m // batch // tm
    steps = batch * nt
    row = lambda width: pl.BlockSpec((tm, width), lambda b, i: (b * nt + i, 0))
    full = lambda a: pl.BlockSpec(a.shape, lambda b, i: (0,) * a.ndim)
    resident = lambda a: pl.BlockSpec(a.shape, lambda b, i: (0,) * a.ndim,
                                      pipeline_mode=pl.Buffered(1))
    slab = lambda a: pl.BlockSpec((a.shape[0] // steps, a.shape[1]), lambda b, i: (b * nt + i, 0))
    out_shapes = (jax.ShapeDtypeStruct((m, GLA_V + HGRN_V), BF16),)
    cast_shapes = tuple(jax.ShapeDtypeStruct(w.shape, BF16) for w in cast_weights)
    scratch = [pltpu.VMEM((D_MODEL, GLA_COLS), BF16), pltpu.VMEM((D_MODEL, LANES), BF16),
               pltpu.VMEM((D_MODEL, HGRN_COLS), BF16)]
    for mix in _MIXERS:
        w, gw = mix["heads"] * mix["dk"], max(mix["dk"], LANES)
        wv = mix["heads"] * mix["dv"]
        scratch += [pltpu.VMEM((tm, w), F32), pltpu.VMEM((tm, w), F32), pltpu.VMEM((tm, w), F32),
                    pltpu.VMEM((tm, wv), BF16), pltpu.VMEM((tm, wv), F32),
                    pltpu.VMEM((tm, w), F32), pltpu.VMEM((mix["heads"], mix["dv"], gw), F32)]
    outs = pl.pallas_call(
        functools.partial(_mixer_kernel, n_cast=len(cast_weights)),
        grid=(batch, nt),
        in_specs=[row(D_MODEL), full(wn), resident(w_t), full(w_gk), full(bgk), full(lbl),
                  full(nwa), full(nwb)] + [slab(w) for w in cast_weights],
        out_specs=tuple(row(s.shape[1]) for s in out_shapes) + tuple(slab(w) for w in cast_weights),
        out_shape=out_shapes + cast_shapes,
        scratch_shapes=scratch,
        compiler_params=pltpu.CompilerParams(
            dimension_semantics=("arbitrary", "arbitrary"), vmem_limit_bytes=VMEM_LIMIT_BYTES),
        name="proj_gla_hgrn_mixers",
    )(x2, wn, w_t, w_gk, bgk, lbl, nwa, nwb, *cast_weights)
    return outs[0], outs[1:]


def _tail_kernel(x_ref, y_ref, w_out_ref, post_mix_ref, pre_mlp_ref, post_mlp_ref,
                 w_up_ref, w_down_ref, out_ref):
    sub = x_ref.shape[0] // TAIL_SUBTILES
    tiles = [slice(s * sub, (s + 1) * sub) for s in range(TAIL_SUBTILES)]
    hs, us = {}, {}

    def out_proj(s):
        r = tiles[s]
        mix = jnp.dot(y_ref[r, :], w_out_ref[...], preferred_element_type=F32)
        hs[s] = x_ref[r, :] + _rmsnorm(mix, post_mix_ref[...])

    def up(s):
        us[s] = jnp.dot(_rmsnorm(hs[s], pre_mlp_ref[...]).astype(BF16), w_up_ref[...],
                        preferred_element_type=F32)

    def down(s):
        relu = jnp.maximum(us.pop(s), 0.0)
        m = jnp.dot((relu * relu).astype(BF16), w_down_ref[...], preferred_element_type=F32)
        out_ref[tiles[s], :] = hs.pop(s) + _rmsnorm(m, post_mlp_ref[...])

    for s in range(TAIL_SUBTILES):
        out_proj(s)
    for s in range(TAIL_SUBTILES):
        up(s)
        down(s)


def _tail_call(x2, y, w_out, post_mix, pre_mlp, post_mlp, w_up, w_down):
    m = x2.shape[0]
    tm = TAIL_ROWS
    row = lambda width: pl.BlockSpec((tm, width), lambda i: (i, 0))
    full = lambda a: pl.BlockSpec(a.shape, lambda i: (0,) * a.ndim)
    resident = lambda a: pl.BlockSpec(a.shape, lambda i: (0,) * a.ndim,
                                      pipeline_mode=pl.Buffered(1))
    return pl.pallas_call(
        _tail_kernel,
        grid=(m // tm,),
        in_specs=[row(D_MODEL), row(GLA_V + HGRN_V), resident(w_out), full(post_mix), full(pre_mlp),
                  full(post_mlp), resident(w_up), resident(w_down)],
        out_specs=row(D_MODEL),
        out_shape=jax.ShapeDtypeStruct((m, D_MODEL), F32),
        compiler_params=pltpu.CompilerParams(
            dimension_semantics=("parallel",), vmem_limit_bytes=VMEM_LIMIT_BYTES),
        name="out_proj_mlp",
    )(x2, y, w_out, post_mix, pre_mlp, post_mlp, w_up, w_down)


def kernel(x, w_in, w_gk_up, b_gk, gla_norm_w, hgrn_norm_w, hgrn_lower_bounds, w_out,
           pre_mix_norm, post_mix_norm, pre_mlp_norm, post_mlp_norm, w_up, w_down):
    batch, seq, d = x.shape
    x2 = x.reshape(batch * seq, d)
    l = 0
    w_t = jnp.transpose(w_in[l])
    row2 = lambda a: a.reshape(1, -1)

    y, (w_out_b, w_up_b, w_down_b) = _mixer_call(
        x2, row2(pre_mix_norm[l]), w_t, w_gk_up[l], row2(b_gk[l]), hgrn_lower_bounds,
        row2(gla_norm_w[l]), row2(hgrn_norm_w[l]), (w_out[l], w_up[l], w_down[l]), batch=batch)
    out = _tail_call(x2, y, w_out_b, row2(post_mix_norm[l]), row2(pre_mlp_norm[l]),
                     row2(post_mlp_norm[l]), w_up_b, w_down_b)
    return out.reshape(batch, seq, d)
```

```python
import functools

import jax
import jax.numpy as jnp
from jax import lax
from jax.experimental import pallas as pl
from jax.experimental.pallas import tpu as pltpu

F32 = jnp.float32
BF16 = jnp.bfloat16

D_MODEL = 1024
GLA_HEADS, GLA_DK, GLA_DV = 4, 64, 128
GLA_QK = GLA_HEADS * GLA_DK
GLA_V = GLA_HEADS * GLA_DV
GLA_GATE_RANK = 16
GLA_GATE_NORM = 16.0
HGRN_HEADS, HGRN_DF, HGRN_DV = 4, 128, 128
HGRN_F = HGRN_HEADS * HGRN_DF
HGRN_V = HGRN_HEADS * HGRN_DV
EPS = 1e-6
LOG2_E = 1.4426950408889634

LANES = 128
VMEM_LIMIT_BYTES = 56 * 1024 * 1024

_C_GQ = 0
_C_GK = _C_GQ + GLA_QK
_C_GV = _C_GK + GLA_QK
_C_GG = _C_GV + GLA_V
GLA_COLS = _C_GG + GLA_V
_C_HQ = 0
_C_HF = _C_HQ + HGRN_F
_C_HI = _C_HF + HGRN_F
_C_HG = _C_HI + HGRN_V
HGRN_COLS = _C_HG + HGRN_V

MIXER_ROWS = 512
TRANSPOSE_ROWS = 512
CUMSUM_ROWS = 256
CHUNK = 64
SUB = 16
NLAG = CHUNK // SUB
LAG_OFFSETS = [sum(CHUNK - SUB * m for m in range(l)) for l in range(NLAG + 1)]
LAG_ROWS = LAG_OFFSETS[NLAG]
TAIL_ROWS = 1024
TAIL_SUBTILES = 2


def _rmsnorm(x, w):
    return x * lax.rsqrt(jnp.mean(x * x, axis=-1, keepdims=True) + EPS) * w


def _sigmoid(x):
    return 1.0 / (1.0 + jnp.exp2(x * (-LOG2_E)))


def _silu(x):
    return x * _sigmoid(x)


def _log_sigmoid(x):
    return -(jnp.maximum(-x, 0.0) + jnp.log(1.0 + jnp.exp2(jnp.abs(x) * (-LOG2_E))))


def _transpose_projection_weight(wt_ref, wa_ref, wbraw_ref, wb_ref):
    turn = lambda lo, n: wt_ref[lo:lo + n, :].astype(BF16).T
    for r in range(0, GLA_COLS, TRANSPOSE_ROWS):
        wa_ref[:, r:r + TRANSPOSE_ROWS] = turn(r, TRANSPOSE_ROWS)
    wbraw_ref[...] = turn(GLA_COLS, LANES)
    hgrn0 = GLA_COLS + GLA_GATE_RANK
    for r in range(0, HGRN_COLS, TRANSPOSE_ROWS):
        wb_ref[:, r:r + TRANSPOSE_ROWS] = turn(hgrn0 + r, TRANSPOSE_ROWS)


def _projection_program(x_ref, wn_ref, wa_ref, wbraw_ref, wb_ref, wgk_ref, bgk_ref, lbl_ref,
                        gla, hgrn):
    hb = _rmsnorm(x_ref[...], wn_ref[...]).astype(BF16)

    def proj(w_ref, lo, width):
        return jnp.dot(hb, w_ref[:, lo:lo + width], preferred_element_type=F32)

    def decays_and_operands():
        lbl = lbl_ref[...]
        e = jnp.exp(lbl - jnp.max(lbl, axis=0, keepdims=True))
        lb = e[0:1, :] / jnp.sum(e, axis=0, keepdims=True)
        f = lb + (1.0 - lb) * _sigmoid(proj(wb_ref, _C_HF, HGRN_F))
        hgrn["k"][...] = 1.0 - f
        hgrn["g"][...] = jnp.log(f) * LOG2_E
        glr = proj(wbraw_ref, 0, LANES)[:, :GLA_GATE_RANK].astype(BF16)
        gla["q"][...] = proj(wa_ref, _C_GQ, GLA_QK) * (GLA_DK ** -0.5)
        gla["k"][...] = proj(wa_ref, _C_GK, GLA_QK)
        gla["v"][...] = proj(wa_ref, _C_GV, GLA_V).astype(BF16)
        gk = jnp.dot(glr, wgk_ref[...].astype(BF16), preferred_element_type=F32) + bgk_ref[...]
        gla["g"][...] = _log_sigmoid(gk) * (LOG2_E / GLA_GATE_NORM)
        hgrn["q"][...] = _silu(proj(wb_ref, _C_HQ, HGRN_F))
        hgrn["v"][...] = proj(wb_ref, _C_HI, HGRN_V).astype(BF16)

    def output_gates():
        gla["gate"][...] = gla["gain"][...] * _silu(proj(wa_ref, _C_GG, GLA_V))
        hgrn["gate"][...] = hgrn["gain"][...] * _silu(proj(wb_ref, _C_HG, HGRN_V))

    return decays_and_operands, output_gates


def _mixer_program(q_ref, k_ref, g_ref, v_ref, gate_ref, y_ref, y_col0, b_ref, st_ref, *,
                   heads, dk, dv):
    gw = max(dk, LANES)
    hpg = gw // dk
    ngroups = heads // hpg

    ri = lax.broadcasted_iota(jnp.int32, (CUMSUM_ROWS, CUMSUM_ROWS), 0)
    ci = lax.broadcasted_iota(jnp.int32, (CUMSUM_ROWS, CUMSUM_ROWS), 1)
    tri = jnp.where((ri // CHUNK == ci // CHUNK) & (ci <= ri), 1.0, 0.0).astype(BF16)

    def cumsum_block(r0):
        blk = slice(r0, r0 + CUMSUM_ROWS)
        g = g_ref[blk, :]
        g_hi = g.astype(BF16)
        g_lo = (g - g_hi.astype(F32)).astype(BF16)
        b_ref[blk, :] = (jnp.dot(tri, g_hi, preferred_element_type=F32)
                         + jnp.dot(tri, g_lo, preferred_element_type=F32))

    si = lax.broadcasted_iota(jnp.int32, (SUB, CHUNK), 0)
    sj = lax.broadcasted_iota(jnp.int32, (SUB, CHUNK), 1)
    col_block = [sj // SUB == r for r in range(NLAG)]
    diag_block = [col_block[r] & (sj - SUB * r <= si) for r in range(NLAG)]
    lane = lax.broadcasted_iota(jnp.int32, (1, gw), 1)

    head_ids = [(gi, hh) for gi in range(ngroups) for hh in range(hpg)]
    v_of = lambda h, c: v_ref[c * CHUNK:(c + 1) * CHUNK, h * dv:(h + 1) * dv]
    state = {(gi, hh): st_ref[gi * hpg + hh] for gi, hh in head_ids}
    probs, qin, start_state, prepared = {}, {}, {}, {}
    blk = lambda a, r: a[SUB * r:SUB * (r + 1), :]

    def prep(c):
        r0 = c * CHUNK
        rws = slice(r0, r0 + CHUNK)
        for gi in range(ngroups):
            lanes = slice(gi * gw, (gi + 1) * gw)
            b = b_ref[rws, lanes]
            q = q_ref[rws, lanes]
            k = k_ref[rws, lanes]
            ends = {r: b_ref[r0 + SUB * r + SUB - 1:r0 + SUB * r + SUB, lanes] for r in range(NLAG)}
            ends[-1] = jnp.zeros((1, gw), F32)
            rows_of = lambda f: jnp.concatenate(
                [jnp.broadcast_to(f(r), (SUB, gw)) for r in range(NLAG)], axis=0)

            q0 = q * jnp.exp2(b - rows_of(lambda r: ends[r]))
            q1 = q * jnp.exp2(b - rows_of(lambda r: ends[r - 1]))
            k_hat = k * jnp.exp2(rows_of(lambda r: ends[r]) - b)
            lag_rows = [q0.astype(BF16), q1[SUB:, :].astype(BF16)]
            for l in range(2, NLAG):
                lag_rows.append(jnp.concatenate(
                    [blk(q1, r) * jnp.exp2(ends[r - 1] - ends[r - l]) for r in range(l, NLAG)],
                    axis=0).astype(BF16))
            q_lag = jnp.concatenate(lag_rows, axis=0)
            q_in = jnp.concatenate(
                [blk(q1, r) * jnp.exp2(ends[r - 1]) for r in range(NLAG)], axis=0)
            k_dec = jnp.concatenate(
                [blk(k_hat, r) * jnp.exp2(ends[NLAG - 1] - ends[r]) for r in range(NLAG)],
                axis=0).astype(BF16)
            dec = jnp.exp2(ends[NLAG - 1])
            prepared[gi, c] = (q_lag, k_dec, dec)
            for hh in range(hpg):
                if hpg == 1:
                    prepared[gi, hh, c], qin[gi, hh, c] = k_hat.astype(BF16), q_in.astype(BF16)
                else:
                    in_head = (lane >= hh * dk) & (lane < (hh + 1) * dk)
                    prepared[gi, hh, c] = jnp.where(in_head, k_hat, 0.0).astype(BF16)
                    qin[gi, hh, c] = jnp.where(in_head, q_in, 0.0).astype(BF16)

    def scores_and_state(c):
        for gi in range(ngroups):
            q_lag, k_dec, dec = prepared.pop((gi, c))
            for hh in range(hpg):
                h = gi * hpg + hh
                k_h = prepared.pop((gi, hh, c))
                s_all = lax.dot_general(q_lag, k_h, (((1,), (1,)), ((), ())),
                                        preferred_element_type=F32)
                p_rows = []
                for r in range(NLAG):
                    p_r = jnp.where(diag_block[r], blk(s_all, r), 0.0)
                    for l in range(1, r + 1):
                        src = LAG_OFFSETS[l] + SUB * (r - l)
                        p_r = jnp.where(col_block[r - l], s_all[src:src + SUB, :], p_r)
                    p_rows.append(p_r)
                probs[gi, hh, c] = jnp.concatenate(p_rows, axis=0).astype(BF16)
                upd = lax.dot_general(v_of(h, c), k_dec, (((0,), (0,)), ((), ())),
                                      preferred_element_type=F32)
                start_state[gi, hh, c] = state[gi, hh].astype(BF16).T
                state[gi, hh] = state[gi, hh] * dec + upd

    def outputs(c):
        for gi, hh in head_ids:
            h = gi * hpg + hh
            lhs = jnp.concatenate([qin.pop((gi, hh, c)), probs.pop((gi, hh, c))], axis=1)
            rhs = jnp.concatenate([start_state.pop((gi, hh, c)), v_of(h, c)], axis=0)
            o = jnp.dot(lhs, rhs, preferred_element_type=F32)
            rws, cols = slice(c * CHUNK, (c + 1) * CHUNK), slice(h * dv, (h + 1) * dv)
            y = o * lax.rsqrt(jnp.mean(o * o, axis=-1, keepdims=True) + EPS) * gate_ref[rws, cols]
            y_ref[rws, y_col0 + h * dv:y_col0 + (h + 1) * dv] = y.astype(BF16)

    def store_state():
        for gi, hh in head_ids:
            st_ref[gi * hpg + hh] = state[gi, hh]

    return cumsum_block, prep, scores_and_state, outputs, store_state


_MIXERS = (dict(heads=GLA_HEADS, dk=GLA_DK, dv=GLA_DV), dict(heads=HGRN_HEADS, dk=HGRN_DF, dv=HGRN_DV))


def _mixer_kernel(x_ref, wn_ref, wt_ref, wgk_ref, bgk_ref, lbl_ref, nwa_ref, nwb_ref, *rest,
                  n_cast):
    cast_in, rest = rest[:n_cast], rest[n_cast:]
    y_ref, rest = rest[0], rest[1:]
    cast_out, rest = rest[:n_cast], rest[n_cast:]
    (wa_ref, wbraw_ref, wb_ref,
     qa_ref, ka_ref, ga_ref, va_ref, gga_ref, ba_ref, sta_ref,
     qb_ref, kb_ref, gb_ref, vb_ref, ggb_ref, bb_ref, stb_ref) = rest

    @pl.when((pl.program_id(0) == 0) & (pl.program_id(1) == 0))
    def _():
        _transpose_projection_weight(wt_ref, wa_ref, wbraw_ref, wb_ref)

    @pl.when(pl.program_id(1) == 0)
    def _():
        sta_ref[...] = jnp.zeros_like(sta_ref)
        stb_ref[...] = jnp.zeros_like(stb_ref)

    for src, dst in zip(cast_in, cast_out):
        dst[...] = src[...].astype(BF16)

    decays_and_operands, output_gates = _projection_program(
        x_ref, wn_ref, wa_ref, wbraw_ref, wb_ref, wgk_ref, bgk_ref, lbl_ref,
        dict(q=qa_ref, k=ka_ref, g=ga_ref, v=va_ref, gate=gga_ref, gain=nwa_ref),
        dict(q=qb_ref, k=kb_ref, g=gb_ref, v=vb_ref, gate=ggb_ref, gain=nwb_ref))
    programs = [
        _mixer_program(qa_ref, ka_ref, ga_ref, va_ref, gga_ref, y_ref, 0,
                       ba_ref, sta_ref, **_MIXERS[0]),
        _mixer_program(qb_ref, kb_ref, gb_ref, vb_ref, ggb_ref, y_ref, GLA_V,
                       bb_ref, stb_ref, **_MIXERS[1]),
    ]
    rows = x_ref.shape[0]
    nchunk = rows // CHUNK

    decays_and_operands()
    for cumsum_block, _, _, _, _ in programs:
        for r0 in range(0, rows, CUMSUM_ROWS):
            cumsum_block(r0)
    output_gates()
    for c in range(nchunk):
        for _, prep, _, _, _ in programs:
            prep(c)
    for _, _, scores_and_state, outputs, store_state in programs:
        for c in range(nchunk):
            scores_and_state(c)
            if c >= 1:
                outputs(c - 1)
        outputs(nchunk - 1)
        store_state()


def _mixer_call(x2, wn, w_t, w_gk, bgk, lbl, nwa, nwb, cast_weights, *, batch):
    m = x2.shape[0]
    tm = MIXER_ROWS
    nt = m // batch // tm
    steps = batch * nt
    assert steps * tm == m and tm % CUMSUM_ROWS == 0 and CUMSUM_ROWS % CHUNK == 0
    assert all(w.shape[0] % steps == 0 for w in cast_weights)
    row = lambda width: pl.BlockSpec((tm, width), lambda b, i: (b * nt + i, 0))
    full = lambda a: pl.BlockSpec(a.shape, lambda b, i: (0,) * a.ndim)
    resident = lambda a: pl.BlockSpec(a.shape, lambda b, i: (0,) * a.ndim,
                                      pipeline_mode=pl.Buffered(1))
    slab = lambda a: pl.BlockSpec((a.shape[0] // steps, a.shape[1]), lambda b, i: (b * nt + i, 0))
    out_shapes = (jax.ShapeDtypeStruct((m, GLA_V + HGRN_V), BF16),)
    cast_shapes = tuple(jax.ShapeDtypeStruct(w.shape, BF16) for w in cast_weights)
    scratch = [pltpu.VMEM((D_MODEL, GLA_COLS), BF16), pltpu.VMEM((D_MODEL, LANES), BF16),
               pltpu.VMEM((D_MODEL, HGRN_COLS), BF16)]
    for mix in _MIXERS:
        w, gw = mix["heads"] * mix["dk"], max(mix["dk"], LANES)
        wv = mix["heads"] * mix["dv"]
        scratch += [pltpu.VMEM((tm, w), F32), pltpu.VMEM((tm, w), F32), pltpu.VMEM((tm, w), F32),
                    pltpu.VMEM((tm, wv), BF16), pltpu.VMEM((tm, wv), F32),
                    pltpu.VMEM((tm, w), F32), pltpu.VMEM((mix["heads"], mix["dv"], gw), F32)]
    outs = pl.pallas_call(
        functools.partial(_mixer_kernel, n_cast=len(cast_weights)),
        grid=(batch, nt),
        in_specs=[row(D_MODEL), full(wn), resident(w_t), full(w_gk), full(bgk), full(lbl),
                  full(nwa), full(nwb)] + [slab(w) for w in cast_weights],
        out_specs=tuple(row(s.shape[1]) for s in out_shapes) + tuple(slab(w) for w in cast_weights),
        out_shape=out_shapes + cast_shapes,
        scratch_shapes=scratch,
        compiler_params=pltpu.CompilerParams(
            dimension_semantics=("arbitrary", "arbitrary"), vmem_limit_bytes=VMEM_LIMIT_BYTES),
        name="proj_gla_hgrn_mixers",
    )(x2, wn, w_t, w_gk, bgk, lbl, nwa, nwb, *cast_weights)
    return outs[0], outs[1:]


def _tail_kernel(x_ref, y_ref, w_out_ref, post_mix_ref, pre_mlp_ref, post_mlp_ref,
                 w_up_ref, w_down_ref, out_ref):
    sub = x_ref.shape[0] // TAIL_SUBTILES
    tiles = [slice(s * sub, (s + 1) * sub) for s in range(TAIL_SUBTILES)]
    hs, us = {}, {}

    def out_proj(s):
        r = tiles[s]
        mix = jnp.dot(y_ref[r, :], w_out_ref[...], preferred_element_type=F32)
        hs[s] = x_ref[r, :] + _rmsnorm(mix, post_mix_ref[...])

    def up(s):
        us[s] = jnp.dot(_rmsnorm(hs[s], pre_mlp_ref[...]).astype(BF16), w_up_ref[...],
                        preferred_element_type=F32)

    def down(s):
        relu = jnp.maximum(us.pop(s), 0.0)
        m = jnp.dot((relu * relu).astype(BF16), w_down_ref[...], preferred_element_type=F32)
        out_ref[tiles[s], :] = hs.pop(s) + _rmsnorm(m, post_mlp_ref[...])

    for s in range(TAIL_SUBTILES):
        out_proj(s)
    for s in range(TAIL_SUBTILES):
        up(s)
        down(s)


def _tail_call(x2, y, w_out, post_mix, pre_mlp, post_mlp, w_up, w_down):
    m = x2.shape[0]
    tm = TAIL_ROWS
    assert m % tm == 0 and tm % TAIL_SUBTILES == 0
    row = lambda width: pl.BlockSpec((tm, width), lambda i: (i, 0))
    full = lambda a: pl.BlockSpec(a.shape, lambda i: (0,) * a.ndim)
    resident = lambda a: pl.BlockSpec(a.shape, lambda i: (0,) * a.ndim,
                                      pipeline_mode=pl.Buffered(1))
    return pl.pallas_call(
        _tail_kernel,
        grid=(m // tm,),
        in_specs=[row(D_MODEL), row(GLA_V + HGRN_V), resident(w_out), full(post_mix), full(pre_mlp),
                  full(post_mlp), resident(w_up), resident(w_down)],
        out_specs=row(D_MODEL),
        out_shape=jax.ShapeDtypeStruct((m, D_MODEL), F32),
        compiler_params=pltpu.CompilerParams(
            dimension_semantics=("parallel",), vmem_limit_bytes=VMEM_LIMIT_BYTES),
        name="out_proj_mlp",
    )(x2, y, w_out, post_mix, pre_mlp, post_mlp, w_up, w_down)


def kernel(x, w_in, w_gk_up, b_gk, gla_norm_w, hgrn_norm_w, hgrn_lower_bounds, w_out,
           pre_mix_norm, post_mix_norm, pre_mlp_norm, post_mlp_norm, w_up, w_down):
    batch, seq, d = x.shape
    assert d == D_MODEL and w_in.shape[0] == 1 and w_in.shape[2] == GLA_COLS + GLA_GATE_RANK + HGRN_COLS
    x2 = x.reshape(batch * seq, d)
    l = 0
    w_t = jnp.transpose(w_in[l])
    row2 = lambda a: a.reshape(1, -1)

    y, (w_out_b, w_up_b, w_down_b) = _mixer_call(
        x2, row2(pre_mix_norm[l]), w_t, w_gk_up[l], row2(b_gk[l]), hgrn_lower_bounds,
        row2(gla_norm_w[l]), row2(hgrn_norm_w[l]), (w_out[l], w_up[l], w_down[l]), batch=batch)
    out = _tail_call(x2, y, w_out_b, row2(post_mix_norm[l]), row2(pre_mlp_norm[l]),
                     row2(post_mlp_norm[l]), w_up_b, w_down_b)
    return out.reshape(batch, seq, d)
```

```python
import functools

import jax
import jax.numpy as jnp
from jax import lax
from jax.experimental import pallas as pl
from jax.experimental.pallas import tpu as pltpu

F32 = jnp.float32
BF16 = jnp.bfloat16

D_MODEL = 1024
GLA_HEADS, GLA_DK, GLA_DV = 4, 64, 128
GLA_QK = GLA_HEADS * GLA_DK
GLA_V = GLA_HEADS * GLA_DV
GLA_GATE_RANK = 16
GLA_GATE_NORM = 16.0
HGRN_HEADS, HGRN_DF, HGRN_DV = 4, 128, 128
HGRN_F = HGRN_HEADS * HGRN_DF
HGRN_V = HGRN_HEADS * HGRN_DV
assert GLA_DV == HGRN_DV
HEAD_DV = GLA_DV
EPS = 1e-6
LOG2_E = 1.4426950408889634

LANES = 128
VMEM_LIMIT_BYTES = 56 * 1024 * 1024

_C_GQ = 0
_C_GK = _C_GQ + GLA_QK
_C_GV = _C_GK + GLA_QK
_C_GG = _C_GV + GLA_V
GLA_COLS = _C_GG + GLA_V
_C_HQ = 0
_C_HF = _C_HQ + HGRN_F
_C_HI = _C_HF + HGRN_F
_C_HG = _C_HI + HGRN_V
HGRN_COLS = _C_HG + HGRN_V

MIXER_ROWS = 512
TRANSPOSE_ROWS = 512
CUMSUM_ROWS = 256
CHUNK = 64
SUB = 16
NLAG = CHUNK // SUB
LAG_OFFSETS = [sum(CHUNK - SUB * m for m in range(l)) for l in range(NLAG + 1)]
LAG_ROWS = LAG_OFFSETS[NLAG]
TAIL_ROWS = 1024
TAIL_SUBTILES = 2


def _rmsnorm(x, w):
    return x * lax.rsqrt(jnp.mean(x * x, axis=-1, keepdims=True) + EPS) * w


def _sigmoid(x):
    return 1.0 / (1.0 + jnp.exp2(x * (-LOG2_E)))


def _silu(x):
    return x * _sigmoid(x)


def _log_sigmoid(x):
    return -(jnp.maximum(-x, 0.0) + jnp.log(1.0 + jnp.exp2(jnp.abs(x) * (-LOG2_E))))


def _transpose_projection_weight(wt_ref, wa_ref, wbraw_ref, wb_ref):
    turn = lambda lo, n: wt_ref[lo:lo + n, :].astype(BF16).T
    for r in range(0, GLA_COLS, TRANSPOSE_ROWS):
        wa_ref[:, r:r + TRANSPOSE_ROWS] = turn(r, TRANSPOSE_ROWS)
    wbraw_ref[...] = turn(GLA_COLS, LANES)
    hgrn0 = GLA_COLS + GLA_GATE_RANK
    for r in range(0, HGRN_COLS, TRANSPOSE_ROWS):
        wb_ref[:, r:r + TRANSPOSE_ROWS] = turn(hgrn0 + r, TRANSPOSE_ROWS)


def _projection_program(x_ref, wn_ref, wa_ref, wbraw_ref, wb_ref, wgk_ref, bgk_ref, lbl_ref,
                        gla, hgrn):
    hb = _rmsnorm(x_ref[...], wn_ref[...]).astype(BF16)

    def proj(w_ref, lo, width):
        return jnp.dot(hb, w_ref[:, lo:lo + width], preferred_element_type=F32)

    def decays_and_operands():
        lbl = lbl_ref[...]
        e = jnp.exp(lbl - jnp.max(lbl, axis=0, keepdims=True))
        lb = e[0:1, :] / jnp.sum(e, axis=0, keepdims=True)
        f = lb + (1.0 - lb) * _sigmoid(proj(wb_ref, _C_HF, HGRN_F))
        hgrn["k"][...] = 1.0 - f
        hgrn["g"][...] = jnp.log(f) * LOG2_E
        glr = proj(wbraw_ref, 0, LANES)[:, :GLA_GATE_RANK].astype(BF16)
        gla["q"][...] = proj(wa_ref, _C_GQ, GLA_QK) * (GLA_DK ** -0.5)
        gla["k"][...] = proj(wa_ref, _C_GK, GLA_QK)
        gla["v"][...] = proj(wa_ref, _C_GV, GLA_V).astype(BF16)
        gk = jnp.dot(glr, wgk_ref[...].astype(BF16), preferred_element_type=F32) + bgk_ref[...]
        gla["g"][...] = _log_sigmoid(gk) * (LOG2_E / GLA_GATE_NORM)
        hgrn["q"][...] = _silu(proj(wb_ref, _C_HQ, HGRN_F))
        hgrn["v"][...] = proj(wb_ref, _C_HI, HGRN_V).astype(BF16)

    def output_gates():
        t_ref, col0 = gla["gate"]
        t_ref[:, col0:col0 + GLA_V] = (
            gla["gain"][...] * _silu(proj(wa_ref, _C_GG, GLA_V))).astype(BF16)
        t_ref, col0 = hgrn["gate"]
        t_ref[:, col0:col0 + HGRN_V] = (
            hgrn["gain"][...] * _silu(proj(wb_ref, _C_HG, HGRN_V))).astype(BF16)

    return decays_and_operands, output_gates


def _mixer_program(q_ref, k_ref, g_ref, v_ref, o_ref, o_col0, b_ref, st_ref, *, heads, dk, dv):
    gw = max(dk, LANES)
    hpg = gw // dk
    ngroups = heads // hpg

    ri = lax.broadcasted_iota(jnp.int32, (CUMSUM_ROWS, CUMSUM_ROWS), 0)
    ci = lax.broadcasted_iota(jnp.int32, (CUMSUM_ROWS, CUMSUM_ROWS), 1)
    tri = jnp.where((ri // CHUNK == ci // CHUNK) & (ci <= ri), 1.0, 0.0).astype(BF16)

    def cumsum_block(r0):
        blk = slice(r0, r0 + CUMSUM_ROWS)
        g = g_ref[blk, :]
        g_hi = g.astype(BF16)
        g_lo = (g - g_hi.astype(F32)).astype(BF16)
        b_ref[blk, :] = (jnp.dot(tri, g_hi, preferred_element_type=F32)
                         + jnp.dot(tri, g_lo, preferred_element_type=F32))

    si = lax.broadcasted_iota(jnp.int32, (SUB, CHUNK), 0)
    sj = lax.broadcasted_iota(jnp.int32, (SUB, CHUNK), 1)
    col_block = [sj // SUB == r for r in range(NLAG)]
    diag_block = [col_block[r] & (sj - SUB * r <= si) for r in range(NLAG)]
    lane = lax.broadcasted_iota(jnp.int32, (1, gw), 1)

    head_ids = [(gi, hh) for gi in range(ngroups) for hh in range(hpg)]
    v_of = lambda h, c: v_ref[c * CHUNK:(c + 1) * CHUNK, h * dv:(h + 1) * dv]
    state = {(gi, hh): st_ref[gi * hpg + hh] for gi, hh in head_ids}
    probs, qin, start_state, prepared = {}, {}, {}, {}
    blk = lambda a, r: a[SUB * r:SUB * (r + 1), :]

    def prep(c):
        r0 = c * CHUNK
        rws = slice(r0, r0 + CHUNK)
        for gi in range(ngroups):
            lanes = slice(gi * gw, (gi + 1) * gw)
            b = b_ref[rws, lanes]
            q = q_ref[rws, lanes]
            k = k_ref[rws, lanes]
            ends = {r: b_ref[r0 + SUB * r + SUB - 1:r0 + SUB * r + SUB, lanes] for r in range(NLAG)}
            ends[-1] = jnp.zeros((1, gw), F32)
            rows_of = lambda f: jnp.concatenate(
                [jnp.broadcast_to(f(r), (SUB, gw)) for r in range(NLAG)], axis=0)

            q0 = q * jnp.exp2(b - rows_of(lambda r: ends[r]))
            q1 = q * jnp.exp2(b - rows_of(lambda r: ends[r - 1]))
            k_hat = k * jnp.exp2(rows_of(lambda r: ends[r]) - b)
            lag_rows = [q0.astype(BF16), q1[SUB:, :].astype(BF16)]
            for l in range(2, NLAG):
                lag_rows.append(jnp.concatenate(
                    [blk(q1, r) * jnp.exp2(ends[r - 1] - ends[r - l]) for r in range(l, NLAG)],
                    axis=0).astype(BF16))
            q_lag = jnp.concatenate(lag_rows, axis=0)
            q_in = jnp.concatenate(
                [blk(q1, r) * jnp.exp2(ends[r - 1]) for r in range(NLAG)], axis=0)
            k_dec = jnp.concatenate(
                [blk(k_hat, r) * jnp.exp2(ends[NLAG - 1] - ends[r]) for r in range(NLAG)],
                axis=0).astype(BF16)
            dec = jnp.exp2(ends[NLAG - 1])
            prepared[gi, c] = (q_lag, k_dec, dec)
            for hh in range(hpg):
                if hpg == 1:
                    prepared[gi, hh, c], qin[gi, hh, c] = k_hat.astype(BF16), q_in.astype(BF16)
                else:
                    in_head = (lane >= hh * dk) & (lane < (hh + 1) * dk)
                    prepared[gi, hh, c] = jnp.where(in_head, k_hat, 0.0).astype(BF16)
                    qin[gi, hh, c] = jnp.where(in_head, q_in, 0.0).astype(BF16)

    def scores_and_state(c):
        for gi in range(ngroups):
            q_lag, k_dec, dec = prepared.pop((gi, c))
            for hh in range(hpg):
                h = gi * hpg + hh
                k_h = prepared.pop((gi, hh, c))
                s_all = lax.dot_general(q_lag, k_h, (((1,), (1,)), ((), ())),
                                        preferred_element_type=F32)
                p_rows = []
                for r in range(NLAG):
                    p_r = jnp.where(diag_block[r], blk(s_all, r), 0.0)
                    for l in range(1, r + 1):
                        src = LAG_OFFSETS[l] + SUB * (r - l)
                        p_r = jnp.where(col_block[r - l], s_all[src:src + SUB, :], p_r)
                    p_rows.append(p_r)
                probs[gi, hh, c] = jnp.concatenate(p_rows, axis=0).astype(BF16)
                upd = lax.dot_general(v_of(h, c), k_dec, (((0,), (0,)), ((), ())),
                                      preferred_element_type=F32)
                start_state[gi, hh, c] = state[gi, hh].astype(BF16).T
                state[gi, hh] = state[gi, hh] * dec + upd

    def outputs(c):
        for gi, hh in head_ids:
            h = gi * hpg + hh
            lhs = jnp.concatenate([qin.pop((gi, hh, c)), probs.pop((gi, hh, c))], axis=1)
            rhs = jnp.concatenate([start_state.pop((gi, hh, c)), v_of(h, c)], axis=0)
            o = jnp.dot(lhs, rhs, preferred_element_type=F32)
            o_ref[c * CHUNK:(c + 1) * CHUNK, o_col0 + h * dv:o_col0 + (h + 1) * dv] = o.astype(BF16)

    def store_state():
        for gi, hh in head_ids:
            st_ref[gi * hpg + hh] = state[gi, hh]

    return cumsum_block, prep, scores_and_state, outputs, store_state


_MIXERS = (dict(heads=GLA_HEADS, dk=GLA_DK, dv=GLA_DV), dict(heads=HGRN_HEADS, dk=HGRN_DF, dv=HGRN_DV))


def _mixer_kernel(x_ref, wn_ref, wt_ref, wgk_ref, bgk_ref, lbl_ref, nwa_ref, nwb_ref, *rest,
                  n_cast):
    cast_in, rest = rest[:n_cast], rest[n_cast:]
    (o_ref, t_ref), rest = rest[:2], rest[2:]
    cast_out, rest = rest[:n_cast], rest[n_cast:]
    (wa_ref, wbraw_ref, wb_ref,
     qa_ref, ka_ref, ga_ref, va_ref, ba_ref, sta_ref,
     qb_ref, kb_ref, gb_ref, vb_ref, bb_ref, stb_ref) = rest

    @pl.when((pl.program_id(0) == 0) & (pl.program_id(1) == 0))
    def _():
        _transpose_projection_weight(wt_ref, wa_ref, wbraw_ref, wb_ref)

    @pl.when(pl.program_id(1) == 0)
    def _():
        sta_ref[...] = jnp.zeros_like(sta_ref)
        stb_ref[...] = jnp.zeros_like(stb_ref)

    for src, dst in zip(cast_in, cast_out):
        dst[...] = src[...].astype(BF16)

    decays_and_operands, output_gates = _projection_program(
        x_ref, wn_ref, wa_ref, wbraw_ref, wb_ref, wgk_ref, bgk_ref, lbl_ref,
        dict(q=qa_ref, k=ka_ref, g=ga_ref, v=va_ref, gate=(t_ref, 0), gain=nwa_ref),
        dict(q=qb_ref, k=kb_ref, g=gb_ref, v=vb_ref, gate=(t_ref, GLA_V), gain=nwb_ref))
    programs = [
        _mixer_program(qa_ref, ka_ref, ga_ref, va_ref, o_ref, 0, ba_ref, sta_ref, **_MIXERS[0]),
        _mixer_program(qb_ref, kb_ref, gb_ref, vb_ref, o_ref, GLA_V, bb_ref, stb_ref, **_MIXERS[1]),
    ]
    rows = x_ref.shape[0]
    nchunk = rows // CHUNK

    decays_and_operands()
    for cumsum_block, _, _, _, _ in programs:
        for r0 in range(0, rows, CUMSUM_ROWS):
            cumsum_block(r0)
    output_gates()
    for c in range(nchunk):
        for _, prep, _, _, _ in programs:
            prep(c)
    for _, _, scores_and_state, outputs, store_state in programs:
        for c in range(nchunk):
            scores_and_state(c)
            if c >= 1:
                outputs(c - 1)
        outputs(nchunk - 1)
        store_state()


def _mixer_call(x2, wn, w_t, w_gk, bgk, lbl, nwa, nwb, cast_weights, *, batch):
    m = x2.shape[0]
    tm = MIXER_ROWS
    nt = m // batch // tm
    steps = batch * nt
    assert steps * tm == m and tm % CUMSUM_ROWS == 0 and CUMSUM_ROWS % CHUNK == 0
    assert all(w.shape[0] % steps == 0 for w in cast_weights)
    row = lambda width: pl.BlockSpec((tm, width), lambda b, i: (b * nt + i, 0))
    full = lambda a: pl.BlockSpec(a.shape, lambda b, i: (0,) * a.ndim)
    resident = lambda a: pl.BlockSpec(a.shape, lambda b, i: (0,) * a.ndim,
                                      pipeline_mode=pl.Buffered(1))
    slab = lambda a: pl.BlockSpec((a.shape[0] // steps, a.shape[1]), lambda b, i: (b * nt + i, 0))
    out_shapes = (jax.ShapeDtypeStruct((m, GLA_V + HGRN_V), BF16),) * 2
    cast_shapes = tuple(jax.ShapeDtypeStruct(w.shape, BF16) for w in cast_weights)
    scratch = [pltpu.VMEM((D_MODEL, GLA_COLS), BF16), pltpu.VMEM((D_MODEL, LANES), BF16),
               pltpu.VMEM((D_MODEL, HGRN_COLS), BF16)]
    for mix in _MIXERS:
        w, gw = mix["heads"] * mix["dk"], max(mix["dk"], LANES)
        wv = mix["heads"] * mix["dv"]
        scratch += [pltpu.VMEM((tm, w), F32), pltpu.VMEM((tm, w), F32), pltpu.VMEM((tm, w), F32),
                    pltpu.VMEM((tm, wv), BF16),
                    pltpu.VMEM((tm, w), F32), pltpu.VMEM((mix["heads"], mix["dv"], gw), F32)]
    outs = pl.pallas_call(
        functools.partial(_mixer_kernel, n_cast=len(cast_weights)),
        grid=(batch, nt),
        in_specs=[row(D_MODEL), full(wn), resident(w_t), full(w_gk), full(bgk), full(lbl),
                  full(nwa), full(nwb)] + [slab(w) for w in cast_weights],
        out_specs=tuple(row(s.shape[1]) for s in out_shapes) + tuple(slab(w) for w in cast_weights),
        out_shape=out_shapes + cast_shapes,
        scratch_shapes=scratch,
        compiler_params=pltpu.CompilerParams(
            dimension_semantics=("arbitrary", "arbitrary"), vmem_limit_bytes=VMEM_LIMIT_BYTES),
        name="proj_gla_hgrn_mixers",
    )(x2, wn, w_t, w_gk, bgk, lbl, nwa, nwb, *cast_weights)
    return outs[0], outs[1], outs[2:]


def _tail_kernel(x_ref, o_ref, t_ref, w_out_ref, post_mix_ref, pre_mlp_ref, post_mlp_ref,
                 w_up_ref, w_down_ref, out_ref):
    sub = x_ref.shape[0] // TAIL_SUBTILES
    tiles = [slice(s * sub, (s + 1) * sub) for s in range(TAIL_SUBTILES)]
    hs, us = {}, {}

    def out_proj(s):
        r = tiles[s]
        heads = []
        for c0 in range(0, GLA_V + HGRN_V, HEAD_DV):
            o = o_ref[r, c0:c0 + HEAD_DV].astype(F32)
            heads.append(o * lax.rsqrt(jnp.mean(o * o, axis=-1, keepdims=True) + EPS))
        y = (jnp.concatenate(heads, axis=-1) * t_ref[r, :].astype(F32)).astype(BF16)
        mix = jnp.dot(y, w_out_ref[...], preferred_element_type=F32)
        hs[s] = x_ref[r, :] + _rmsnorm(mix, post_mix_ref[...])

    def up(s):
        us[s] = jnp.dot(_rmsnorm(hs[s], pre_mlp_ref[...]).astype(BF16), w_up_ref[...],
                        preferred_element_type=F32)

    def down(s):
        relu = jnp.maximum(us.pop(s), 0.0)
        m = jnp.dot((relu * relu).astype(BF16), w_down_ref[...], preferred_element_type=F32)
        out_ref[tiles[s], :] = hs.pop(s) + _rmsnorm(m, post_mlp_ref[...])

    for s in range(TAIL_SUBTILES):
        out_proj(s)
    for s in range(TAIL_SUBTILES):
        up(s)
        down(s)


def _tail_call(x2, o, t, w_out, post_mix, pre_mlp, post_mlp, w_up, w_down):
    m = x2.shape[0]
    tm = TAIL_ROWS
    assert m % tm == 0 and tm % TAIL_SUBTILES == 0
    row = lambda width: pl.BlockSpec((tm, width), lambda i: (i, 0))
    full = lambda a: pl.BlockSpec(a.shape, lambda i: (0,) * a.ndim)
    resident = lambda a: pl.BlockSpec(a.shape, lambda i: (0,) * a.ndim,
                                      pipeline_mode=pl.Buffered(1))
    return pl.pallas_call(
        _tail_kernel,
        grid=(m // tm,),
        in_specs=[row(D_MODEL), row(GLA_V + HGRN_V), row(GLA_V + HGRN_V), resident(w_out),
                  full(post_mix), full(pre_mlp), full(post_mlp), resident(w_up), resident(w_down)],
        out_specs=row(D_MODEL),
        out_shape=jax.ShapeDtypeStruct((m, D_MODEL), F32),
        compiler_params=pltpu.CompilerParams(
            dimension_semantics=("parallel",), vmem_limit_bytes=VMEM_LIMIT_BYTES),
        name="out_proj_mlp",
    )(x2, o, t, w_out, post_mix, pre_mlp, post_mlp, w_up, w_down)


def kernel(x, w_in, w_gk_up, b_gk, gla_norm_w, hgrn_norm_w, hgrn_lower_bounds, w_out,
           pre_mix_norm, post_mix_norm, pre_mlp_norm, post_mlp_norm, w_up, w_down):
    batch, seq, d = x.shape
    assert d == D_MODEL and w_in.shape[0] == 1 and w_in.shape[2] == GLA_COLS + GLA_GATE_RANK + HGRN_COLS
    x2 = x.reshape(batch * seq, d)
    l = 0
    w_t = jnp.transpose(w_in[l])
    row2 = lambda a: a.reshape(1, -1)

    o, t, (w_out_b, w_up_b, w_down_b) = _mixer_call(
        x2, row2(pre_mix_norm[l]), w_t, w_gk_up[l], row2(b_gk[l]), hgrn_lower_bounds,
        row2(gla_norm_w[l]), row2(hgrn_norm_w[l]), (w_out[l], w_up[l], w_down[l]), batch=batch)
    out = _tail_call(x2, o, t, w_out_b, row2(post_mix_norm[l]), row2(pre_mlp_norm[l]),
                     row2(post_mlp_norm[l]), w_up_b, w_down_b)
    return out.reshape(batch, seq, d)
```

```python
import functools

import jax
import jax.numpy as jnp
from jax import lax
from jax.experimental import pallas as pl
from jax.experimental.pallas import tpu as pltpu

F32 = jnp.float32
BF16 = jnp.bfloat16

D_MODEL = 1024
GLA_HEADS, GLA_DK, GLA_DV = 4, 64, 128
GLA_QK = GLA_HEADS * GLA_DK
GLA_V = GLA_HEADS * GLA_DV
GLA_GATE_RANK = 16
GLA_GATE_NORM = 16.0
HGRN_HEADS, HGRN_DF, HGRN_DV = 4, 128, 128
HGRN_F = HGRN_HEADS * HGRN_DF
HGRN_V = HGRN_HEADS * HGRN_DV
assert GLA_DV == HGRN_DV
HEAD_DV = GLA_DV
EPS = 1e-6
LOG2_E = 1.4426950408889634

LANES = 128
VMEM_LIMIT_BYTES = 56 * 1024 * 1024

_C_GQ = 0
_C_GK = _C_GQ + GLA_QK
_C_GV = _C_GK + GLA_QK
_C_GG = _C_GV + GLA_V
GLA_COLS = _C_GG + GLA_V
_C_HQ = 0
_C_HF = _C_HQ + HGRN_F
_C_HI = _C_HF + HGRN_F
_C_HG = _C_HI + HGRN_V
HGRN_COLS = _C_HG + HGRN_V

MIXER_ROWS = 512
TRANSPOSE_ROWS = 512
CUMSUM_ROWS = 64
CHUNK = 64
SUB = 16
NLAG = CHUNK // SUB
LAG_OFFSETS = [sum(CHUNK - SUB * m for m in range(l)) for l in range(NLAG + 1)]
LAG_ROWS = LAG_OFFSETS[NLAG]
TAIL_ROWS = 1024
TAIL_SUBTILES = 2


def _rmsnorm(x, w):
    return x * lax.rsqrt(jnp.mean(x * x, axis=-1, keepdims=True) + EPS) * w


def _sigmoid(x):
    return 1.0 / (1.0 + jnp.exp2(x * (-LOG2_E)))


def _silu(x):
    return x * _sigmoid(x)


def _log_sigmoid(x):
    return -(jnp.maximum(-x, 0.0) + jnp.log(1.0 + jnp.exp2(jnp.abs(x) * (-LOG2_E))))


def _transpose_projection_weight(wt_ref, wa_ref, wbraw_ref, wb_ref):
    turn = lambda lo, n: wt_ref[lo:lo + n, :].astype(BF16).T
    for r in range(0, GLA_COLS, TRANSPOSE_ROWS):
        wa_ref[:, r:r + TRANSPOSE_ROWS] = turn(r, TRANSPOSE_ROWS)
    wbraw_ref[...] = turn(GLA_COLS, LANES)
    hgrn0 = GLA_COLS + GLA_GATE_RANK
    for r in range(0, HGRN_COLS, TRANSPOSE_ROWS):
        wb_ref[:, r:r + TRANSPOSE_ROWS] = turn(hgrn0 + r, TRANSPOSE_ROWS)


def _projection_program(x_ref, wn_ref, wa_ref, wbraw_ref, wb_ref, wgk_ref, bgk_ref, lbl_ref,
                        gla, hgrn):
    hb = _rmsnorm(x_ref[...], wn_ref[...]).astype(BF16)

    def proj(w_ref, lo, width):
        return jnp.dot(hb, w_ref[:, lo:lo + width], preferred_element_type=F32)

    def decays_and_operands():
        lbl = lbl_ref[...]
        e = jnp.exp(lbl - jnp.max(lbl, axis=0, keepdims=True))
        lb = e[0:1, :] / jnp.sum(e, axis=0, keepdims=True)
        f = lb + (1.0 - lb) * _sigmoid(proj(wb_ref, _C_HF, HGRN_F))
        hgrn["k"][...] = 1.0 - f
        hgrn["g"][...] = jnp.log(f) * LOG2_E
        glr = proj(wbraw_ref, 0, LANES)[:, :GLA_GATE_RANK].astype(BF16)
        gla["q"][...] = proj(wa_ref, _C_GQ, GLA_QK) * (GLA_DK ** -0.5)
        gla["k"][...] = proj(wa_ref, _C_GK, GLA_QK)
        gla["v"][...] = proj(wa_ref, _C_GV, GLA_V).astype(BF16)
        gk = jnp.dot(glr, wgk_ref[...].astype(BF16), preferred_element_type=F32) + bgk_ref[...]
        gla["g"][...] = _log_sigmoid(gk) * (LOG2_E / GLA_GATE_NORM)
        hgrn["q"][...] = _silu(proj(wb_ref, _C_HQ, HGRN_F))
        hgrn["v"][...] = proj(wb_ref, _C_HI, HGRN_V).astype(BF16)

    def output_gates():
        t_ref, col0 = gla["gate"]
        t_ref[:, col0:col0 + GLA_V] = (
            gla["gain"][...] * _silu(proj(wa_ref, _C_GG, GLA_V))).astype(BF16)
        t_ref, col0 = hgrn["gate"]
        t_ref[:, col0:col0 + HGRN_V] = (
            hgrn["gain"][...] * _silu(proj(wb_ref, _C_HG, HGRN_V))).astype(BF16)

    return decays_and_operands, output_gates


def _mixer_program(q_ref, k_ref, g_ref, v_ref, o_ref, o_col0, b_ref, st_ref, *, heads, dk, dv):
    gw = max(dk, LANES)
    hpg = gw // dk
    ngroups = heads // hpg

    ri = lax.broadcasted_iota(jnp.int32, (CUMSUM_ROWS, 2 * CUMSUM_ROWS), 0)
    ci = lax.broadcasted_iota(jnp.int32, (CUMSUM_ROWS, 2 * CUMSUM_ROWS), 1) % CUMSUM_ROWS
    tri2 = jnp.where((ri // CHUNK == ci // CHUNK) & (ci <= ri), 1.0, 0.0).astype(BF16)

    def cumsum_block(r0):
        blk = slice(r0, r0 + CUMSUM_ROWS)
        g = g_ref[blk, :]
        g_hi = g.astype(BF16)
        g_lo = (g - g_hi.astype(F32)).astype(BF16)
        b_ref[blk, :] = jnp.dot(tri2, jnp.concatenate([g_hi, g_lo], axis=0),
                                preferred_element_type=F32)

    si = lax.broadcasted_iota(jnp.int32, (SUB, CHUNK), 0)
    sj = lax.broadcasted_iota(jnp.int32, (SUB, CHUNK), 1)
    col_block = [sj // SUB == r for r in range(NLAG)]
    diag_block = [col_block[r] & (sj - SUB * r <= si) for r in range(NLAG)]
    lane = lax.broadcasted_iota(jnp.int32, (1, gw), 1)

    head_ids = [(gi, hh) for gi in range(ngroups) for hh in range(hpg)]
    v_of = lambda h, c: v_ref[c * CHUNK:(c + 1) * CHUNK, h * dv:(h + 1) * dv]
    state = {(gi, hh): st_ref[gi * hpg + hh] for gi, hh in head_ids}
    probs, qin, start_state, prepared = {}, {}, {}, {}
    blk = lambda a, r: a[SUB * r:SUB * (r + 1), :]

    def prep(c):
        r0 = c * CHUNK
        rws = slice(r0, r0 + CHUNK)
        for gi in range(ngroups):
            lanes = slice(gi * gw, (gi + 1) * gw)
            b = b_ref[rws, lanes]
            q = q_ref[rws, lanes]
            k = k_ref[rws, lanes]
            ends = {r: b_ref[r0 + SUB * r + SUB - 1:r0 + SUB * r + SUB, lanes] for r in range(NLAG)}
            ends[-1] = jnp.zeros((1, gw), F32)
            rows_of = lambda f: jnp.concatenate(
                [jnp.broadcast_to(f(r), (SUB, gw)) for r in range(NLAG)], axis=0)

            q0 = q * jnp.exp2(b - rows_of(lambda r: ends[r]))
            q1 = q * jnp.exp2(b - rows_of(lambda r: ends[r - 1]))
            k_hat = k * jnp.exp2(rows_of(lambda r: ends[r]) - b)
            lag_rows = [q0.astype(BF16), q1[SUB:, :].astype(BF16)]
            for l in range(2, NLAG):
                lag_rows.append(jnp.concatenate(
                    [blk(q1, r) * jnp.exp2(ends[r - 1] - ends[r - l]) for r in range(l, NLAG)],
                    axis=0).astype(BF16))
            q_lag = jnp.concatenate(lag_rows, axis=0)
            q_in = jnp.concatenate(
                [blk(q1, r) * jnp.exp2(ends[r - 1]) for r in range(NLAG)], axis=0)
            k_dec = jnp.concatenate(
                [blk(k_hat, r) * jnp.exp2(ends[NLAG - 1] - ends[r]) for r in range(NLAG)],
                axis=0).astype(BF16)
            dec = jnp.exp2(ends[NLAG - 1])
            prepared[gi, c] = (q_lag, k_dec, dec)
            for hh in range(hpg):
                if hpg == 1:
                    prepared[gi, hh, c], qin[gi, hh, c] = k_hat.astype(BF16), q_in.astype(BF16)
                else:
                    in_head = (lane >= hh * dk) & (lane < (hh + 1) * dk)
                    prepared[gi, hh, c] = jnp.where(in_head, k_hat, 0.0).astype(BF16)
                    qin[gi, hh, c] = jnp.where(in_head, q_in, 0.0).astype(BF16)

    def scores_and_state(c):
        for gi in range(ngroups):
            q_lag, k_dec, dec = prepared.pop((gi, c))
            for hh in range(hpg):
                h = gi * hpg + hh
                k_h = prepared.pop((gi, hh, c))
                s_all = lax.dot_general(q_lag, k_h, (((1,), (1,)), ((), ())),
                                        preferred_element_type=F32)
                p_rows = []
                for r in range(NLAG):
                    p_r = jnp.where(diag_block[r], blk(s_all, r), 0.0)
                    for l in range(1, r + 1):
                        src = LAG_OFFSETS[l] + SUB * (r - l)
                        p_r = jnp.where(col_block[r - l], s_all[src:src + SUB, :], p_r)
                    p_rows.append(p_r)
                probs[gi, hh, c] = jnp.concatenate(p_rows, axis=0).astype(BF16)
                upd = lax.dot_general(v_of(h, c), k_dec, (((0,), (0,)), ((), ())),
                                      preferred_element_type=F32)
                start_state[gi, hh, c] = state[gi, hh].astype(BF16).T
                state[gi, hh] = state[gi, hh] * dec + upd

    def outputs(c):
        for gi, hh in head_ids:
            h = gi * hpg + hh
            lhs = jnp.concatenate([qin.pop((gi, hh, c)), probs.pop((gi, hh, c))], axis=1)
            rhs = jnp.concatenate([start_state.pop((gi, hh, c)), v_of(h, c)], axis=0)
            o = jnp.dot(lhs, rhs, preferred_element_type=F32)
            o_ref[c * CHUNK:(c + 1) * CHUNK, o_col0 + h * dv:o_col0 + (h + 1) * dv] = o.astype(BF16)

    def store_state():
        for gi, hh in head_ids:
            st_ref[gi * hpg + hh] = state[gi, hh]

    return cumsum_block, prep, scores_and_state, outputs, store_state


_MIXERS = (dict(heads=GLA_HEADS, dk=GLA_DK, dv=GLA_DV), dict(heads=HGRN_HEADS, dk=HGRN_DF, dv=HGRN_DV))


def _mixer_kernel(x_ref, wn_ref, wt_ref, wgk_ref, bgk_ref, lbl_ref, nwa_ref, nwb_ref, *rest,
                  n_cast):
    cast_in, rest = rest[:n_cast], rest[n_cast:]
    (o_ref, t_ref), rest = rest[:2], rest[2:]
    cast_out, rest = rest[:n_cast], rest[n_cast:]
    (wa_ref, wbraw_ref, wb_ref,
     qa_ref, ka_ref, ga_ref, va_ref, ba_ref, sta_ref,
     qb_ref, kb_ref, gb_ref, vb_ref, bb_ref, stb_ref) = rest

    @pl.when((pl.program_id(0) == 0) & (pl.program_id(1) == 0))
    def _():
        _transpose_projection_weight(wt_ref, wa_ref, wbraw_ref, wb_ref)

    @pl.when(pl.program_id(1) == 0)
    def _():
        sta_ref[...] = jnp.zeros_like(sta_ref)
        stb_ref[...] = jnp.zeros_like(stb_ref)

    for src, dst in zip(cast_in, cast_out):
        dst[...] = src[...].astype(BF16)

    decays_and_operands, output_gates = _projection_program(
        x_ref, wn_ref, wa_ref, wbraw_ref, wb_ref, wgk_ref, bgk_ref, lbl_ref,
        dict(q=qa_ref, k=ka_ref, g=ga_ref, v=va_ref, gate=(t_ref, 0), gain=nwa_ref),
        dict(q=qb_ref, k=kb_ref, g=gb_ref, v=vb_ref, gate=(t_ref, GLA_V), gain=nwb_ref))
    programs = [
        _mixer_program(qa_ref, ka_ref, ga_ref, va_ref, o_ref, 0, ba_ref, sta_ref, **_MIXERS[0]),
        _mixer_program(qb_ref, kb_ref, gb_ref, vb_ref, o_ref, GLA_V, bb_ref, stb_ref, **_MIXERS[1]),
    ]
    rows = x_ref.shape[0]
    nchunk = rows // CHUNK

    decays_and_operands()
    for cumsum_block, _, _, _, _ in programs:
        for r0 in range(0, rows, CUMSUM_ROWS):
            cumsum_block(r0)
    output_gates()
    for c in range(nchunk):
        for _, prep, _, _, _ in programs:
            prep(c)
    for _, _, scores_and_state, outputs, store_state in programs:
        for c in range(nchunk):
            scores_and_state(c)
            if c >= 1:
                outputs(c - 1)
        outputs(nchunk - 1)
        store_state()


def _mixer_call(x2, wn, w_t, w_gk, bgk, lbl, nwa, nwb, cast_weights, *, batch):
    m = x2.shape[0]
    tm = MIXER_ROWS
    nt = m // batch // tm
    steps = batch * nt
    assert steps * tm == m and tm % CUMSUM_ROWS == 0 and CUMSUM_ROWS % CHUNK == 0
    assert all(w.shape[0] % steps == 0 for w in cast_weights)
    row = lambda width: pl.BlockSpec((tm, width), lambda b, i: (b * nt + i, 0))
    full = lambda a: pl.BlockSpec(a.shape, lambda b, i: (0,) * a.ndim)
    resident = lambda a: pl.BlockSpec(a.shape, lambda b, i: (0,) * a.ndim,
                                      pipeline_mode=pl.Buffered(1))
    slab = lambda a: pl.BlockSpec((a.shape[0] // steps, a.shape[1]), lambda b, i: (b * nt + i, 0))
    out_shapes = (jax.ShapeDtypeStruct((m, GLA_V + HGRN_V), BF16),) * 2
    cast_shapes = tuple(jax.ShapeDtypeStruct(w.shape, BF16) for w in cast_weights)
    scratch = [pltpu.VMEM((D_MODEL, GLA_COLS), BF16), pltpu.VMEM((D_MODEL, LANES), BF16),
               pltpu.VMEM((D_MODEL, HGRN_COLS), BF16)]
    for mix in _MIXERS:
        w, gw = mix["heads"] * mix["dk"], max(mix["dk"], LANES)
        wv = mix["heads"] * mix["dv"]
        scratch += [pltpu.VMEM((tm, w), F32), pltpu.VMEM((tm, w), F32), pltpu.VMEM((tm, w), F32),
                    pltpu.VMEM((tm, wv), BF16),
                    pltpu.VMEM((tm, w), F32), pltpu.VMEM((mix["heads"], mix["dv"], gw), F32)]
    outs = pl.pallas_call(
        functools.partial(_mixer_kernel, n_cast=len(cast_weights)),
        grid=(batch, nt),
        in_specs=[row(D_MODEL), full(wn), resident(w_t), full(w_gk), full(bgk), full(lbl),
                  full(nwa), full(nwb)] + [slab(w) for w in cast_weights],
        out_specs=tuple(row(s.shape[1]) for s in out_shapes) + tuple(slab(w) for w in cast_weights),
        out_shape=out_shapes + cast_shapes,
        scratch_shapes=scratch,
        compiler_params=pltpu.CompilerParams(
            dimension_semantics=("arbitrary", "arbitrary"), vmem_limit_bytes=VMEM_LIMIT_BYTES),
        name="proj_gla_hgrn_mixers",
    )(x2, wn, w_t, w_gk, bgk, lbl, nwa, nwb, *cast_weights)
    return outs[0], outs[1], outs[2:]


def _tail_kernel(x_ref, o_ref, t_ref, w_out_ref, post_mix_ref, pre_mlp_ref, post_mlp_ref,
                 w_up_ref, w_down_ref, out_ref):
    sub = x_ref.shape[0] // TAIL_SUBTILES
    tiles = [slice(s * sub, (s + 1) * sub) for s in range(TAIL_SUBTILES)]
    hs, us = {}, {}

    def out_proj(s):
        r = tiles[s]
        heads = []
        for c0 in range(0, GLA_V + HGRN_V, HEAD_DV):
            o = o_ref[r, c0:c0 + HEAD_DV].astype(F32)
            heads.append(o * lax.rsqrt(jnp.mean(o * o, axis=-1, keepdims=True) + EPS))
        y = (jnp.concatenate(heads, axis=-1) * t_ref[r, :].astype(F32)).astype(BF16)
        mix = jnp.dot(y, w_out_ref[...], preferred_element_type=F32)
        hs[s] = x_ref[r, :] + _rmsnorm(mix, post_mix_ref[...])

    def up(s):
        us[s] = jnp.dot(_rmsnorm(hs[s], pre_mlp_ref[...]).astype(BF16), w_up_ref[...],
                        preferred_element_type=F32)

    def down(s):
        relu = jnp.maximum(us.pop(s), 0.0)
        m = jnp.dot((relu * relu).astype(BF16), w_down_ref[...], preferred_element_type=F32)
        out_ref[tiles[s], :] = hs.pop(s) + _rmsnorm(m, post_mlp_ref[...])

    for s in range(TAIL_SUBTILES):
        out_proj(s)
    for s in range(TAIL_SUBTILES):
        up(s)
        down(s)


def _tail_call(x2, o, t, w_out, post_mix, pre_mlp, post_mlp, w_up, w_down):
    m = x2.shape[0]
    tm = TAIL_ROWS
    assert m % tm == 0 and tm % TAIL_SUBTILES == 0
    row = lambda width: pl.BlockSpec((tm, width), lambda i: (i, 0))
    full = lambda a: pl.BlockSpec(a.shape, lambda i: (0,) * a.ndim)
    resident = lambda a: pl.BlockSpec(a.shape, lambda i: (0,) * a.ndim,
                                      pipeline_mode=pl.Buffered(1))
    return pl.pallas_call(
        _tail_kernel,
        grid=(m // tm,),
        in_specs=[row(D_MODEL), row(GLA_V + HGRN_V), row(GLA_V + HGRN_V), resident(w_out),
                  full(post_mix), full(pre_mlp), full(post_mlp), resident(w_up), resident(w_down)],
        out_specs=row(D_MODEL),
        out_shape=jax.ShapeDtypeStruct((m, D_MODEL), F32),
        compiler_params=pltpu.CompilerParams(
            dimension_semantics=("parallel",), vmem_limit_bytes=VMEM_LIMIT_BYTES),
        name="out_proj_mlp",
    )(x2, o, t, w_out, post_mix, pre_mlp, post_mlp, w_up, w_down)


def kernel(x, w_in, w_gk_up, b_gk, gla_norm_w, hgrn_norm_w, hgrn_lower_bounds, w_out,
           pre_mix_norm, post_mix_norm, pre_mlp_norm, post_mlp_norm, w_up, w_down):
    batch, seq, d = x.shape
    assert d == D_MODEL and w_in.shape[0] == 1 and w_in.shape[2] == GLA_COLS + GLA_GATE_RANK + HGRN_COLS
    x2 = x.reshape(batch * seq, d)
    l = 0
    w_t = jnp.transpose(w_in[l])
    row2 = lambda a: a.reshape(1, -1)

    o, t, (w_out_b, w_up_b, w_down_b) = _mixer_call(
        x2, row2(pre_mix_norm[l]), w_t, w_gk_up[l], row2(b_gk[l]), hgrn_lower_bounds,
        row2(gla_norm_w[l]), row2(hgrn_norm_w[l]), (w_out[l], w_up[l], w_down[l]), batch=batch)
    out = _tail_call(x2, o, t, w_out_b, row2(post_mix_norm[l]), row2(pre_mlp_norm[l]),
                     row2(post_mlp_norm[l]), w_up_b, w_down_b)
    return out.reshape(batch, seq, d)
```

```python
import functools

import jax
import jax.numpy as jnp
from jax import lax
from jax.experimental import pallas as pl
from jax.experimental.pallas import tpu as pltpu

F32 = jnp.float32
BF16 = jnp.bfloat16

D_MODEL = 1024
GLA_HEADS, GLA_DK, GLA_DV = 4, 64, 128
GLA_QK = GLA_HEADS * GLA_DK
GLA_V = GLA_HEADS * GLA_DV
GLA_GATE_RANK = 16
GLA_GATE_NORM = 16.0
HGRN_HEADS, HGRN_DF, HGRN_DV = 4, 128, 128
HGRN_F = HGRN_HEADS * HGRN_DF
HGRN_V = HGRN_HEADS * HGRN_DV
assert GLA_DV == HGRN_DV
HEAD_DV = GLA_DV
EPS = 1e-6
LOG2_E = 1.4426950408889634

LANES = 128
VMEM_LIMIT_BYTES = 56 * 1024 * 1024

_C_GQ = 0
_C_GK = _C_GQ + GLA_QK
_C_GV = _C_GK + GLA_QK
_C_GG = _C_GV + GLA_V
GLA_COLS = _C_GG + GLA_V
_C_HQ = 0
_C_HF = _C_HQ + HGRN_F
_C_HI = _C_HF + HGRN_F
_C_HG = _C_HI + HGRN_V
HGRN_COLS = _C_HG + HGRN_V

MIXER_ROWS = 512
TRANSPOSE_ROWS = 512
CUMSUM_ROWS = 256
CHUNK = 64
SUB = 16
NLAG = CHUNK // SUB
LAG_OFFSETS = [sum(CHUNK - SUB * m for m in range(l)) for l in range(NLAG + 1)]
LAG_ROWS = LAG_OFFSETS[NLAG]
TAIL_ROWS = 1024
TAIL_SUBTILES = 2


def _rmsnorm(x, w):
    return x * lax.rsqrt(jnp.mean(x * x, axis=-1, keepdims=True) + EPS) * w


def _sigmoid(x):
    return 1.0 / (1.0 + jnp.exp2(x * (-LOG2_E)))


def _silu(x):
    return x * _sigmoid(x)


def _log_sigmoid(x):
    return -(jnp.maximum(-x, 0.0) + jnp.log(1.0 + jnp.exp2(jnp.abs(x) * (-LOG2_E))))


def _transpose_projection_weight(wt_ref, wa_ref, wbraw_ref, wb_ref):
    turn = lambda lo, n: wt_ref[lo:lo + n, :].astype(BF16).T
    for r in range(0, GLA_COLS, TRANSPOSE_ROWS):
        wa_ref[:, r:r + TRANSPOSE_ROWS] = turn(r, TRANSPOSE_ROWS)
    wbraw_ref[...] = turn(GLA_COLS, LANES)
    hgrn0 = GLA_COLS + GLA_GATE_RANK
    for r in range(0, HGRN_COLS, TRANSPOSE_ROWS):
        wb_ref[:, r:r + TRANSPOSE_ROWS] = turn(hgrn0 + r, TRANSPOSE_ROWS)


def _projection_program(x_ref, wn_ref, wa_ref, wbraw_ref, wb_ref, wgk_ref, bgk_ref, lbl_ref,
                        gla, hgrn):
    hb = _rmsnorm(x_ref[...], wn_ref[...]).astype(BF16)

    def proj(w_ref, lo, width):
        return jnp.dot(hb, w_ref[:, lo:lo + width], preferred_element_type=F32)

    def decays_and_operands():
        lbl = lbl_ref[...]
        e = jnp.exp(lbl - jnp.max(lbl, axis=0, keepdims=True))
        lb = e[0:1, :] / jnp.sum(e, axis=0, keepdims=True)
        f = lb + (1.0 - lb) * _sigmoid(proj(wb_ref, _C_HF, HGRN_F))
        hgrn["k"][...] = 1.0 - f
        hgrn["g"][...] = jnp.log(f) * LOG2_E
        glr = proj(wbraw_ref, 0, LANES)[:, :GLA_GATE_RANK].astype(BF16)
        gla["q"][...] = proj(wa_ref, _C_GQ, GLA_QK) * (GLA_DK ** -0.5)
        gla["k"][...] = proj(wa_ref, _C_GK, GLA_QK)
        gla["v"][...] = proj(wa_ref, _C_GV, GLA_V).astype(BF16)
        gk = jnp.dot(glr, wgk_ref[...].astype(BF16), preferred_element_type=F32) + bgk_ref[...]
        gla["g"][...] = _log_sigmoid(gk) * (LOG2_E / GLA_GATE_NORM)
        hgrn["q"][...] = _silu(proj(wb_ref, _C_HQ, HGRN_F))
        hgrn["v"][...] = proj(wb_ref, _C_HI, HGRN_V).astype(BF16)

    def output_gates():
        t_ref, col0 = gla["gate"]
        t_ref[:, col0:col0 + GLA_V] = (
            gla["gain"][...] * _silu(proj(wa_ref, _C_GG, GLA_V))).astype(BF16)
        t_ref, col0 = hgrn["gate"]
        t_ref[:, col0:col0 + HGRN_V] = (
            hgrn["gain"][...] * _silu(proj(wb_ref, _C_HG, HGRN_V))).astype(BF16)

    return decays_and_operands, output_gates


def _mixer_program(q_ref, k_ref, g_ref, v_ref, o_ref, o_col0, b_ref, st_ref, *, heads, dk, dv):
    gw = max(dk, LANES)
    hpg = gw // dk
    ngroups = heads // hpg

    ri = lax.broadcasted_iota(jnp.int32, (CUMSUM_ROWS, CUMSUM_ROWS), 0)
    ci = lax.broadcasted_iota(jnp.int32, (CUMSUM_ROWS, CUMSUM_ROWS), 1)
    tri = jnp.where((ri // CHUNK == ci // CHUNK) & (ci <= ri), 1.0, 0.0).astype(BF16)

    def cumsum_block(r0):
        blk = slice(r0, r0 + CUMSUM_ROWS)
        g = g_ref[blk, :]
        g_hi = g.astype(BF16)
        g_lo = (g - g_hi.astype(F32)).astype(BF16)
        b_ref[blk, :] = (jnp.dot(tri, g_hi, preferred_element_type=F32)
                         + jnp.dot(tri, g_lo, preferred_element_type=F32))

    si = lax.broadcasted_iota(jnp.int32, (SUB, CHUNK), 0)
    sj = lax.broadcasted_iota(jnp.int32, (SUB, CHUNK), 1)
    col_block = [sj // SUB == r for r in range(NLAG)]
    diag_block = [col_block[r] & (sj - SUB * r <= si) for r in range(NLAG)]
    lane = lax.broadcasted_iota(jnp.int32, (1, gw), 1)

    head_ids = [(gi, hh) for gi in range(ngroups) for hh in range(hpg)]
    v_of = lambda h, c: v_ref[c * CHUNK:(c + 1) * CHUNK, h * dv:(h + 1) * dv]
    state = {(gi, hh): st_ref[gi * hpg + hh] for gi, hh in head_ids}
    probs, qin, start_state, prepared = {}, {}, {}, {}
    blk = lambda a, r: a[SUB * r:SUB * (r + 1), :]

    def prep(c):
        r0 = c * CHUNK
        rws = slice(r0, r0 + CHUNK)
        for gi in range(ngroups):
            lanes = slice(gi * gw, (gi + 1) * gw)
            b = b_ref[rws, lanes]
            q = q_ref[rws, lanes]
            k = k_ref[rws, lanes]
            ends = {r: b_ref[r0 + SUB * r + SUB - 1:r0 + SUB * r + SUB, lanes] for r in range(NLAG)}
            ends[-1] = jnp.zeros((1, gw), F32)
            rows_of = lambda f: jnp.concatenate(
                [jnp.broadcast_to(f(r), (SUB, gw)) for r in range(NLAG)], axis=0)

            q0 = q * jnp.exp2(b - rows_of(lambda r: ends[r]))
            q1 = q * jnp.exp2(b - rows_of(lambda r: ends[r - 1]))
            k_hat = k * jnp.exp2(rows_of(lambda r: ends[r]) - b)
            lag_rows = [q0.astype(BF16), q1[SUB:, :].astype(BF16)]
            for l in range(2, NLAG):
                lag_rows.append(jnp.concatenate(
                    [blk(q1, r) * jnp.exp2(ends[r - 1] - ends[r - l]) for r in range(l, NLAG)],
                    axis=0).astype(BF16))
            q_lag = jnp.concatenate(lag_rows, axis=0)
            q_in = jnp.concatenate(
                [blk(q1, r) * jnp.exp2(ends[r - 1]) for r in range(NLAG)], axis=0)
            k_dec = jnp.concatenate(
                [blk(k_hat, r) * jnp.exp2(ends[NLAG - 1] - ends[r]) for r in range(NLAG)],
                axis=0).astype(BF16)
            dec = jnp.exp2(ends[NLAG - 1])
            prepared[gi, c] = (q_lag, k_dec, dec)
            for hh in range(hpg):
                if hpg == 1:
                    prepared[gi, hh, c], qin[gi, hh, c] = k_hat.astype(BF16), q_in.astype(BF16)
                else:
                    in_head = (lane >= hh * dk) & (lane < (hh + 1) * dk)
                    prepared[gi, hh, c] = jnp.where(in_head, k_hat, 0.0).astype(BF16)
                    qin[gi, hh, c] = jnp.where(in_head, q_in, 0.0).astype(BF16)

    def scores_and_state(c):
        for gi in range(ngroups):
            q_lag, k_dec, dec = prepared.pop((gi, c))
            for hh in range(hpg):
                h = gi * hpg + hh
                k_h = prepared.pop((gi, hh, c))
                s_all = lax.dot_general(q_lag, k_h, (((1,), (1,)), ((), ())),
                                        preferred_element_type=F32)
                p_rows = []
                for r in range(NLAG):
                    p_r = jnp.where(diag_block[r], blk(s_all, r), 0.0)
                    for l in range(1, r + 1):
                        src = LAG_OFFSETS[l] + SUB * (r - l)
                        p_r = jnp.where(col_block[r - l], s_all[src:src + SUB, :], p_r)
                    p_rows.append(p_r)
                probs[gi, hh, c] = jnp.concatenate(p_rows, axis=0).astype(BF16)
                upd = lax.dot_general(v_of(h, c), k_dec, (((0,), (0,)), ((), ())),
                                      preferred_element_type=F32)
                start_state[gi, hh, c] = state[gi, hh].astype(BF16).T
                state[gi, hh] = state[gi, hh] * dec + upd

    def outputs(c):
        for gi, hh in head_ids:
            h = gi * hpg + hh
            lhs = jnp.concatenate([qin.pop((gi, hh, c)), probs.pop((gi, hh, c))], axis=1)
            rhs = jnp.concatenate([start_state.pop((gi, hh, c)), v_of(h, c)], axis=0)
            o = jnp.dot(lhs, rhs, preferred_element_type=F32)
            o_ref[c * CHUNK:(c + 1) * CHUNK, o_col0 + h * dv:o_col0 + (h + 1) * dv] = o.astype(BF16)

    def store_state():
        for gi, hh in head_ids:
            st_ref[gi * hpg + hh] = state[gi, hh]

    return cumsum_block, prep, scores_and_state, outputs, store_state


_MIXERS = (dict(heads=GLA_HEADS, dk=GLA_DK, dv=GLA_DV), dict(heads=HGRN_HEADS, dk=HGRN_DF, dv=HGRN_DV))


def _mixer_kernel(x_ref, wn_ref, wt_ref, wgk_ref, bgk_ref, lbl_ref, nwa_ref, nwb_ref, *rest,
                  n_cast):
    cast_in, rest = rest[:n_cast], rest[n_cast:]
    (o_ref, t_ref), rest = rest[:2], rest[2:]
    cast_out, rest = rest[:n_cast], rest[n_cast:]
    (wa_ref, wbraw_ref, wb_ref,
     qa_ref, ka_ref, ga_ref, va_ref, ba_ref, sta_ref,
     qb_ref, kb_ref, gb_ref, vb_ref, bb_ref, stb_ref) = rest

    @pl.when((pl.program_id(0) == 0) & (pl.program_id(1) == 0))
    def _():
        _transpose_projection_weight(wt_ref, wa_ref, wbraw_ref, wb_ref)

    @pl.when(pl.program_id(1) == 0)
    def _():
        sta_ref[...] = jnp.zeros_like(sta_ref)
        stb_ref[...] = jnp.zeros_like(stb_ref)

    for src, dst in zip(cast_in, cast_out):
        dst[...] = src[...].astype(BF16)

    decays_and_operands, output_gates = _projection_program(
        x_ref, wn_ref, wa_ref, wbraw_ref, wb_ref, wgk_ref, bgk_ref, lbl_ref,
        dict(q=qa_ref, k=ka_ref, g=ga_ref, v=va_ref, gate=(t_ref, 0), gain=nwa_ref),
        dict(q=qb_ref, k=kb_ref, g=gb_ref, v=vb_ref, gate=(t_ref, GLA_V), gain=nwb_ref))
    programs = [
        _mixer_program(qa_ref, ka_ref, ga_ref, va_ref, o_ref, 0, ba_ref, sta_ref, **_MIXERS[0]),
        _mixer_program(qb_ref, kb_ref, gb_ref, vb_ref, o_ref, GLA_V, bb_ref, stb_ref, **_MIXERS[1]),
    ]
    rows = x_ref.shape[0]
    nchunk = rows // CHUNK

    decays_and_operands()
    for cumsum_block, _, _, _, _ in programs:
        for r0 in range(0, rows, CUMSUM_ROWS):
            cumsum_block(r0)
    output_gates()
    for c in range(nchunk):
        for _, prep, _, _, _ in programs:
            prep(c)
    for _, _, scores_and_state, outputs, store_state in programs:
        for c in range(nchunk):
            scores_and_state(c)
            if c >= 1:
                outputs(c - 1)
        outputs(nchunk - 1)
        store_state()


def _mixer_call(x2, wn, w_t, w_gk, bgk, lbl, nwa, nwb, cast_weights, *, batch):
    m = x2.shape[0]
    tm = MIXER_ROWS
    nt = m // batch // tm
    steps = batch * nt
    assert steps * tm == m and tm % CUMSUM_ROWS == 0 and CUMSUM_ROWS % CHUNK == 0
    assert all(w.shape[0] % steps == 0 for w in cast_weights)
    row = lambda width: pl.BlockSpec((tm, width), lambda b, i: (b * nt + i, 0))
    full = lambda a: pl.BlockSpec(a.shape, lambda b, i: (0,) * a.ndim)
    resident = lambda a: pl.BlockSpec(a.shape, lambda b, i: (0,) * a.ndim,
                                      pipeline_mode=pl.Buffered(1))
    slab = lambda a: pl.BlockSpec((a.shape[0] // steps, a.shape[1]), lambda b, i: (b * nt + i, 0))
    out_shapes = (jax.ShapeDtypeStruct((m, GLA_V + HGRN_V), BF16),) * 2
    cast_shapes = tuple(jax.ShapeDtypeStruct(w.shape, BF16) for w in cast_weights)
    scratch = [pltpu.VMEM((D_MODEL, GLA_COLS), BF16), pltpu.VMEM((D_MODEL, LANES), BF16),
               pltpu.VMEM((D_MODEL, HGRN_COLS), BF16)]
    for mix in _MIXERS:
        w, gw = mix["heads"] * mix["dk"], max(mix["dk"], LANES)
        wv = mix["heads"] * mix["dv"]
        scratch += [pltpu.VMEM((tm, w), F32), pltpu.VMEM((tm, w), F32), pltpu.VMEM((tm, w), F32),
                    pltpu.VMEM((tm, wv), BF16),
                    pltpu.VMEM((tm, w), F32), pltpu.VMEM((mix["heads"], mix["dv"], gw), F32)]
    outs = pl.pallas_call(
        functools.partial(_mixer_kernel, n_cast=len(cast_weights)),
        grid=(batch, nt),
        in_specs=[row(D_MODEL), full(wn), resident(w_t), full(w_gk), full(bgk), full(lbl),
                  full(nwa), full(nwb)] + [slab(w) for w in cast_weights],
        out_specs=tuple(row(s.shape[1]) for s in out_shapes) + tuple(slab(w) for w in cast_weights),
        out_shape=out_shapes + cast_shapes,
        scratch_shapes=scratch,
        compiler_params=pltpu.CompilerParams(
            dimension_semantics=("arbitrary", "arbitrary"), vmem_limit_bytes=VMEM_LIMIT_BYTES),
        name="proj_gla_hgrn_mixers",
    )(x2, wn, w_t, w_gk, bgk, lbl, nwa, nwb, *cast_weights)
    return outs[0], outs[1], outs[2:]


def _tail_kernel(x_ref, o_ref, t_ref, w_out_ref, post_mix_ref, pre_mlp_ref, post_mlp_ref,
                 w_up_ref, w_down_ref, out_ref):
    sub = x_ref.shape[0] // TAIL_SUBTILES
    tiles = [slice(s * sub, (s + 1) * sub) for s in range(TAIL_SUBTILES)]
    hs, us = {}, {}

    def out_proj(s):
        r = tiles[s]
        heads = []
        for c0 in range(0, GLA_V + HGRN_V, HEAD_DV):
            o = o_ref[r, c0:c0 + HEAD_DV].astype(F32)
            heads.append(o * lax.rsqrt(jnp.mean(o * o, axis=-1, keepdims=True) + EPS))
        y = (jnp.concatenate(heads, axis=-1) * t_ref[r, :].astype(F32)).astype(BF16)
        mix = jnp.dot(y, w_out_ref[...], preferred_element_type=F32)
        hs[s] = x_ref[r, :] + _rmsnorm(mix, post_mix_ref[...])

    def up(s):
        us[s] = jnp.dot(_rmsnorm(hs[s], pre_mlp_ref[...]).astype(BF16), w_up_ref[...],
                        preferred_element_type=F32)

    def down(s):
        relu = jnp.maximum(us.pop(s), 0.0)
        m = jnp.dot((relu * relu).astype(BF16), w_down_ref[...], preferred_element_type=F32)
        out_ref[tiles[s], :] = hs.pop(s) + _rmsnorm(m, post_mlp_ref[...])

    for s in range(TAIL_SUBTILES):
        out_proj(s)
    for s in range(TAIL_SUBTILES):
        up(s)
        down(s)


def _tail_call(x2, o, t, w_out, post_mix, pre_mlp, post_mlp, w_up, w_down):
    m = x2.shape[0]
    tm = TAIL_ROWS
    assert m % tm == 0 and tm % TAIL_SUBTILES == 0
    row = lambda width: pl.BlockSpec((tm, width), lambda i: (i, 0))
    full = lambda a: pl.BlockSpec(a.shape, lambda i: (0,) * a.ndim)
    resident = lambda a: pl.BlockSpec(a.shape, lambda i: (0,) * a.ndim,
                                      pipeline_mode=pl.Buffered(1))
    return pl.pallas_call(
        _tail_kernel,
        grid=(m // tm,),
        in_specs=[row(D_MODEL), row(GLA_V + HGRN_V), row(GLA_V + HGRN_V), resident(w_out),
                  full(post_mix), full(pre_mlp), full(post_mlp), resident(w_up), resident(w_down)],
        out_specs=row(D_MODEL),
        out_shape=jax.ShapeDtypeStruct((m, D_MODEL), F32),
        compiler_params=pltpu.CompilerParams(
            dimension_semantics=("parallel",), vmem_limit_bytes=VMEM_LIMIT_BYTES),
        name="out_proj_mlp",
    )(x2, o, t, w_out, post_mix, pre_mlp, post_mlp, w_up, w_down)


def kernel(x, w_in, w_gk_up, b_gk, gla_norm_w, hgrn_norm_w, hgrn_lower_bounds, w_out,
           pre_mix_norm, post_mix_norm, pre_mlp_norm, post_mlp_norm, w_up, w_down):
    batch, seq, d = x.shape
    assert d == D_MODEL and w_in.shape[0] == 1 and w_in.shape[2] == GLA_COLS + GLA_GATE_RANK + HGRN_COLS
    x2 = x.reshape(batch * seq, d)
    l = 0
    w_t = jnp.transpose(w_in[l])
    row2 = lambda a: a.reshape(1, -1)

    o, t, (w_out_b, w_up_b, w_down_b) = _mixer_call(
        x2, row2(pre_mix_norm[l]), w_t, w_gk_up[l], row2(b_gk[l]), hgrn_lower_bounds,
        row2(gla_norm_w[l]), row2(hgrn_norm_w[l]), (w_out[l], w_up[l], w_down[l]), batch=batch)
    out = _tail_call(x2, o, t, w_out_b, row2(post_mix_norm[l]), row2(pre_mlp_norm[l]),
                     row2(post_mlp_norm[l]), w_up_b, w_down_b)
    return out.reshape(batch, seq, d)
```

```python
import functools

import jax
import jax.numpy as jnp
from jax import lax
from jax.experimental import pallas as pl
from jax.experimental.pallas import tpu as pltpu

F32 = jnp.float32
BF16 = jnp.bfloat16

D_MODEL = 1024
GLA_HEADS, GLA_DK, GLA_DV = 4, 64, 128
GLA_QK = GLA_HEADS * GLA_DK
GLA_V = GLA_HEADS * GLA_DV
GLA_GATE_RANK = 16
GLA_GATE_NORM = 16.0
HGRN_HEADS, HGRN_DF, HGRN_DV = 4, 128, 128
HGRN_F = HGRN_HEADS * HGRN_DF
HGRN_V = HGRN_HEADS * HGRN_DV
assert GLA_DV == HGRN_DV
HEAD_DV = GLA_DV
EPS = 1e-6
LOG2_E = 1.4426950408889634

LANES = 128
VMEM_LIMIT_BYTES = 56 * 1024 * 1024

_C_GQ = 0
_C_GK = _C_GQ + GLA_QK
_C_GV = _C_GK + GLA_QK
_C_GG = _C_GV + GLA_V
GLA_COLS = _C_GG + GLA_V
_C_HQ = 0
_C_HF = _C_HQ + HGRN_F
_C_HI = _C_HF + HGRN_F
_C_HG = _C_HI + HGRN_V
HGRN_COLS = _C_HG + HGRN_V

MIXER_ROWS = 512
TRANSPOSE_ROWS = 512
CUMSUM_ROWS = 256
CHUNK = 64
SUB = 16
NLAG = CHUNK // SUB
LAG_OFFSETS = [sum(CHUNK - SUB * m for m in range(l)) for l in range(NLAG + 1)]
LAG_ROWS = LAG_OFFSETS[NLAG]
TAIL_ROWS = 1024
TAIL_SUBTILES = 2
TAIL_HIDDEN_COLS = 1024


def _rmsnorm(x, w):
    return x * lax.rsqrt(jnp.mean(x * x, axis=-1, keepdims=True) + EPS) * w


def _sigmoid(x):
    return 1.0 / (1.0 + jnp.exp2(x * (-LOG2_E)))


def _silu(x):
    return x * _sigmoid(x)


def _log_sigmoid(x):
    return -(jnp.maximum(-x, 0.0) + jnp.log(1.0 + jnp.exp2(jnp.abs(x) * (-LOG2_E))))


def _transpose_projection_weight(wt_ref, wa_ref, wbraw_ref, wb_ref):
    turn = lambda lo, n: wt_ref[lo:lo + n, :].astype(BF16).T
    for r in range(0, GLA_COLS, TRANSPOSE_ROWS):
        wa_ref[:, r:r + TRANSPOSE_ROWS] = turn(r, TRANSPOSE_ROWS)
    wbraw_ref[...] = turn(GLA_COLS, LANES)
    hgrn0 = GLA_COLS + GLA_GATE_RANK
    for r in range(0, HGRN_COLS, TRANSPOSE_ROWS):
        wb_ref[:, r:r + TRANSPOSE_ROWS] = turn(hgrn0 + r, TRANSPOSE_ROWS)


def _projection_program(x_ref, wn_ref, wa_ref, wbraw_ref, wb_ref, wgk_ref, bgk_ref, lbl_ref,
                        gla, hgrn):
    hb = _rmsnorm(x_ref[...], wn_ref[...]).astype(BF16)

    def proj(w_ref, lo, width):
        return jnp.dot(hb, w_ref[:, lo:lo + width], preferred_element_type=F32)

    def decays_and_operands():
        lbl = lbl_ref[...]
        e = jnp.exp(lbl - jnp.max(lbl, axis=0, keepdims=True))
        lb = e[0:1, :] / jnp.sum(e, axis=0, keepdims=True)
        f = lb + (1.0 - lb) * _sigmoid(proj(wb_ref, _C_HF, HGRN_F))
        hgrn["k"][...] = 1.0 - f
        hgrn["g"][...] = jnp.log(f) * LOG2_E
        glr = proj(wbraw_ref, 0, LANES)[:, :GLA_GATE_RANK].astype(BF16)
        gla["q"][...] = proj(wa_ref, _C_GQ, GLA_QK) * (GLA_DK ** -0.5)
        gla["k"][...] = proj(wa_ref, _C_GK, GLA_QK)
        gla["v"][...] = proj(wa_ref, _C_GV, GLA_V).astype(BF16)
        gk = jnp.dot(glr, wgk_ref[...].astype(BF16), preferred_element_type=F32) + bgk_ref[...]
        gla["g"][...] = _log_sigmoid(gk) * (LOG2_E / GLA_GATE_NORM)
        hgrn["q"][...] = _silu(proj(wb_ref, _C_HQ, HGRN_F))
        hgrn["v"][...] = proj(wb_ref, _C_HI, HGRN_V).astype(BF16)

    def output_gates():
        t_ref, col0 = gla["gate"]
        t_ref[:, col0:col0 + GLA_V] = (
            gla["gain"][...] * _silu(proj(wa_ref, _C_GG, GLA_V))).astype(BF16)
        t_ref, col0 = hgrn["gate"]
        t_ref[:, col0:col0 + HGRN_V] = (
            hgrn["gain"][...] * _silu(proj(wb_ref, _C_HG, HGRN_V))).astype(BF16)

    return decays_and_operands, output_gates


def _mixer_program(q_ref, k_ref, g_ref, v_ref, o_ref, o_col0, b_ref, st_ref, *, heads, dk, dv):
    gw = max(dk, LANES)
    hpg = gw // dk
    ngroups = heads // hpg

    ri = lax.broadcasted_iota(jnp.int32, (CUMSUM_ROWS, CUMSUM_ROWS), 0)
    ci = lax.broadcasted_iota(jnp.int32, (CUMSUM_ROWS, CUMSUM_ROWS), 1)
    tri = jnp.where((ri // CHUNK == ci // CHUNK) & (ci <= ri), 1.0, 0.0).astype(BF16)

    def cumsum_block(r0):
        blk = slice(r0, r0 + CUMSUM_ROWS)
        g = g_ref[blk, :]
        g_hi = g.astype(BF16)
        g_lo = (g - g_hi.astype(F32)).astype(BF16)
        b_ref[blk, :] = (jnp.dot(tri, g_hi, preferred_element_type=F32)
                         + jnp.dot(tri, g_lo, preferred_element_type=F32))

    si = lax.broadcasted_iota(jnp.int32, (SUB, CHUNK), 0)
    sj = lax.broadcasted_iota(jnp.int32, (SUB, CHUNK), 1)
    col_block = [sj // SUB == r for r in range(NLAG)]
    diag_block = [col_block[r] & (sj - SUB * r <= si) for r in range(NLAG)]
    lane = lax.broadcasted_iota(jnp.int32, (1, gw), 1)

    head_ids = [(gi, hh) for gi in range(ngroups) for hh in range(hpg)]
    v_of = lambda h, c: v_ref[c * CHUNK:(c + 1) * CHUNK, h * dv:(h + 1) * dv]
    state = {(gi, hh): st_ref[gi * hpg + hh] for gi, hh in head_ids}
    probs, qin, start_state, prepared = {}, {}, {}, {}
    blk = lambda a, r: a[SUB * r:SUB * (r + 1), :]

    def prep(c):
        r0 = c * CHUNK
        rws = slice(r0, r0 + CHUNK)
        for gi in range(ngroups):
            lanes = slice(gi * gw, (gi + 1) * gw)
            b = b_ref[rws, lanes]
            q = q_ref[rws, lanes]
            k = k_ref[rws, lanes]
            ends = {r: b_ref[r0 + SUB * r + SUB - 1:r0 + SUB * r + SUB, lanes] for r in range(NLAG)}
            ends[-1] = jnp.zeros((1, gw), F32)
            rows_of = lambda f: jnp.concatenate(
                [jnp.broadcast_to(f(r), (SUB, gw)) for r in range(NLAG)], axis=0)

            q0 = q * jnp.exp2(b - rows_of(lambda r: ends[r]))
            q1 = q * jnp.exp2(b - rows_of(lambda r: ends[r - 1]))
            k_hat = k * jnp.exp2(rows_of(lambda r: ends[r]) - b)
            lag_rows = [q0.astype(BF16), q1[SUB:, :].astype(BF16)]
            for l in range(2, NLAG):
                lag_rows.append(jnp.concatenate(
                    [blk(q1, r) * jnp.exp2(ends[r - 1] - ends[r - l]) for r in range(l, NLAG)],
                    axis=0).astype(BF16))
            q_lag = jnp.concatenate(lag_rows, axis=0)
            q_in = jnp.concatenate(
                [blk(q1, r) * jnp.exp2(ends[r - 1]) for r in range(NLAG)], axis=0)
            k_dec = jnp.concatenate(
                [blk(k_hat, r) * jnp.exp2(ends[NLAG - 1] - ends[r]) for r in range(NLAG)],
                axis=0).astype(BF16)
            dec = jnp.exp2(ends[NLAG - 1])
            prepared[gi, c] = (q_lag, k_dec, dec)
            for hh in range(hpg):
                if hpg == 1:
                    prepared[gi, hh, c], qin[gi, hh, c] = k_hat.astype(BF16), q_in.astype(BF16)
                else:
                    in_head = (lane >= hh * dk) & (lane < (hh + 1) * dk)
                    prepared[gi, hh, c] = jnp.where(in_head, k_hat, 0.0).astype(BF16)
                    qin[gi, hh, c] = jnp.where(in_head, q_in, 0.0).astype(BF16)

    def scores_and_state(c):
        for gi in range(ngroups):
            q_lag, k_dec, dec = prepared.pop((gi, c))
            for hh in range(hpg):
                h = gi * hpg + hh
                k_h = prepared.pop((gi, hh, c))
                s_all = lax.dot_general(q_lag, k_h, (((1,), (1,)), ((), ())),
                                        preferred_element_type=F32)
                p_rows = []
                for r in range(NLAG):
                    p_r = jnp.where(diag_block[r], blk(s_all, r), 0.0)
                    for l in range(1, r + 1):
                        src = LAG_OFFSETS[l] + SUB * (r - l)
                        p_r = jnp.where(col_block[r - l], s_all[src:src + SUB, :], p_r)
                    p_rows.append(p_r)
                probs[gi, hh, c] = jnp.concatenate(p_rows, axis=0).astype(BF16)
                upd = lax.dot_general(v_of(h, c), k_dec, (((0,), (0,)), ((), ())),
                                      preferred_element_type=F32)
                start_state[gi, hh, c] = state[gi, hh].astype(BF16).T
                state[gi, hh] = state[gi, hh] * dec + upd

    def outputs(c):
        for gi, hh in head_ids:
            h = gi * hpg + hh
            lhs = jnp.concatenate([qin.pop((gi, hh, c)), probs.pop((gi, hh, c))], axis=1)
            rhs = jnp.concatenate([start_state.pop((gi, hh, c)), v_of(h, c)], axis=0)
            o = jnp.dot(lhs, rhs, preferred_element_type=F32)
            o_ref[c * CHUNK:(c + 1) * CHUNK, o_col0 + h * dv:o_col0 + (h + 1) * dv] = o.astype(BF16)

    def store_state():
        for gi, hh in head_ids:
            st_ref[gi * hpg + hh] = state[gi, hh]

    return cumsum_block, prep, scores_and_state, outputs, store_state


_MIXERS = (dict(heads=GLA_HEADS, dk=GLA_DK, dv=GLA_DV), dict(heads=HGRN_HEADS, dk=HGRN_DF, dv=HGRN_DV))


def _mixer_kernel(x_ref, wn_ref, wt_ref, wgk_ref, bgk_ref, lbl_ref, nwa_ref, nwb_ref, *rest,
                  n_cast):
    cast_in, rest = rest[:n_cast], rest[n_cast:]
    (o_ref, t_ref), rest = rest[:2], rest[2:]
    cast_out, rest = rest[:n_cast], rest[n_cast:]
    (wa_ref, wbraw_ref, wb_ref,
     qa_ref, ka_ref, ga_ref, va_ref, ba_ref, sta_ref,
     qb_ref, kb_ref, gb_ref, vb_ref, bb_ref, stb_ref) = rest

    @pl.when((pl.program_id(0) == 0) & (pl.program_id(1) == 0))
    def _():
        _transpose_projection_weight(wt_ref, wa_ref, wbraw_ref, wb_ref)

    @pl.when(pl.program_id(1) == 0)
    def _():
        sta_ref[...] = jnp.zeros_like(sta_ref)
        stb_ref[...] = jnp.zeros_like(stb_ref)

    for src, dst in zip(cast_in, cast_out):
        dst[...] = src[...].astype(BF16)

    decays_and_operands, output_gates = _projection_program(
        x_ref, wn_ref, wa_ref, wbraw_ref, wb_ref, wgk_ref, bgk_ref, lbl_ref,
        dict(q=qa_ref, k=ka_ref, g=ga_ref, v=va_ref, gate=(t_ref, 0), gain=nwa_ref),
        dict(q=qb_ref, k=kb_ref, g=gb_ref, v=vb_ref, gate=(t_ref, GLA_V), gain=nwb_ref))
    programs = [
        _mixer_program(qa_ref, ka_ref, ga_ref, va_ref, o_ref, 0, ba_ref, sta_ref, **_MIXERS[0]),
        _mixer_program(qb_ref, kb_ref, gb_ref, vb_ref, o_ref, GLA_V, bb_ref, stb_ref, **_MIXERS[1]),
    ]
    rows = x_ref.shape[0]
    nchunk = rows // CHUNK

    decays_and_operands()
    for cumsum_block, _, _, _, _ in programs:
        for r0 in range(0, rows, CUMSUM_ROWS):
            cumsum_block(r0)
    output_gates()
    for c in range(nchunk):
        for _, prep, _, _, _ in programs:
            prep(c)
    for _, _, scores_and_state, outputs, store_state in programs:
        for c in range(nchunk):
            scores_and_state(c)
            if c >= 1:
                outputs(c - 1)
        outputs(nchunk - 1)
        store_state()


def _mixer_call(x2, wn, w_t, w_gk, bgk, lbl, nwa, nwb, cast_weights, *, batch):
    m = x2.shape[0]
    tm = MIXER_ROWS
    nt = m // batch // tm
    steps = batch * nt
    assert steps * tm == m and tm % CUMSUM_ROWS == 0 and CUMSUM_ROWS % CHUNK == 0
    assert all(w.shape[0] % steps == 0 for w in cast_weights)
    row = lambda width: pl.BlockSpec((tm, width), lambda b, i: (b * nt + i, 0))
    full = lambda a: pl.BlockSpec(a.shape, lambda b, i: (0,) * a.ndim)
    resident = lambda a: pl.BlockSpec(a.shape, lambda b, i: (0,) * a.ndim,
                                      pipeline_mode=pl.Buffered(1))
    slab = lambda a: pl.BlockSpec((a.shape[0] // steps, a.shape[1]), lambda b, i: (b * nt + i, 0))
    out_shapes = (jax.ShapeDtypeStruct((m, GLA_V + HGRN_V), BF16),) * 2
    cast_shapes = tuple(jax.ShapeDtypeStruct(w.shape, BF16) for w in cast_weights)
    scratch = [pltpu.VMEM((D_MODEL, GLA_COLS), BF16), pltpu.VMEM((D_MODEL, LANES), BF16),
               pltpu.VMEM((D_MODEL, HGRN_COLS), BF16)]
    for mix in _MIXERS:
        w, gw = mix["heads"] * mix["dk"], max(mix["dk"], LANES)
        wv = mix["heads"] * mix["dv"]
        scratch += [pltpu.VMEM((tm, w), F32), pltpu.VMEM((tm, w), F32), pltpu.VMEM((tm, w), F32),
                    pltpu.VMEM((tm, wv), BF16),
                    pltpu.VMEM((tm, w), F32), pltpu.VMEM((mix["heads"], mix["dv"], gw), F32)]
    outs = pl.pallas_call(
        functools.partial(_mixer_kernel, n_cast=len(cast_weights)),
        grid=(batch, nt),
        in_specs=[row(D_MODEL), full(wn), resident(w_t), full(w_gk), full(bgk), full(lbl),
                  full(nwa), full(nwb)] + [slab(w) for w in cast_weights],
        out_specs=tuple(row(s.shape[1]) for s in out_shapes) + tuple(slab(w) for w in cast_weights),
        out_shape=out_shapes + cast_shapes,
        scratch_shapes=scratch,
        compiler_params=pltpu.CompilerParams(
            dimension_semantics=("arbitrary", "arbitrary"), vmem_limit_bytes=VMEM_LIMIT_BYTES),
        name="proj_gla_hgrn_mixers",
    )(x2, wn, w_t, w_gk, bgk, lbl, nwa, nwb, *cast_weights)
    return outs[0], outs[1], outs[2:]


def _tail_kernel(x_ref, o_ref, t_ref, w_out_ref, post_mix_ref, pre_mlp_ref, post_mlp_ref,
                 w_up_ref, w_down_ref, out_ref):
    sub = x_ref.shape[0] // TAIL_SUBTILES
    tiles = [slice(s * sub, (s + 1) * sub) for s in range(TAIL_SUBTILES)]
    hs = {}

    def out_proj(s):
        r = tiles[s]
        heads = []
        for c0 in range(0, GLA_V + HGRN_V, HEAD_DV):
            o = o_ref[r, c0:c0 + HEAD_DV].astype(F32)
            heads.append(o * lax.rsqrt(jnp.mean(o * o, axis=-1, keepdims=True) + EPS))
        y = (jnp.concatenate(heads, axis=-1) * t_ref[r, :].astype(F32)).astype(BF16)
        mix = jnp.dot(y, w_out_ref[...], preferred_element_type=F32)
        hs[s] = x_ref[r, :] + _rmsnorm(mix, post_mix_ref[...])

    def mlp(s):
        hb = _rmsnorm(hs[s], pre_mlp_ref[...]).astype(BF16)
        pieces = list(range(0, w_up_ref.shape[1], TAIL_HIDDEN_COLS))
        acts, m = {}, None

        def up(c0):
            u = jnp.dot(hb, w_up_ref[:, c0:c0 + TAIL_HIDDEN_COLS], preferred_element_type=F32)
            relu = jnp.maximum(u, 0.0)
            acts[c0] = (relu * relu).astype(BF16)

        def down(c0, m):
            part = jnp.dot(acts.pop(c0), w_down_ref[c0:c0 + TAIL_HIDDEN_COLS, :],
                           preferred_element_type=F32)
            return part if m is None else m + part

        up(pieces[0])
        for prev, c0 in zip(pieces, pieces[1:]):
            up(c0)
            m = down(prev, m)
        m = down(pieces[-1], m)
        out_ref[tiles[s], :] = hs.pop(s) + _rmsnorm(m, post_mlp_ref[...])

    for s in range(TAIL_SUBTILES):
        out_proj(s)
    for s in range(TAIL_SUBTILES):
        mlp(s)


def _tail_call(x2, o, t, w_out, post_mix, pre_mlp, post_mlp, w_up, w_down):
    m = x2.shape[0]
    tm = TAIL_ROWS
    assert m % tm == 0 and tm % TAIL_SUBTILES == 0
    row = lambda width: pl.BlockSpec((tm, width), lambda i: (i, 0))
    full = lambda a: pl.BlockSpec(a.shape, lambda i: (0,) * a.ndim)
    resident = lambda a: pl.BlockSpec(a.shape, lambda i: (0,) * a.ndim,
                                      pipeline_mode=pl.Buffered(1))
    return pl.pallas_call(
        _tail_kernel,
        grid=(m // tm,),
        in_specs=[row(D_MODEL), row(GLA_V + HGRN_V), row(GLA_V + HGRN_V), resident(w_out),
                  full(post_mix), full(pre_mlp), full(post_mlp), resident(w_up), resident(w_down)],
        out_specs=row(D_MODEL),
        out_shape=jax.ShapeDtypeStruct((m, D_MODEL), F32),
        compiler_params=pltpu.CompilerParams(
            dimension_semantics=("parallel",), vmem_limit_bytes=VMEM_LIMIT_BYTES),
        name="out_proj_mlp",
    )(x2, o, t, w_out, post_mix, pre_mlp, post_mlp, w_up, w_down)


def kernel(x, w_in, w_gk_up, b_gk, gla_norm_w, hgrn_norm_w, hgrn_lower_bounds, w_out,
           pre_mix_norm, post_mix_norm, pre_mlp_norm, post_mlp_norm, w_up, w_down):
    batch, seq, d = x.shape
    assert d == D_MODEL and w_in.shape[0] == 1 and w_in.shape[2] == GLA_COLS + GLA_GATE_RANK + HGRN_COLS
    x2 = x.reshape(batch * seq, d)
    l = 0
    w_t = jnp.transpose(w_in[l])
    row2 = lambda a: a.reshape(1, -1)

    o, t, (w_out_b, w_up_b, w_down_b) = _mixer_call(
        x2, row2(pre_mix_norm[l]), w_t, w_gk_up[l], row2(b_gk[l]), hgrn_lower_bounds,
        row2(gla_norm_w[l]), row2(hgrn_norm_w[l]), (w_out[l], w_up[l], w_down[l]), batch=batch)
    out = _tail_call(x2, o, t, w_out_b, row2(post_mix_norm[l]), row2(pre_mlp_norm[l]),
                     row2(post_mlp_norm[l]), w_up_b, w_down_b)
    return out.reshape(batch, seq, d)
```

```python
import functools

import jax
import jax.numpy as jnp
from jax import lax
from jax.experimental import pallas as pl
from jax.experimental.pallas import tpu as pltpu

F32 = jnp.float32
BF16 = jnp.bfloat16

D_MODEL = 1024
GLA_HEADS, GLA_DK, GLA_DV = 4, 64, 128
GLA_QK = GLA_HEADS * GLA_DK
GLA_V = GLA_HEADS * GLA_DV
GLA_GATE_RANK = 16
GLA_GATE_NORM = 16.0
HGRN_HEADS, HGRN_DF, HGRN_DV = 4, 128, 128
HGRN_F = HGRN_HEADS * HGRN_DF
HGRN_V = HGRN_HEADS * HGRN_DV
assert GLA_DV == HGRN_DV
HEAD_DV = GLA_DV
EPS = 1e-6
LOG2_E = 1.4426950408889634

LANES = 128
VMEM_LIMIT_BYTES = 56 * 1024 * 1024

_C_GQ = 0
_C_GK = _C_GQ + GLA_QK
_C_GV = _C_GK + GLA_QK
_C_GG = _C_GV + GLA_V
GLA_COLS = _C_GG + GLA_V
_C_HQ = 0
_C_HF = _C_HQ + HGRN_F
_C_HI = _C_HF + HGRN_F
_C_HG = _C_HI + HGRN_V
HGRN_COLS = _C_HG + HGRN_V

MIXER_ROWS = 512
TRANSPOSE_ROWS = 512
CUMSUM_ROWS = 256
CHUNK = 64
SUB = 16
NLAG = CHUNK // SUB
LAG_OFFSETS = [sum(CHUNK - SUB * m for m in range(l)) for l in range(NLAG + 1)]
LAG_ROWS = LAG_OFFSETS[NLAG]
TAIL_ROWS = 1024
TAIL_SUBTILE_ROWS = (256, 384, 384)


def _rmsnorm(x, w):
    return x * lax.rsqrt(jnp.mean(x * x, axis=-1, keepdims=True) + EPS) * w


def _sigmoid(x):
    return 1.0 / (1.0 + jnp.exp2(x * (-LOG2_E)))


def _silu(x):
    return x * _sigmoid(x)


def _log_sigmoid(x):
    return -(jnp.maximum(-x, 0.0) + jnp.log(1.0 + jnp.exp2(jnp.abs(x) * (-LOG2_E))))


def _transpose_projection_weight(wt_ref, wa_ref, wbraw_ref, wb_ref):
    turn = lambda lo, n: wt_ref[lo:lo + n, :].astype(BF16).T
    for r in range(0, GLA_COLS, TRANSPOSE_ROWS):
        wa_ref[:, r:r + TRANSPOSE_ROWS] = turn(r, TRANSPOSE_ROWS)
    wbraw_ref[...] = turn(GLA_COLS, LANES)
    hgrn0 = GLA_COLS + GLA_GATE_RANK
    for r in range(0, HGRN_COLS, TRANSPOSE_ROWS):
        wb_ref[:, r:r + TRANSPOSE_ROWS] = turn(hgrn0 + r, TRANSPOSE_ROWS)


def _projection_program(x_ref, wn_ref, wa_ref, wbraw_ref, wb_ref, wgk_ref, bgk_ref, lbl_ref,
                        gla, hgrn):
    hb = _rmsnorm(x_ref[...], wn_ref[...]).astype(BF16)

    def proj(w_ref, lo, width):
        return jnp.dot(hb, w_ref[:, lo:lo + width], preferred_element_type=F32)

    def decays_and_operands():
        lbl = lbl_ref[...]
        e = jnp.exp(lbl - jnp.max(lbl, axis=0, keepdims=True))
        lb = e[0:1, :] / jnp.sum(e, axis=0, keepdims=True)
        f = lb + (1.0 - lb) * _sigmoid(proj(wb_ref, _C_HF, HGRN_F))
        hgrn["k"][...] = 1.0 - f
        hgrn["g"][...] = jnp.log(f) * LOG2_E
        glr = proj(wbraw_ref, 0, LANES)[:, :GLA_GATE_RANK].astype(BF16)
        gla["q"][...] = proj(wa_ref, _C_GQ, GLA_QK) * (GLA_DK ** -0.5)
        gla["k"][...] = proj(wa_ref, _C_GK, GLA_QK)
        gla["v"][...] = proj(wa_ref, _C_GV, GLA_V).astype(BF16)
        gk = jnp.dot(glr, wgk_ref[...].astype(BF16), preferred_element_type=F32) + bgk_ref[...]
        gla["g"][...] = _log_sigmoid(gk) * (LOG2_E / GLA_GATE_NORM)
        hgrn["q"][...] = _silu(proj(wb_ref, _C_HQ, HGRN_F))
        hgrn["v"][...] = proj(wb_ref, _C_HI, HGRN_V).astype(BF16)

    def output_gates():
        t_ref, col0 = gla["gate"]
        t_ref[:, col0:col0 + GLA_V] = (
            gla["gain"][...] * _silu(proj(wa_ref, _C_GG, GLA_V))).astype(BF16)
        t_ref, col0 = hgrn["gate"]
        t_ref[:, col0:col0 + HGRN_V] = (
            hgrn["gain"][...] * _silu(proj(wb_ref, _C_HG, HGRN_V))).astype(BF16)

    return decays_and_operands, output_gates


def _mixer_program(q_ref, k_ref, g_ref, v_ref, o_ref, o_col0, b_ref, st_ref, *, heads, dk, dv):
    gw = max(dk, LANES)
    hpg = gw // dk
    ngroups = heads // hpg

    ri = lax.broadcasted_iota(jnp.int32, (CUMSUM_ROWS, CUMSUM_ROWS), 0)
    ci = lax.broadcasted_iota(jnp.int32, (CUMSUM_ROWS, CUMSUM_ROWS), 1)
    tri = jnp.where((ri // CHUNK == ci // CHUNK) & (ci <= ri), 1.0, 0.0).astype(BF16)

    def cumsum_block(r0):
        blk = slice(r0, r0 + CUMSUM_ROWS)
        g = g_ref[blk, :]
        g_hi = g.astype(BF16)
        g_lo = (g - g_hi.astype(F32)).astype(BF16)
        b_ref[blk, :] = (jnp.dot(tri, g_hi, preferred_element_type=F32)
                         + jnp.dot(tri, g_lo, preferred_element_type=F32))

    si = lax.broadcasted_iota(jnp.int32, (SUB, CHUNK), 0)
    sj = lax.broadcasted_iota(jnp.int32, (SUB, CHUNK), 1)
    col_block = [sj // SUB == r for r in range(NLAG)]
    diag_block = [col_block[r] & (sj - SUB * r <= si) for r in range(NLAG)]
    lane = lax.broadcasted_iota(jnp.int32, (1, gw), 1)

    head_ids = [(gi, hh) for gi in range(ngroups) for hh in range(hpg)]
    v_of = lambda h, c: v_ref[c * CHUNK:(c + 1) * CHUNK, h * dv:(h + 1) * dv]
    state = {(gi, hh): st_ref[gi * hpg + hh] for gi, hh in head_ids}
    probs, qin, start_state, prepared = {}, {}, {}, {}
    blk = lambda a, r: a[SUB * r:SUB * (r + 1), :]

    def prep(c):
        r0 = c * CHUNK
        rws = slice(r0, r0 + CHUNK)
        for gi in range(ngroups):
            lanes = slice(gi * gw, (gi + 1) * gw)
            b = b_ref[rws, lanes]
            q = q_ref[rws, lanes]
            k = k_ref[rws, lanes]
            ends = {r: b_ref[r0 + SUB * r + SUB - 1:r0 + SUB * r + SUB, lanes] for r in range(NLAG)}
            ends[-1] = jnp.zeros((1, gw), F32)
            rows_of = lambda f: jnp.concatenate(
                [jnp.broadcast_to(f(r), (SUB, gw)) for r in range(NLAG)], axis=0)

            q0 = q * jnp.exp2(b - rows_of(lambda r: ends[r]))
            q1 = q * jnp.exp2(b - rows_of(lambda r: ends[r - 1]))
            k_hat = k * jnp.exp2(rows_of(lambda r: ends[r]) - b)
            lag_rows = [q0.astype(BF16), q1[SUB:, :].astype(BF16)]
            for l in range(2, NLAG):
                lag_rows.append(jnp.concatenate(
                    [blk(q1, r) * jnp.exp2(ends[r - 1] - ends[r - l]) for r in range(l, NLAG)],
                    axis=0).astype(BF16))
            q_lag = jnp.concatenate(lag_rows, axis=0)
            q_in = jnp.concatenate(
                [blk(q1, r) * jnp.exp2(ends[r - 1]) for r in range(NLAG)], axis=0)
            k_dec = jnp.concatenate(
                [blk(k_hat, r) * jnp.exp2(ends[NLAG - 1] - ends[r]) for r in range(NLAG)],
                axis=0).astype(BF16)
            dec = jnp.exp2(ends[NLAG - 1])
            prepared[gi, c] = (q_lag, k_dec, dec)
            for hh in range(hpg):
                if hpg == 1:
                    prepared[gi, hh, c], qin[gi, hh, c] = k_hat.astype(BF16), q_in.astype(BF16)
                else:
                    in_head = (lane >= hh * dk) & (lane < (hh + 1) * dk)
                    prepared[gi, hh, c] = jnp.where(in_head, k_hat, 0.0).astype(BF16)
                    qin[gi, hh, c] = jnp.where(in_head, q_in, 0.0).astype(BF16)

    def scores_and_state(c):
        for gi in range(ngroups):
            q_lag, k_dec, dec = prepared.pop((gi, c))
            for hh in range(hpg):
                h = gi * hpg + hh
                k_h = prepared.pop((gi, hh, c))
                s_all = lax.dot_general(q_lag, k_h, (((1,), (1,)), ((), ())),
                                        preferred_element_type=F32)
                p_rows = []
                for r in range(NLAG):
                    p_r = jnp.where(diag_block[r], blk(s_all, r), 0.0)
                    for l in range(1, r + 1):
                        src = LAG_OFFSETS[l] + SUB * (r - l)
                        p_r = jnp.where(col_block[r - l], s_all[src:src + SUB, :], p_r)
                    p_rows.append(p_r)
                probs[gi, hh, c] = jnp.concatenate(p_rows, axis=0).astype(BF16)
                upd = lax.dot_general(v_of(h, c), k_dec, (((0,), (0,)), ((), ())),
                                      preferred_element_type=F32)
                start_state[gi, hh, c] = state[gi, hh].astype(BF16).T
                state[gi, hh] = state[gi, hh] * dec + upd

    def outputs(c):
        for gi, hh in head_ids:
            h = gi * hpg + hh
            lhs = jnp.concatenate([qin.pop((gi, hh, c)), probs.pop((gi, hh, c))], axis=1)
            rhs = jnp.concatenate([start_state.pop((gi, hh, c)), v_of(h, c)], axis=0)
            o = jnp.dot(lhs, rhs, preferred_element_type=F32)
            o_ref[c * CHUNK:(c + 1) * CHUNK, o_col0 + h * dv:o_col0 + (h + 1) * dv] = o.astype(BF16)

    def store_state():
        for gi, hh in head_ids:
            st_ref[gi * hpg + hh] = state[gi, hh]

    return cumsum_block, prep, scores_and_state, outputs, store_state


_MIXERS = (dict(heads=GLA_HEADS, dk=GLA_DK, dv=GLA_DV), dict(heads=HGRN_HEADS, dk=HGRN_DF, dv=HGRN_DV))


def _mixer_kernel(x_ref, wn_ref, wt_ref, wgk_ref, bgk_ref, lbl_ref, nwa_ref, nwb_ref, *rest,
                  n_cast):
    cast_in, rest = rest[:n_cast], rest[n_cast:]
    (o_ref, t_ref), rest = rest[:2], rest[2:]
    cast_out, rest = rest[:n_cast], rest[n_cast:]
    (wa_ref, wbraw_ref, wb_ref,
     qa_ref, ka_ref, ga_ref, va_ref, ba_ref, sta_ref,
     qb_ref, kb_ref, gb_ref, vb_ref, bb_ref, stb_ref) = rest

    @pl.when((pl.program_id(0) == 0) & (pl.program_id(1) == 0))
    def _():
        _transpose_projection_weight(wt_ref, wa_ref, wbraw_ref, wb_ref)

    @pl.when(pl.program_id(1) == 0)
    def _():
        sta_ref[...] = jnp.zeros_like(sta_ref)
        stb_ref[...] = jnp.zeros_like(stb_ref)

    for src, dst in zip(cast_in, cast_out):
        dst[...] = src[...].astype(BF16)

    decays_and_operands, output_gates = _projection_program(
        x_ref, wn_ref, wa_ref, wbraw_ref, wb_ref, wgk_ref, bgk_ref, lbl_ref,
        dict(q=qa_ref, k=ka_ref, g=ga_ref, v=va_ref, gate=(t_ref, 0), gain=nwa_ref),
        dict(q=qb_ref, k=kb_ref, g=gb_ref, v=vb_ref, gate=(t_ref, GLA_V), gain=nwb_ref))
    programs = [
        _mixer_program(qa_ref, ka_ref, ga_ref, va_ref, o_ref, 0, ba_ref, sta_ref, **_MIXERS[0]),
        _mixer_program(qb_ref, kb_ref, gb_ref, vb_ref, o_ref, GLA_V, bb_ref, stb_ref, **_MIXERS[1]),
    ]
    rows = x_ref.shape[0]
    nchunk = rows // CHUNK

    decays_and_operands()
    for cumsum_block, _, _, _, _ in programs:
        for r0 in range(0, rows, CUMSUM_ROWS):
            cumsum_block(r0)
    output_gates()
    for c in range(nchunk):
        for _, prep, _, _, _ in programs:
            prep(c)
    for _, _, scores_and_state, outputs, store_state in programs:
        for c in range(nchunk):
            scores_and_state(c)
            if c >= 1:
                outputs(c - 1)
        outputs(nchunk - 1)
        store_state()


def _mixer_call(x2, wn, w_t, w_gk, bgk, lbl, nwa, nwb, cast_weights, *, batch):
    m = x2.shape[0]
    tm = MIXER_ROWS
    nt = m // batch // tm
    steps = batch * nt
    assert steps * tm == m and tm % CUMSUM_ROWS == 0 and CUMSUM_ROWS % CHUNK == 0
    assert all(w.shape[0] % steps == 0 for w in cast_weights)
    row = lambda width: pl.BlockSpec((tm, width), lambda b, i: (b * nt + i, 0))
    full = lambda a: pl.BlockSpec(a.shape, lambda b, i: (0,) * a.ndim)
    resident = lambda a: pl.BlockSpec(a.shape, lambda b, i: (0,) * a.ndim,
                                      pipeline_mode=pl.Buffered(1))
    slab = lambda a: pl.BlockSpec((a.shape[0] // steps, a.shape[1]), lambda b, i: (b * nt + i, 0))
    out_shapes = (jax.ShapeDtypeStruct((m, GLA_V + HGRN_V), BF16),) * 2
    cast_shapes = tuple(jax.ShapeDtypeStruct(w.shape, BF16) for w in cast_weights)
    scratch = [pltpu.VMEM((D_MODEL, GLA_COLS), BF16), pltpu.VMEM((D_MODEL, LANES), BF16),
               pltpu.VMEM((D_MODEL, HGRN_COLS), BF16)]
    for mix in _MIXERS:
        w, gw = mix["heads"] * mix["dk"], max(mix["dk"], LANES)
        wv = mix["heads"] * mix["dv"]
        scratch += [pltpu.VMEM((tm, w), F32), pltpu.VMEM((tm, w), F32), pltpu.VMEM((tm, w), F32),
                    pltpu.VMEM((tm, wv), BF16),
                    pltpu.VMEM((tm, w), F32), pltpu.VMEM((mix["heads"], mix["dv"], gw), F32)]
    outs = pl.pallas_call(
        functools.partial(_mixer_kernel, n_cast=len(cast_weights)),
        grid=(batch, nt),
        in_specs=[row(D_MODEL), full(wn), resident(w_t), full(w_gk), full(bgk), full(lbl),
                  full(nwa), full(nwb)] + [slab(w) for w in cast_weights],
        out_specs=tuple(row(s.shape[1]) for s in out_shapes) + tuple(slab(w) for w in cast_weights),
        out_shape=out_shapes + cast_shapes,
        scratch_shapes=scratch,
        compiler_params=pltpu.CompilerParams(
            dimension_semantics=("arbitrary", "arbitrary"), vmem_limit_bytes=VMEM_LIMIT_BYTES),
        name="proj_gla_hgrn_mixers",
    )(x2, wn, w_t, w_gk, bgk, lbl, nwa, nwb, *cast_weights)
    return outs[0], outs[1], outs[2:]


def _tail_kernel(x_ref, o_ref, t_ref, w_out_ref, post_mix_ref, pre_mlp_ref, post_mlp_ref,
                 w_up_ref, w_down_ref, out_ref):
    starts = [0] + [sum(TAIL_SUBTILE_ROWS[:s + 1]) for s in range(len(TAIL_SUBTILE_ROWS))]
    tiles = [slice(lo, hi) for lo, hi in zip(starts, starts[1:])]
    hs, us = {}, {}

    def out_proj(s):
        r = tiles[s]
        heads = []
        for c0 in range(0, GLA_V + HGRN_V, HEAD_DV):
            o = o_ref[r, c0:c0 + HEAD_DV].astype(F32)
            heads.append(o * lax.rsqrt(jnp.mean(o * o, axis=-1, keepdims=True) + EPS))
        y = (jnp.concatenate(heads, axis=-1) * t_ref[r, :].astype(F32)).astype(BF16)
        mix = jnp.dot(y, w_out_ref[...], preferred_element_type=F32)
        hs[s] = x_ref[r, :] + _rmsnorm(mix, post_mix_ref[...])

    def up(s):
        us[s] = jnp.dot(_rmsnorm(hs[s], pre_mlp_ref[...]).astype(BF16), w_up_ref[...],
                        preferred_element_type=F32)

    def down(s):
        relu = jnp.maximum(us.pop(s), 0.0)
        m = jnp.dot((relu * relu).astype(BF16), w_down_ref[...], preferred_element_type=F32)
        out_ref[tiles[s], :] = hs.pop(s) + _rmsnorm(m, post_mlp_ref[...])

    for s in range(len(tiles)):
        out_proj(s)
    for s in range(len(tiles)):
        up(s)
        down(s)


def _tail_call(x2, o, t, w_out, post_mix, pre_mlp, post_mlp, w_up, w_down):
    m = x2.shape[0]
    tm = TAIL_ROWS
    assert m % tm == 0 and sum(TAIL_SUBTILE_ROWS) == tm
    row = lambda width: pl.BlockSpec((tm, width), lambda i: (i, 0))
    full = lambda a: pl.BlockSpec(a.shape, lambda i: (0,) * a.ndim)
    resident = lambda a: pl.BlockSpec(a.shape, lambda i: (0,) * a.ndim,
                                      pipeline_mode=pl.Buffered(1))
    return pl.pallas_call(
        _tail_kernel,
        grid=(m // tm,),
        in_specs=[row(D_MODEL), row(GLA_V + HGRN_V), row(GLA_V + HGRN_V), resident(w_out),
                  full(post_mix), full(pre_mlp), full(post_mlp), resident(w_up), resident(w_down)],
        out_specs=row(D_MODEL),
        out_shape=jax.ShapeDtypeStruct((m, D_MODEL), F32),
        compiler_params=pltpu.CompilerParams(
            dimension_semantics=("parallel",), vmem_limit_bytes=VMEM_LIMIT_BYTES),
        name="out_proj_mlp",
    )(x2, o, t, w_out, post_mix, pre_mlp, post_mlp, w_up, w_down)


def kernel(x, w_in, w_gk_up, b_gk, gla_norm_w, hgrn_norm_w, hgrn_lower_bounds, w_out,
           pre_mix_norm, post_mix_norm, pre_mlp_norm, post_mlp_norm, w_up, w_down):
    batch, seq, d = x.shape
    assert d == D_MODEL and w_in.shape[0] == 1 and w_in.shape[2] == GLA_COLS + GLA_GATE_RANK + HGRN_COLS
    x2 = x.reshape(batch * seq, d)
    l = 0
    w_t = jnp.transpose(w_in[l])
    row2 = lambda a: a.reshape(1, -1)

    o, t, (w_out_b, w_up_b, w_down_b) = _mixer_call(
        x2, row2(pre_mix_norm[l]), w_t, w_gk_up[l], row2(b_gk[l]), hgrn_lower_bounds,
        row2(gla_norm_w[l]), row2(hgrn_norm_w[l]), (w_out[l], w_up[l], w_down[l]), batch=batch)
    out = _tail_call(x2, o, t, w_out_b, row2(post_mix_norm[l]), row2(pre_mlp_norm[l]),
                     row2(post_mlp_norm[l]), w_up_b, w_down_b)
    return out.reshape(batch, seq, d)
```

```python
import functools

import jax
import jax.numpy as jnp
from jax import lax
from jax.experimental import pallas as pl
from jax.experimental.pallas import tpu as pltpu

F32 = jnp.float32
BF16 = jnp.bfloat16

D_MODEL = 1024
GLA_HEADS, GLA_DK, GLA_DV = 4, 64, 128
GLA_QK = GLA_HEADS * GLA_DK
GLA_V = GLA_HEADS * GLA_DV
GLA_GATE_RANK = 16
GLA_GATE_NORM = 16.0
HGRN_HEADS, HGRN_DF, HGRN_DV = 4, 128, 128
HGRN_F = HGRN_HEADS * HGRN_DF
HGRN_V = HGRN_HEADS * HGRN_DV
assert GLA_DV == HGRN_DV
HEAD_DV = GLA_DV
EPS = 1e-6
LOG2_E = 1.4426950408889634

LANES = 128
VMEM_LIMIT_BYTES = 56 * 1024 * 1024

_C_GQ = 0
_C_GK = _C_GQ + GLA_QK
_C_GV = _C_GK + GLA_QK
_C_GG = _C_GV + GLA_V
GLA_COLS = _C_GG + GLA_V
_C_HQ = 0
_C_HF = _C_HQ + HGRN_F
_C_HI = _C_HF + HGRN_F
_C_HG = _C_HI + HGRN_V
HGRN_COLS = _C_HG + HGRN_V

MIXER_ROWS = 512
TRANSPOSE_ROWS = 512
CUMSUM_ROWS = 256
CHUNK = 64
SUB = 16
NLAG = CHUNK // SUB
LAG_OFFSETS = [sum(CHUNK - SUB * m for m in range(l)) for l in range(NLAG + 1)]
LAG_ROWS = LAG_OFFSETS[NLAG]
TAIL_ROWS = 1024
TAIL_SUBTILES = 2


def _rmsnorm(x, w):
    return x * lax.rsqrt(jnp.mean(x * x, axis=-1, keepdims=True) + EPS) * w


def _sigmoid(x):
    return 1.0 / (1.0 + jnp.exp2(x * (-LOG2_E)))


def _silu(x):
    return x * _sigmoid(x)


def _log_sigmoid(x):
    return -(jnp.maximum(-x, 0.0) + jnp.log(1.0 + jnp.exp2(jnp.abs(x) * (-LOG2_E))))


def _transpose_projection_weight(wt_ref, wa_ref, wbraw_ref, wb_ref):
    turn = lambda lo, n: wt_ref[lo:lo + n, :].astype(BF16).T
    for r in range(0, GLA_COLS, TRANSPOSE_ROWS):
        wa_ref[:, r:r + TRANSPOSE_ROWS] = turn(r, TRANSPOSE_ROWS)
    wbraw_ref[...] = turn(GLA_COLS, LANES)
    hgrn0 = GLA_COLS + GLA_GATE_RANK
    for r in range(0, HGRN_COLS, TRANSPOSE_ROWS):
        wb_ref[:, r:r + TRANSPOSE_ROWS] = turn(hgrn0 + r, TRANSPOSE_ROWS)


def _projection_program(x_ref, wn_ref, wa_ref, wbraw_ref, wb_ref, wgk_ref, bgk_ref, lbl_ref,
                        gla, hgrn):
    hb = _rmsnorm(x_ref[...], wn_ref[...]).astype(BF16)

    def proj(w_ref, lo, width):
        return jnp.dot(hb, w_ref[:, lo:lo + width], preferred_element_type=F32)

    def decays_and_operands():
        lbl = lbl_ref[...]
        e = jnp.exp(lbl - jnp.max(lbl, axis=0, keepdims=True))
        lb = e[0:1, :] / jnp.sum(e, axis=0, keepdims=True)
        f = lb + (1.0 - lb) * _sigmoid(proj(wb_ref, _C_HF, HGRN_F))
        hgrn["k"][...] = 1.0 - f
        hgrn["g"][...] = jnp.log(f) * LOG2_E
        glr = proj(wbraw_ref, 0, LANES)[:, :GLA_GATE_RANK].astype(BF16)
        gla["q"][...] = proj(wa_ref, _C_GQ, GLA_QK) * (GLA_DK ** -0.5)
        gla["k"][...] = proj(wa_ref, _C_GK, GLA_QK)
        gla["v"][...] = proj(wa_ref, _C_GV, GLA_V).astype(BF16)
        gk = jnp.dot(glr, wgk_ref[...].astype(BF16), preferred_element_type=F32) + bgk_ref[...]
        gla["g"][...] = _log_sigmoid(gk) * (LOG2_E / GLA_GATE_NORM)
        hgrn["q"][...] = _silu(proj(wb_ref, _C_HQ, HGRN_F))
        hgrn["v"][...] = proj(wb_ref, _C_HI, HGRN_V).astype(BF16)

    def output_gates():
        t_ref, col0 = gla["gate"]
        t_ref[:, col0:col0 + GLA_V] = (
            gla["gain"][...] * _silu(proj(wa_ref, _C_GG, GLA_V))).astype(BF16)
        t_ref, col0 = hgrn["gate"]
        t_ref[:, col0:col0 + HGRN_V] = (
            hgrn["gain"][...] * _silu(proj(wb_ref, _C_HG, HGRN_V))).astype(BF16)

    return decays_and_operands, output_gates


def _mixer_program(q_ref, k_ref, g_ref, v_ref, o_ref, o_col0, b_ref, st_ref, *, heads, dk, dv):
    gw = max(dk, LANES)
    hpg = gw // dk
    ngroups = heads // hpg

    ri = lax.broadcasted_iota(jnp.int32, (CUMSUM_ROWS, CUMSUM_ROWS), 0)
    ci = lax.broadcasted_iota(jnp.int32, (CUMSUM_ROWS, CUMSUM_ROWS), 1)
    tri = jnp.where((ri // CHUNK == ci // CHUNK) & (ci <= ri), 1.0, 0.0).astype(BF16)

    def cumsum_block(r0):
        blk = slice(r0, r0 + CUMSUM_ROWS)
        g = g_ref[blk, :]
        g_hi = g.astype(BF16)
        g_lo = (g - g_hi.astype(F32)).astype(BF16)
        b_ref[blk, :] = (jnp.dot(tri, g_hi, preferred_element_type=F32)
                         + jnp.dot(tri, g_lo, preferred_element_type=F32))

    si = lax.broadcasted_iota(jnp.int32, (SUB, CHUNK), 0)
    sj = lax.broadcasted_iota(jnp.int32, (SUB, CHUNK), 1)
    col_block = [sj // SUB == r for r in range(NLAG)]
    diag_block = [col_block[r] & (sj - SUB * r <= si) for r in range(NLAG)]
    lane = lax.broadcasted_iota(jnp.int32, (1, gw), 1)

    head_ids = [(gi, hh) for gi in range(ngroups) for hh in range(hpg)]
    v_of = lambda h, c: v_ref[c * CHUNK:(c + 1) * CHUNK, h * dv:(h + 1) * dv]
    state = {(gi, hh): st_ref[gi * hpg + hh] for gi, hh in head_ids}
    probs, qin, start_state, prepared = {}, {}, {}, {}
    blk = lambda a, r: a[SUB * r:SUB * (r + 1), :]

    def prep(c):
        r0 = c * CHUNK
        rws = slice(r0, r0 + CHUNK)
        for gi in range(ngroups):
            lanes = slice(gi * gw, (gi + 1) * gw)
            b = b_ref[rws, lanes]
            q = q_ref[rws, lanes]
            k = k_ref[rws, lanes]
            ends = {r: b_ref[r0 + SUB * r + SUB - 1:r0 + SUB * r + SUB, lanes] for r in range(NLAG)}
            ends[-1] = jnp.zeros((1, gw), F32)
            rows_of = lambda f: jnp.concatenate(
                [jnp.broadcast_to(f(r), (SUB, gw)) for r in range(NLAG)], axis=0)

            q0 = q * jnp.exp2(b - rows_of(lambda r: ends[r]))
            q1 = q * jnp.exp2(b - rows_of(lambda r: ends[r - 1]))
            k_hat = k * jnp.exp2(rows_of(lambda r: ends[r]) - b)
            lag_rows = [q0.astype(BF16), q1[SUB:, :].astype(BF16)]
            for l in range(2, NLAG):
                lag_rows.append(jnp.concatenate(
                    [blk(q1, r) * jnp.exp2(ends[r - 1] - ends[r - l]) for r in range(l, NLAG)],
                    axis=0).astype(BF16))
            q_lag = jnp.concatenate(lag_rows, axis=0)
            q_in = jnp.concatenate(
                [blk(q1, r) * jnp.exp2(ends[r - 1]) for r in range(NLAG)], axis=0)
            k_dec = jnp.concatenate(
                [blk(k_hat, r) * jnp.exp2(ends[NLAG - 1] - ends[r]) for r in range(NLAG)],
                axis=0).astype(BF16)
            dec = jnp.exp2(ends[NLAG - 1])
            prepared[gi, c] = (q_lag, k_dec, dec)
            for hh in range(hpg):
                if hpg == 1:
                    prepared[gi, hh, c], qin[gi, hh, c] = k_hat.astype(BF16), q_in.astype(BF16)
                else:
                    in_head = (lane >= hh * dk) & (lane < (hh + 1) * dk)
                    prepared[gi, hh, c] = jnp.where(in_head, k_hat, 0.0).astype(BF16)
                    qin[gi, hh, c] = jnp.where(in_head, q_in, 0.0).astype(BF16)

    def scores_and_state(c):
        for gi in range(ngroups):
            q_lag, k_dec, dec = prepared.pop((gi, c))
            for hh in range(hpg):
                h = gi * hpg + hh
                k_h = prepared.pop((gi, hh, c))
                s_all = lax.dot_general(q_lag, k_h, (((1,), (1,)), ((), ())),
                                        preferred_element_type=F32)
                p_rows = []
                for r in range(NLAG):
                    p_r = jnp.where(diag_block[r], blk(s_all, r), 0.0)
                    for l in range(1, r + 1):
                        src = LAG_OFFSETS[l] + SUB * (r - l)
                        p_r = jnp.where(col_block[r - l], s_all[src:src + SUB, :], p_r)
                    p_rows.append(p_r)
                probs[gi, hh, c] = jnp.concatenate(p_rows, axis=0).astype(BF16)
                upd = lax.dot_general(v_of(h, c), k_dec, (((0,), (0,)), ((), ())),
                                      preferred_element_type=F32)
                start_state[gi, hh, c] = state[gi, hh].astype(BF16).T
                state[gi, hh] = state[gi, hh] * dec + upd

    def outputs(c):
        for gi, hh in head_ids:
            h = gi * hpg + hh
            lhs = jnp.concatenate([qin.pop((gi, hh, c)), probs.pop((gi, hh, c))], axis=1)
            rhs = jnp.concatenate([start_state.pop((gi, hh, c)), v_of(h, c)], axis=0)
            o = jnp.dot(lhs, rhs, preferred_element_type=F32)
            o_ref[c * CHUNK:(c + 1) * CHUNK, o_col0 + h * dv:o_col0 + (h + 1) * dv] = o.astype(BF16)

    def store_state():
        for gi, hh in head_ids:
            st_ref[gi * hpg + hh] = state[gi, hh]

    return cumsum_block, prep, scores_and_state, outputs, store_state


_MIXERS = (dict(heads=GLA_HEADS, dk=GLA_DK, dv=GLA_DV), dict(heads=HGRN_HEADS, dk=HGRN_DF, dv=HGRN_DV))


def _mixer_kernel(x_ref, wn_ref, wt_ref, wgk_ref, bgk_ref, lbl_ref, nwa_ref, nwb_ref, *rest,
                  n_cast, steps_per_row):
    cast_in, rest = rest[:n_cast], rest[n_cast:]
    (o_ref, t_ref), rest = rest[:2], rest[2:]
    cast_out, rest = rest[:n_cast], rest[n_cast:]
    (wa_ref, wbraw_ref, wb_ref,
     qa_ref, ka_ref, ga_ref, va_ref, ba_ref, sta_ref,
     qb_ref, kb_ref, gb_ref, vb_ref, bb_ref, stb_ref) = rest

    @pl.when(pl.program_id(0) == 0)
    def _():
        _transpose_projection_weight(wt_ref, wa_ref, wbraw_ref, wb_ref)

    @pl.when(pl.program_id(0) % steps_per_row == 0)
    def _():
        sta_ref[...] = jnp.zeros_like(sta_ref)
        stb_ref[...] = jnp.zeros_like(stb_ref)

    for src, dst in zip(cast_in, cast_out):
        dst[...] = src[...].astype(BF16)

    decays_and_operands, output_gates = _projection_program(
        x_ref, wn_ref, wa_ref, wbraw_ref, wb_ref, wgk_ref, bgk_ref, lbl_ref,
        dict(q=qa_ref, k=ka_ref, g=ga_ref, v=va_ref, gate=(t_ref, 0), gain=nwa_ref),
        dict(q=qb_ref, k=kb_ref, g=gb_ref, v=vb_ref, gate=(t_ref, GLA_V), gain=nwb_ref))
    programs = [
        _mixer_program(qa_ref, ka_ref, ga_ref, va_ref, o_ref, 0, ba_ref, sta_ref, **_MIXERS[0]),
        _mixer_program(qb_ref, kb_ref, gb_ref, vb_ref, o_ref, GLA_V, bb_ref, stb_ref, **_MIXERS[1]),
    ]
    rows = x_ref.shape[0]
    nchunk = rows // CHUNK

    decays_and_operands()
    for cumsum_block, _, _, _, _ in programs:
        for r0 in range(0, rows, CUMSUM_ROWS):
            cumsum_block(r0)
    output_gates()
    for c in range(nchunk):
        for _, prep, _, _, _ in programs:
            prep(c)
    for _, _, scores_and_state, outputs, store_state in programs:
        for c in range(nchunk):
            scores_and_state(c)
            if c >= 1:
                outputs(c - 1)
        outputs(nchunk - 1)
        store_state()


def _mixer_call(x2, wn, w_t, w_gk, bgk, lbl, nwa, nwb, cast_weights, *, batch):
    m = x2.shape[0]
    tm = MIXER_ROWS
    nt = m // batch // tm
    steps = batch * nt
    assert steps * tm == m and tm % CUMSUM_ROWS == 0 and CUMSUM_ROWS % CHUNK == 0
    assert all(w.shape[0] % steps == 0 for w in cast_weights)
    row = lambda width: pl.BlockSpec((tm, width), lambda i: (i, 0))
    full = lambda a: pl.BlockSpec(a.shape, lambda i: (0,) * a.ndim)
    resident = lambda a: pl.BlockSpec(a.shape, lambda i: (0,) * a.ndim,
                                      pipeline_mode=pl.Buffered(1))
    slab = lambda a: pl.BlockSpec((a.shape[0] // steps, a.shape[1]), lambda i: (i, 0))
    out_shapes = (jax.ShapeDtypeStruct((m, GLA_V + HGRN_V), BF16),) * 2
    cast_shapes = tuple(jax.ShapeDtypeStruct(w.shape, BF16) for w in cast_weights)
    scratch = [pltpu.VMEM((D_MODEL, GLA_COLS), BF16), pltpu.VMEM((D_MODEL, LANES), BF16),
               pltpu.VMEM((D_MODEL, HGRN_COLS), BF16)]
    for mix in _MIXERS:
        w, gw = mix["heads"] * mix["dk"], max(mix["dk"], LANES)
        wv = mix["heads"] * mix["dv"]
        scratch += [pltpu.VMEM((tm, w), F32), pltpu.VMEM((tm, w), F32), pltpu.VMEM((tm, w), F32),
                    pltpu.VMEM((tm, wv), BF16),
                    pltpu.VMEM((tm, w), F32), pltpu.VMEM((mix["heads"], mix["dv"], gw), F32)]
    outs = pl.pallas_call(
        functools.partial(_mixer_kernel, n_cast=len(cast_weights), steps_per_row=nt),
        grid=(steps,),
        in_specs=[row(D_MODEL), full(wn), resident(w_t), full(w_gk), full(bgk), full(lbl),
                  full(nwa), full(nwb)] + [slab(w) for w in cast_weights],
        out_specs=tuple(row(s.shape[1]) for s in out_shapes) + tuple(slab(w) for w in cast_weights),
        out_shape=out_shapes + cast_shapes,
        scratch_shapes=scratch,
        compiler_params=pltpu.CompilerParams(
            dimension_semantics=("arbitrary",), vmem_limit_bytes=VMEM_LIMIT_BYTES),
        name="proj_gla_hgrn_mixers",
    )(x2, wn, w_t, w_gk, bgk, lbl, nwa, nwb, *cast_weights)
    return outs[0], outs[1], outs[2:]


def _tail_kernel(x_ref, o_ref, t_ref, w_out_ref, post_mix_ref, pre_mlp_ref, post_mlp_ref,
                 w_up_ref, w_down_ref, out_ref):
    sub = x_ref.shape[0] // TAIL_SUBTILES
    tiles = [slice(s * sub, (s + 1) * sub) for s in range(TAIL_SUBTILES)]
    hs, us = {}, {}

    def out_proj(s):
        r = tiles[s]
        heads = []
        for c0 in range(0, GLA_V + HGRN_V, HEAD_DV):
            o = o_ref[r, c0:c0 + HEAD_DV].astype(F32)
            heads.append(o * lax.rsqrt(jnp.mean(o * o, axis=-1, keepdims=True) + EPS))
        y = (jnp.concatenate(heads, axis=-1) * t_ref[r, :].astype(F32)).astype(BF16)
        mix = jnp.dot(y, w_out_ref[...], preferred_element_type=F32)
        hs[s] = x_ref[r, :] + _rmsnorm(mix, post_mix_ref[...])

    def up(s):
        us[s] = jnp.dot(_rmsnorm(hs[s], pre_mlp_ref[...]).astype(BF16), w_up_ref[...],
                        preferred_element_type=F32)

    def down(s):
        relu = jnp.maximum(us.pop(s), 0.0)
        m = jnp.dot((relu * relu).astype(BF16), w_down_ref[...], preferred_element_type=F32)
        out_ref[tiles[s], :] = hs.pop(s) + _rmsnorm(m, post_mlp_ref[...])

    for s in range(TAIL_SUBTILES):
        out_proj(s)
    for s in range(TAIL_SUBTILES):
        up(s)
        down(s)


def _tail_call(x2, o, t, w_out, post_mix, pre_mlp, post_mlp, w_up, w_down):
    m = x2.shape[0]
    tm = TAIL_ROWS
    assert m % tm == 0 and tm % TAIL_SUBTILES == 0
    row = lambda width: pl.BlockSpec((tm, width), lambda i: (i, 0))
    full = lambda a: pl.BlockSpec(a.shape, lambda i: (0,) * a.ndim)
    resident = lambda a: pl.BlockSpec(a.shape, lambda i: (0,) * a.ndim,
                                      pipeline_mode=pl.Buffered(1))
    return pl.pallas_call(
        _tail_kernel,
        grid=(m // tm,),
        in_specs=[row(D_MODEL), row(GLA_V + HGRN_V), row(GLA_V + HGRN_V), resident(w_out),
                  full(post_mix), full(pre_mlp), full(post_mlp), resident(w_up), resident(w_down)],
        out_specs=row(D_MODEL),
        out_shape=jax.ShapeDtypeStruct((m, D_MODEL), F32),
        compiler_params=pltpu.CompilerParams(
            dimension_semantics=("parallel",), vmem_limit_bytes=VMEM_LIMIT_BYTES),
        name="out_proj_mlp",
    )(x2, o, t, w_out, post_mix, pre_mlp, post_mlp, w_up, w_down)


def kernel(x, w_in, w_gk_up, b_gk, gla_norm_w, hgrn_norm_w, hgrn_lower_bounds, w_out,
           pre_mix_norm, post_mix_norm, pre_mlp_norm, post_mlp_norm, w_up, w_down):
    batch, seq, d = x.shape
    assert d == D_MODEL and w_in.shape[0] == 1 and w_in.shape[2] == GLA_COLS + GLA_GATE_RANK + HGRN_COLS
    x2 = x.reshape(batch * seq, d)
    l = 0
    w_t = jnp.transpose(w_in[l])
    row2 = lambda a: a.reshape(1, -1)

    o, t, (w_out_b, w_up_b, w_down_b) = _mixer_call(
        x2, row2(pre_mix_norm[l]), w_t, w_gk_up[l], row2(b_gk[l]), hgrn_lower_bounds,
        row2(gla_norm_w[l]), row2(hgrn_norm_w[l]), (w_out[l], w_up[l], w_down[l]), batch=batch)
    out = _tail_call(x2, o, t, w_out_b, row2(post_mix_norm[l]), row2(pre_mlp_norm[l]),
                     row2(post_mlp_norm[l]), w_up_b, w_down_b)
    return out.reshape(batch, seq, d)
```
